```python
import math
import jax, jax.numpy as jnp
from jax import lax
import numpy as np

D_MODEL = 1024
BATCH = 4
SEQ = 4096
DEPTH = 4

N_MIXERS = 3
HEAD_DIM = 64
BLOCK = 128
D_FF = 4 * D_MODEL
EPS = 1e-6

A_HEADS = D_MODEL // HEAD_DIM
A_GROUPS = ((128, 1, 6), (512, 4, 5), (2048, 16, 5))
B_HEADS = D_MODEL // (2 * HEAD_DIM)
C_HEADS = D_MODEL // HEAD_DIM
C_KV_HEADS = 4
C_GROUP = C_HEADS // C_KV_HEADS
GRID_W = 64
ROPE_THETA = 10000.0
AXIS_ROPE_DIM = HEAD_DIM // 2
NUM_BUCKETS = 32
REL_MAX_DISTANCE = 1024
REL_BIAS_HEADS = A_HEADS

kernel_name = "hybrid_dilated_diff_axial_gqa_encoder"


def rms_norm(x, g):
    xf = x.astype(jnp.float32)
    y = xf * lax.rsqrt(jnp.mean(xf * xf, axis=-1, keepdims=True) + EPS)
    return (y * g.astype(jnp.float32)).astype(x.dtype)


def t5_bucket(rel):
    nb = NUM_BUCKETS // 2
    max_exact = nb // 2
    side = jnp.where(rel > 0, nb, 0)
    n = jnp.abs(rel)
    nf = jnp.maximum(n, 1).astype(jnp.float32)
    large = max_exact + (jnp.log(nf / max_exact) / math.log(REL_MAX_DISTANCE / max_exact)
                         * (nb - max_exact)).astype(jnp.int32)
    large = jnp.minimum(large, nb - 1)
    return side + jnp.where(n < max_exact, n, large)


def dilated_window_attention(xn, w_qkv, w_o, rel_bias):
    B, S, _ = xn.shape
    nblk = S // BLOCK
    starts = jnp.arange(nblk, dtype=jnp.int32) * BLOCK
    qkv = xn @ w_qkv
    q, k, v = jnp.split(qkv, 3, axis=-1)
    q = q.reshape(B, S, A_HEADS, HEAD_DIM) * (HEAD_DIM ** -0.5)
    k = k.reshape(B, S, A_HEADS, HEAD_DIM)
    v = v.reshape(B, S, A_HEADS, HEAD_DIM)
    outs, glses = [], []
    h0 = 0
    for (win, dil, nh) in A_GROUPS:
        qg, kg, vg = q[:, :, h0:h0 + nh], k[:, :, h0:h0 + nh], v[:, :, h0:h0 + nh]
        n_side = win // (2 * dil)
        offs = jnp.arange(-n_side, n_side + 1, dtype=jnp.int32) * dil
        bias = rel_bias[t5_bucket(offs), h0:h0 + nh].astype(jnp.float32).T
        qb = qg.reshape(B, nblk, BLOCK, nh, HEAD_DIM).transpose(1, 0, 2, 3, 4)

        def block_fn(args, kg=kg, vg=vg, offs=offs, bias=bias):
            q_blk, start = args
            pos = start + jnp.arange(BLOCK, dtype=jnp.int32)[:, None] + offs[None, :]
            valid = (pos >= 0) & (pos < S)
            idx = jnp.clip(pos, 0, S - 1)
            k_sel = kg[:, idx]
            v_sel = vg[:, idx]
            logits = jnp.einsum('bqhd,bqkhd->bhqk', q_blk, k_sel).astype(jnp.float32)
            logits = logits + bias[None, :, None, :]
            logits = jnp.where(valid[None, None], logits, -jnp.inf)
            lse = jax.nn.logsumexp(logits, axis=-1)
            p = jnp.exp(logits - lse[..., None])
            o = jnp.einsum('bhqk,bqkhd->bqhd', p.astype(vg.dtype), v_sel)
            return o, lse

        o, lse = lax.map(block_fn, (qb, starts))
        o = o.transpose(1, 0, 2, 3, 4).reshape(B, S, nh, HEAD_DIM)
        lse = lse.transpose(1, 0, 3, 2).reshape(B, S, nh)
        outs.append(o)
        glses.append(jax.nn.logsumexp(lse, axis=-1) - math.log(nh))
        h0 += nh
    alpha = jax.nn.softmax(jnp.stack(glses, axis=-1), axis=-1)
    n_groups = len(A_GROUPS)
    o = jnp.concatenate(
        [outs[g] * (n_groups * alpha[..., g])[..., None, None].astype(outs[g].dtype)
         for g in range(n_groups)], axis=2)
    return o.reshape(B, S, A_HEADS * HEAD_DIM) @ w_o


def differential_attention(xn, w_qkv, lam, subln_g, w_o, rel_bias, lambda_init):
    B, S, _ = xn.shape
    nblk = S // BLOCK
    starts = jnp.arange(nblk, dtype=jnp.int32) * BLOCK
    qkv = xn @ w_qkv
    q, k, v = jnp.split(qkv, 3, axis=-1)
    q = q.reshape(B, S, B_HEADS, 2, HEAD_DIM) * (HEAD_DIM ** -0.5)
    k = k.reshape(B, S, B_HEADS, 2, HEAD_DIM)
    v = v.reshape(B, S, B_HEADS, 2 * HEAD_DIM)
    lamf = lam.astype(jnp.float32)
    lam_full = (jnp.exp(jnp.sum(lamf[0] * lamf[1])) - jnp.exp(jnp.sum(lamf[2] * lamf[3]))
                + lambda_init)
    keys = jnp.arange(S, dtype=jnp.int32)
    qb = q.reshape(B, nblk, BLOCK, B_HEADS, 2, HEAD_DIM).transpose(1, 0, 2, 3, 4, 5)

    def block_fn(args):
        q_blk, start = args
        rel = keys[None, :] - (start + jnp.arange(BLOCK, dtype=jnp.int32))[:, None]
        bias = rel_bias[t5_bucket(rel)].astype(jnp.float32)
        bias = bias.reshape(BLOCK, S, B_HEADS, 2).transpose(2, 3, 0, 1)
        logits = jnp.einsum('bqhjd,bkhjd->bhjqk', q_blk, k).astype(jnp.float32) + bias[None]
        p = jax.nn.softmax(logits, axis=-1)
        a = p[:, :, 0] - lam_full * p[:, :, 1]
        return jnp.einsum('bhqk,bkhe->bqhe', a.astype(v.dtype), v)

    o = lax.map(block_fn, (qb, starts))
    o = o.transpose(1, 0, 2, 3, 4).reshape(B, S, B_HEADS, 2 * HEAD_DIM)
    o = rms_norm(o, subln_g) * (1.0 - lambda_init)
    return o.reshape(B, S, B_HEADS * 2 * HEAD_DIM) @ w_o


def axial_rope_tables(S):
    n_rows = S // GRID_W
    row = jnp.repeat(jnp.arange(n_rows, dtype=jnp.int32), GRID_W).astype(jnp.float32)
    col = jnp.tile(jnp.arange(GRID_W, dtype=jnp.int32), n_rows).astype(jnp.float32)
    half = AXIS_ROPE_DIM // 2
    inv = ROPE_THETA ** (-jnp.arange(half, dtype=jnp.float32) / half)
    ang = jnp.concatenate([row[:, None] * inv, col[:, None] * inv], axis=-1)
    return jnp.cos(ang), jnp.sin(ang)


def apply_axial_rope(x, cos, sin):
    B, S, H, _ = x.shape
    half = AXIS_ROPE_DIM // 2
    xs = x.astype(jnp.float32).reshape(B, S, H, 2, 2, half)
    x1, x2 = xs[..., 0, :], xs[..., 1, :]
    c = cos.reshape(S, 1, 2, half)
    s = sin.reshape(S, 1, 2, half)
    out = jnp.stack([x1 * c - x2 * s, x2 * c + x1 * s], axis=-2)
    return out.reshape(B, S, H, HEAD_DIM).astype(x.dtype)


def axial_gqa_attention(xn, w_qkv, q_norm_g, k_norm_g, w_o, cos, sin):
    B, S, _ = xn.shape
    nblk = S // BLOCK
    qkv = xn @ w_qkv
    q = qkv[..., :C_HEADS * HEAD_DIM].reshape(B, S, C_HEADS, HEAD_DIM)
    k = qkv[..., C_HEADS * HEAD_DIM:(C_HEADS + C_KV_HEADS) * HEAD_DIM].reshape(B, S, C_KV_HEADS, HEAD_DIM)
    v = qkv[..., (C_HEADS + C_KV_HEADS) * HEAD_DIM:].reshape(B, S, C_KV_HEADS, HEAD_DIM)
    q = apply_axial_rope(rms_norm(q, q_norm_g), cos, sin) * (HEAD_DIM ** -0.5)
    k = apply_axial_rope(rms_norm(k, k_norm_g), cos, sin)
    qb = q.reshape(B, nblk, BLOCK, C_KV_HEADS, C_GROUP, HEAD_DIM).transpose(1, 0, 2, 3, 4, 5)

    def block_fn(q_blk):
        logits = jnp.einsum('bqkgd,bskd->bkgqs', q_blk, k).astype(jnp.float32)
        p = jax.nn.softmax(logits, axis=-1)
        return jnp.einsum('bkgqs,bskd->bqkgd', p.astype(v.dtype), v)

    o = lax.map(block_fn, qb)
    o = o.transpose(1, 0, 2, 3, 4, 5).reshape(B, S, C_HEADS * HEAD_DIM)
    return o @ w_o


def sq_relu_mlp(xn, w_in, w_out):
    h = jax.nn.relu(xn @ w_in)
    return (h * h) @ w_out


def lambda_init_fn(layer_idx):
    return 0.8 - 0.6 * math.exp(-0.3 * layer_idx)


def setup_inputs(seed: int = 0) -> dict:
    key = jax.random.key(seed)
    ks = jax.random.split(key, 24)
    n_a = len(range(0, DEPTH, N_MIXERS))
    n_b = len(range(1, DEPTH, N_MIXERS))
    n_c = len(range(2, DEPTH, N_MIXERS))

    def nrm(k, shape, scale):
        return jax.random.normal(k, shape, jnp.float32) * scale

    d_attn = A_HEADS * HEAD_DIM
    c_qkv = (C_HEADS + 2 * C_KV_HEADS) * HEAD_DIM
    return {
        "x": nrm(ks[0], (BATCH, SEQ, D_MODEL), 1.0),
        "rel_bias": nrm(ks[1], (NUM_BUCKETS, REL_BIAS_HEADS), 0.3),
        "norm_mix_g": 1.0 + nrm(ks[2], (DEPTH, D_MODEL), 0.02),
        "norm_mlp_g": 1.0 + nrm(ks[3], (DEPTH, D_MODEL), 0.02),
        "norm_final_g": 1.0 + nrm(ks[4], (D_MODEL,), 0.02),
        "a_w_qkv": nrm(ks[5], (n_a, D_MODEL, 3 * d_attn), D_MODEL ** -0.5),
        "a_w_o": nrm(ks[6], (n_a, d_attn, D_MODEL), d_attn ** -0.5),
        "b_w_qkv": nrm(ks[7], (n_b, D_MODEL, 3 * d_attn), D_MODEL ** -0.5),
        "b_lambda": nrm(ks[8], (n_b, 4, HEAD_DIM), 0.1),
        "b_subln_g": 1.0 + nrm(ks[9], (n_b, 2 * HEAD_DIM), 0.02),
        "b_w_o": nrm(ks[10], (n_b, d_attn, D_MODEL), d_attn ** -0.5),
        "c_w_qkv": nrm(ks[11], (n_c, D_MODEL, c_qkv), D_MODEL ** -0.5),
        "c_q_norm_g": 1.0 + nrm(ks[12], (n_c, HEAD_DIM), 0.02),
        "c_k_norm_g": 1.0 + nrm(ks[13], (n_c, HEAD_DIM), 0.02),
        "c_w_o": nrm(ks[14], (n_c, C_HEADS * HEAD_DIM, D_MODEL), (C_HEADS * HEAD_DIM) ** -0.5),
        "mlp_w_in": nrm(ks[15], (DEPTH, D_MODEL, D_FF), D_MODEL ** -0.5),
        "mlp_w_out": nrm(ks[16], (DEPTH, D_FF, D_MODEL), D_FF ** -0.5),
    }


def reference(x, rel_bias, norm_mix_g, norm_mlp_g, norm_final_g, a_w_qkv, a_w_o,
              b_w_qkv, b_lambda, b_subln_g, b_w_o, c_w_qkv, c_q_norm_g, c_k_norm_g,
              c_w_o, mlp_w_in, mlp_w_out):
    S = x.shape[1]
    cos, sin = axial_rope_tables(S)
    h = x
    for i in range(DEPTH):
        kind, j = i % N_MIXERS, i // N_MIXERS
        hn = rms_norm(h, norm_mix_g[i])
        if kind == 0:
            mix = dilated_window_attention(hn, a_w_qkv[j], a_w_o[j], rel_bias)
        elif kind == 1:
            mix = differential_attention(hn, b_w_qkv[j], b_lambda[j], b_subln_g[j], b_w_o[j],
                                         rel_bias, lambda_init_fn(i))
        else:
            mix = axial_gqa_attention(hn, c_w_qkv[j], c_q_norm_g[j], c_k_norm_g[j], c_w_o[j],
                                      cos, sin)
        h = h + mix
        h = h + sq_relu_mlp(rms_norm(h, norm_mlp_g[i]), mlp_w_in[i], mlp_w_out[i])
    return rms_norm(h, norm_final_g)
```

```python
import functools
import math

import jax
import jax.numpy as jnp
from jax import lax
from jax.experimental import pallas as pl
from jax.experimental.pallas import tpu as pltpu

F32 = jnp.float32
BF16 = jnp.bfloat16

D_MODEL = 1024
HEAD_DIM = 64
LANES = 128
EPS = 1e-6
NEG = -1e30
N_LAYERS = 4
D_FF = 4 * D_MODEL
A_GROUPS = ((128, 1, 6), (512, 4, 5), (2048, 16, 5))
A_PAD_HEADS = 6
A_COLS = A_PAD_HEADS * HEAD_DIM
A_WIN = 256
A_SIDE = 64
NUM_BUCKETS = 32
REL_MAX_DISTANCE = 1024
B_EBLOCKS = 13
B_ECLIP = 6
C_KV_HEADS = 4
GRID_W = 64
ROPE_THETA = 10000.0

VMEM_LIMIT = 56 * 1024 * 1024


def _cparams(sem):
    return pltpu.CompilerParams(dimension_semantics=sem, vmem_limit_bytes=VMEM_LIMIT)


def _lambda_init(layer_idx):
    return 0.8 - 0.6 * math.exp(-0.3 * layer_idx)


def _rms(x, g):
    ms = jnp.mean(x * x, axis=-1, keepdims=True)
    return x * lax.rsqrt(ms + EPS) * g


def _norm_proj_kernel(x_ref, g_ref, w_ref, cs_ref, o_ref, *, chunk):
    xn = _rms(x_ref[...], g_ref[...]).astype(BF16)
    n = o_ref.shape[-1]
    for c in range(0, n, chunk):
        y = jnp.dot(xn, w_ref[:, c:c + chunk], preferred_element_type=F32)
        o_ref[:, c:c + chunk] = (y * cs_ref[:, c:c + chunk]).astype(o_ref.dtype)


def _norm_proj(x, g, w, col_scale, *, tm=512, chunk=None):
    t, d = x.shape
    n = w.shape[1]
    chunk = chunk or n
    return pl.pallas_call(
        functools.partial(_norm_proj_kernel, chunk=chunk),
        grid=(t // tm,),
        in_specs=[
            pl.BlockSpec((tm, d), lambda i: (i, 0)),
            pl.BlockSpec((1, d), lambda i: (0, 0)),
            pl.BlockSpec((d, n), lambda i: (0, 0)),
            pl.BlockSpec((1, n), lambda i: (0, 0)),
        ],
        out_specs=pl.BlockSpec((tm, n), lambda i: (i, 0)),
        out_shape=jax.ShapeDtypeStruct((t, n), BF16),
        compiler_params=_cparams(("parallel",)),
        name="norm_proj",
    )(x, g, w, col_scale)


def _head_norm_rope(y, gain, cos, sin_signed, lane):
    lo = lane < HEAD_DIM
    y2 = y * y
    s0 = jnp.sum(jnp.where(lo, y2, 0.0), axis=-1, keepdims=True)
    s1 = jnp.sum(jnp.where(lo, 0.0, y2), axis=-1, keepdims=True)
    inv = jnp.where(lo, lax.rsqrt(s0 / HEAD_DIM + EPS), lax.rsqrt(s1 / HEAD_DIM + EPS))
    yn = y * inv * gain
    first = (lane & 31) < 16
    partner = jnp.where(first, pltpu.roll(yn, LANES - 16, 1), pltpu.roll(yn, 16, 1))
    return yn * cos + partner * sin_signed


def _norm_proj_rope_kernel(x_ref, g_ref, w_ref, qg_ref, kg_ref, cos_ref, sin_ref, o_ref,
                           *, n_q, n_k, chunk):
    xn = _rms(x_ref[...], g_ref[...]).astype(BF16)
    n = o_ref.shape[-1]
    tm = x_ref.shape[0]
    lane = lax.broadcasted_iota(jnp.int32, (tm, LANES), 1)
    cos = cos_ref[...]
    sin = sin_ref[...]
    for c in range(0, n, chunk):
        y = jnp.dot(xn, w_ref[:, c:c + chunk], preferred_element_type=F32)
        for b in range(0, chunk, LANES):
            col = c + b
            blk = y[:, b:b + LANES]
            if col < n_q:
                blk = _head_norm_rope(blk, qg_ref[...], cos, sin, lane) * (HEAD_DIM ** -0.5)
            elif col < n_q + n_k:
                blk = _head_norm_rope(blk, kg_ref[...], cos, sin, lane)
            o_ref[:, col:col + LANES] = blk.astype(o_ref.dtype)


def _norm_proj_rope(x, g, w, q_gain, k_gain, cos, sin, *, n_q, n_k, seq, tm=512, chunk=512):
    t, d = x.shape
    n = w.shape[1]
    sblk = seq // tm
    return pl.pallas_call(
        functools.partial(_norm_proj_rope_kernel, n_q=n_q, n_k=n_k, chunk=chunk),
        grid=(t // tm,),
        in_specs=[
            pl.BlockSpec((tm, d), lambda i: (i, 0)),
            pl.BlockSpec((1, d), lambda i: (0, 0)),
            pl.BlockSpec((d, n), lambda i: (0, 0)),
            pl.BlockSpec((1, LANES), lambda i: (0, 0)),
            pl.BlockSpec((1, LANES), lambda i: (0, 0)),
            pl.BlockSpec((tm, LANES), lambda i: (i % sblk, 0)),
            pl.BlockSpec((tm, LANES), lambda i: (i % sblk, 0)),
        ],
        out_specs=pl.BlockSpec((tm, n), lambda i: (i, 0)),
        out_shape=jax.ShapeDtypeStruct((t, n), BF16),
        compiler_params=_cparams(("parallel",)),
        name="norm_proj_rope",
    )(x, g, w, q_gain, k_gain, cos, sin)


def _pair_attn_kernel(*refs, tq, tk, n_kt, mode, lambda_init):
    if mode == "diff":
        q_ref, k_ref, v_ref, gb_ref, lam_ref, sg_ref, o_ref = refs
    else:
        q_ref, k_ref, v_ref, o_ref = refs
    qi = pl.program_id(2)
    q = q_ref[...]
    lane = lax.broadcasted_iota(jnp.int32, (tq, LANES), 1)
    lo = lane < HEAD_DIM
    zero = jnp.zeros_like(q)
    qs = jnp.concatenate([jnp.where(lo, q, zero), jnp.where(lo, zero, q)], axis=0)
    rb_n = tq // LANES
    cb_n = tk // LANES

    def body(kt, carry):
        m, l, acc = carry
        ks = pl.multiple_of(kt * tk, tk)
        k = k_ref[pl.ds(ks, tk), :]
        v = v_ref[pl.ds(ks, tk), :]
        s = lax.dot_general(qs, k, (((1,), (1,)), ((), ())), preferred_element_type=F32)
        if mode == "diff":
            base = kt * cb_n - qi * rb_n
            rows = []
            for j in range(2):
                for rb in range(rb_n):
                    blocks = []
                    for cb in range(cb_n):
                        e = jnp.clip(base + (cb - rb), -B_ECLIP, B_ECLIP) + B_ECLIP
                        blocks.append(gb_ref[j, e])
                    rows.append(jnp.concatenate(blocks, axis=1))
            s = s + jnp.concatenate(rows, axis=0)
        m_new = jnp.maximum(m, jnp.max(s, axis=-1, keepdims=True))
        alpha = jnp.exp(m - m_new)
        p = jnp.exp(s - m_new)
        l = alpha * l + jnp.sum(p, axis=-1, keepdims=True)
        acc = alpha * acc + jnp.dot(p.astype(BF16), v, preferred_element_type=F32)
        return m_new, l, acc

    m0 = jnp.full((2 * tq, 1), NEG, F32)
    l0 = jnp.zeros((2 * tq, 1), F32)
    acc0 = jnp.zeros((2 * tq, LANES), F32)
    _, l, acc = lax.fori_loop(0, n_kt, body, (m0, l0, acc0))
    o = acc / l
    o0, o1 = o[:tq], o[tq:]
    if mode == "diff":
        lam = lam_ref[...]
        lam_full = (jnp.exp(jnp.sum(lam[0:1] * lam[1:2], axis=-1, keepdims=True))
                    - jnp.exp(jnp.sum(lam[2:3] * lam[3:4], axis=-1, keepdims=True))
                    + lambda_init)
        a = o0 - lam_full * o1
        y = _rms(a, sg_ref[...]) * (1.0 - lambda_init)
    else:
        y = jnp.where(lo, o0, o1)
    o_ref[...] = y.astype(o_ref.dtype)


def _pair_attn(qkv, *, n_pairs, k_block, v_block, mode, gb=None, lam=None, subln=None,
               lambda_init=0.0, tq=256, tk=512):
    b, s, _ = qkv.shape
    in_specs = [
        pl.BlockSpec((None, tq, LANES), lambda bi, p, qi: (bi, qi, p)),
        pl.BlockSpec((None, s, LANES), lambda bi, p, qi: (bi, 0, k_block(p))),
        pl.BlockSpec((None, s, LANES), lambda bi, p, qi: (bi, 0, v_block(p))),
    ]
    args = [qkv, qkv, qkv]
    if mode == "diff":
        in_specs += [
            pl.BlockSpec((None, 2, B_EBLOCKS, LANES, LANES), lambda bi, p, qi: (p, 0, 0, 0, 0)),
            pl.BlockSpec((4, HEAD_DIM), lambda bi, p, qi: (0, 0)),
            pl.BlockSpec((1, LANES), lambda bi, p, qi: (0, 0)),
        ]
        args += [gb, lam, subln]
    return pl.pallas_call(
        functools.partial(_pair_attn_kernel, tq=tq, tk=tk, n_kt=s // tk, mode=mode,
                          lambda_init=lambda_init),
        grid=(b, n_pairs, s // tq),
        in_specs=in_specs,
        out_specs=pl.BlockSpec((None, tq, LANES), lambda bi, p, qi: (bi, qi, p)),
        out_shape=jax.ShapeDtypeStruct((b, s, n_pairs * LANES), BF16),
        compiler_params=_cparams(("parallel", "parallel", "arbitrary")),
        name="pair_attn_" + mode,
    )(*args)


def _window_attn_kernel(q_ref, k_ref, v_ref, ga_ref, o_ref, lse_ref, *, tq, sub_len):
    i = pl.program_id(3)
    lane = lax.broadcasted_iota(jnp.int32, (LANES, LANES), 1)
    lo = lane < HEAD_DIM
    for sb in range(tq // LANES):
        q0 = i * tq + sb * LANES
        start = pl.multiple_of(jnp.clip(q0 - A_SIDE, 0, sub_len - A_WIN), A_SIDE)
        variant = jnp.where(q0 == 0, 1, jnp.where(q0 == sub_len - LANES, 2, 0))
        q = q_ref[sb * LANES:(sb + 1) * LANES, :]
        zero = jnp.zeros_like(q)
        qs = jnp.concatenate([jnp.where(lo, q, zero), jnp.where(lo, zero, q)], axis=0)
        kw = k_ref[pl.ds(start, A_WIN), :]
        vw = v_ref[pl.ds(start, A_WIN), :]
        s = lax.dot_general(qs, kw, (((1,), (1,)), ((), ())), preferred_element_type=F32)
        s = s + jnp.concatenate([ga_ref[0, variant], ga_ref[1, variant]], axis=0)
        m = jnp.max(s, axis=-1, keepdims=True)
        p = jnp.exp(s - m)
        l = jnp.sum(p, axis=-1, keepdims=True)
        acc = jnp.dot(p.astype(BF16), vw, preferred_element_type=F32)
        o = acc / l
        lse = m + jnp.log(l)
        rows = slice(sb * LANES, (sb + 1) * LANES)
        o_ref[rows, :] = jnp.where(lo, o[:LANES], o[LANES:]).astype(o_ref.dtype)
        lse_ref[rows, :] = jnp.where(lo, lse[:LANES], lse[LANES:])


def _window_attn(slab, ga, *, tq):
    b, dil, sub_len, _ = slab.shape
    n_p = A_COLS // LANES
    qspec = pl.BlockSpec((None, None, tq, LANES), lambda bi, r, p, i: (bi, r, i, p))
    out_spec = pl.BlockSpec((None, None, tq, LANES), lambda bi, r, p, i: (bi, r, i, p))
    return pl.pallas_call(
        functools.partial(_window_attn_kernel, tq=tq, sub_len=sub_len),
        grid=(b, dil, n_p, sub_len // tq),
        in_specs=[
            qspec,
            pl.BlockSpec((None, None, sub_len, LANES), lambda bi, r, p, i: (bi, r, 0, n_p + p)),
            pl.BlockSpec((None, None, sub_len, LANES), lambda bi, r, p, i: (bi, r, 0, 2 * n_p + p)),
            pl.BlockSpec((None, 2, 3, LANES, A_WIN), lambda bi, r, p, i: (p, 0, 0, 0, 0)),
        ],
        out_specs=[out_spec, out_spec],
        out_shape=[jax.ShapeDtypeStruct((b, dil, sub_len, A_COLS), BF16),
                   jax.ShapeDtypeStruct((b, dil, sub_len, A_COLS), F32)],
        compiler_params=_cparams(("parallel", "parallel", "parallel", "arbitrary")),
        name="window_attn",
    )(slab, slab, slab, ga)


def _post_kernel(*refs, mix_groups, final_norm, ff_chunk):
    refs = list(refs)
    h_ref = refs.pop(0)
    o_ref = refs.pop(0)
    lse_ref = refs.pop(0) if mix_groups else None
    wo_ref, g_ref, win_ref, wout_ref = refs[:4]
    refs = refs[4:]
    gf_ref = refs.pop(0) if final_norm else None
    out_ref = refs[0]

    o = o_ref[...]
    if mix_groups:
        tm, n = o.shape
        lane = lax.broadcasted_iota(jnp.int32, (tm, n), 1)
        grp = (lane >= A_COLS).astype(jnp.int32) + (lane >= 2 * A_COLS).astype(jnp.int32)
        head = (lane - grp * A_COLS) >> 6
        n_heads = jnp.where(grp == 0, A_GROUPS[0][2],
                            jnp.where(grp == 1, A_GROUPS[1][2], A_GROUPS[2][2]))
        real = head < n_heads
        lse = lse_ref[...]
        mx = jnp.max(jnp.where(real, lse, NEG), axis=-1, keepdims=True)
        e = jnp.where(real, jnp.exp(lse - mx), 0.0)
        s_g = [jnp.sum(jnp.where(grp == gi, e, 0.0), axis=-1, keepdims=True)
               / (HEAD_DIM * A_GROUPS[gi][2]) for gi in range(3)]
        tot = s_g[0] + s_g[1] + s_g[2]
        alpha = jnp.where(grp == 0, s_g[0], jnp.where(grp == 1, s_g[1], s_g[2])) / tot
        o = (o.astype(F32) * (len(A_GROUPS) * alpha)).astype(BF16)
    h1 = h_ref[...] + jnp.dot(o, wo_ref[...], preferred_element_type=F32)
    xn = _rms(h1, g_ref[...]).astype(BF16)
    acc = h1
    for c in range(0, D_FF, ff_chunk):
        u = jnp.dot(xn, win_ref[:, c:c + ff_chunk], preferred_element_type=F32)
        u = jnp.maximum(u, 0.0)
        u = (u * u).astype(BF16)
        acc = acc + jnp.dot(u, wout_ref[c:c + ff_chunk, :], preferred_element_type=F32)
    if final_norm:
        acc = _rms(acc, gf_ref[...])
    out_ref[...] = acc


def _post(h, o, w_o, g_mlp, w_in, w_out, *, lse=None, g_final=None, tm=256, ff_chunk=1024):
    t, d = h.shape
    n_o = o.shape[1]
    const = lambda i: (0, 0)
    single = dict(pipeline_mode=pl.Buffered(1))
    in_specs = [pl.BlockSpec((tm, d), lambda i: (i, 0)),
                pl.BlockSpec((tm, n_o), lambda i: (i, 0))]
    args = [h, o]
    if lse is not None:
        in_specs.append(pl.BlockSpec((tm, n_o), lambda i: (i, 0)))
        args.append(lse)
    in_specs += [pl.BlockSpec((n_o, d), const, **single),
                 pl.BlockSpec((1, d), const),
                 pl.BlockSpec((d, D_FF), const, **single),
                 pl.BlockSpec((D_FF, d), const, **single)]
    args += [w_o, g_mlp, w_in, w_out]
    if g_final is not None:
        in_specs.append(pl.BlockSpec((1, d), const))
        args.append(g_final)
    return pl.pallas_call(
        functools.partial(_post_kernel, mix_groups=lse is not None,
                          final_norm=g_final is not None, ff_chunk=ff_chunk),
        grid=(t // tm,),
        in_specs=in_specs,
        out_specs=pl.BlockSpec((tm, d), lambda i: (i, 0)),
        out_shape=jax.ShapeDtypeStruct((t, d), F32),
        compiler_params=_cparams(("parallel",)),
        name="post",
    )(*args)


def _t5_bucket(rel):
    nb = NUM_BUCKETS // 2
    max_exact = nb // 2
    side = jnp.where(rel > 0, nb, 0)
    n = jnp.abs(rel)
    nf = jnp.maximum(n, 1).astype(F32)
    large = max_exact + (jnp.log(nf / max_exact) / math.log(REL_MAX_DISTANCE / max_exact)
                         * (nb - max_exact)).astype(jnp.int32)
    large = jnp.minimum(large, nb - 1)
    return side + jnp.where(n < max_exact, n, large)


def _diff_bias_blocks(rel_bias):
    e = jnp.arange(-B_ECLIP, B_ECLIP + 1, dtype=jnp.int32)[:, None, None]
    r = jnp.arange(LANES, dtype=jnp.int32)[None, :, None]
    c = jnp.arange(LANES, dtype=jnp.int32)[None, None, :]
    bias = rel_bias[_t5_bucket(LANES * e + c - r)].astype(F32)
    bias = jnp.transpose(bias, (3, 0, 1, 2))
    return bias.reshape(8, 2, B_EBLOCKS, LANES, LANES)


def _window_bias_tiles(rel_bias, group):
    _, dil, n_heads = A_GROUPS[group]
    h0 = sum(g[2] for g in A_GROUPS[:group])
    off = jnp.array([-A_SIDE, 0, -LANES], dtype=jnp.int32)[:, None, None]
    r = jnp.arange(LANES, dtype=jnp.int32)[None, :, None]
    c = jnp.arange(A_WIN, dtype=jnp.int32)[None, None, :]
    rel = c + off - r
    valid = jnp.abs(rel) <= A_SIDE
    table = jnp.zeros((NUM_BUCKETS, A_PAD_HEADS), F32)
    table = table.at[:, :n_heads].set(rel_bias[:, h0:h0 + n_heads].astype(F32))
    bias = table[_t5_bucket(rel * dil)]
    bias = jnp.where(valid[..., None], bias, NEG)
    bias = jnp.transpose(bias, (3, 0, 1, 2))
    return bias.reshape(A_PAD_HEADS // 2, 2, 3, LANES, A_WIN)


def _rope_tables(seq):
    n_rows = seq // GRID_W
    row = jnp.repeat(jnp.arange(n_rows, dtype=jnp.int32), GRID_W).astype(F32)
    col = jnp.tile(jnp.arange(GRID_W, dtype=jnp.int32), n_rows).astype(F32)
    half = HEAD_DIM // 4
    inv = ROPE_THETA ** (-jnp.arange(half, dtype=F32) / half)
    ang = jnp.concatenate([row[:, None] * inv, col[:, None] * inv], axis=-1)
    lane = jnp.arange(LANES, dtype=jnp.int32) % HEAD_DIM
    idx = (lane // 32) * half + lane % half
    sign = jnp.where((lane % 32) < half, -1.0, 1.0).astype(F32)
    return jnp.cos(ang)[:, idx], jnp.sin(ang)[:, idx] * sign


def _pad_group_cols(w, axis):
    parts = []
    h0 = 0
    for (_, _, nh) in A_GROUPS:
        sl = [slice(None)] * w.ndim
        sl[axis] = slice(h0 * HEAD_DIM, (h0 + nh) * HEAD_DIM)
        part = w[tuple(sl)]
        if nh < A_PAD_HEADS:
            pad = [(0, 0)] * w.ndim
            pad[axis] = (0, (A_PAD_HEADS - nh) * HEAD_DIM)
            part = jnp.pad(part, pad)
        parts.append(part)
        h0 += nh
    return parts


def _dilated_layer(h, g_mix, w_qkv, w_o, rel_bias, batch, seq):
    d_attn = w_qkv.shape[1] // 3
    wq, wk, wv = (w_qkv[:, i * d_attn:(i + 1) * d_attn] for i in range(3))
    qs, ks, vs = _pad_group_cols(wq, 1), _pad_group_cols(wk, 1), _pad_group_cols(wv, 1)
    w = jnp.concatenate([jnp.concatenate([qs[g], ks[g], vs[g]], axis=1) for g in range(3)],
                        axis=1).astype(BF16)
    slab_cols = 3 * A_COLS
    scale = jnp.tile(jnp.concatenate([jnp.full((A_COLS,), HEAD_DIM ** -0.5, F32),
                                      jnp.ones((2 * A_COLS,), F32)]), 3)[None, :]
    qkv = _norm_proj(h, g_mix, w, scale, chunk=slab_cols)
    outs, lses = [], []
    for g, (_, dil, _) in enumerate(A_GROUPS):
        sub_len = seq // dil
        slab = qkv[:, g * slab_cols:(g + 1) * slab_cols]
        slab = slab.reshape(batch, sub_len, dil, slab_cols).transpose(0, 2, 1, 3)
        o, lse = _window_attn(slab, _window_bias_tiles(rel_bias, g), tq=min(512, sub_len))
        outs.append(o.transpose(0, 2, 1, 3).reshape(batch * seq, A_COLS))
        lses.append(lse.transpose(0, 2, 1, 3).reshape(batch * seq, A_COLS))
    w_o_pad = jnp.concatenate(_pad_group_cols(w_o, 0), axis=0).astype(BF16)
    return jnp.concatenate(outs, axis=1), jnp.concatenate(lses, axis=1), w_o_pad


def kernel(x, rel_bias, norm_mix_g, norm_mlp_g, norm_final_g, a_w_qkv, a_w_o, b_w_qkv,
           b_lambda, b_subln_g, b_w_o, c_w_qkv, c_q_norm_g, c_k_norm_g, c_w_o, mlp_w_in,
           mlp_w_out):
    batch, seq, d = x.shape
    t = batch * seq
    h = x.reshape(t, d)
    cos, sin = _rope_tables(seq)
    for i in range(N_LAYERS):
        kind, j = i % 3, i // 3
        g_mix = norm_mix_g[i][None, :]
        lse = None
        if kind == 0:
            o, lse, w_o = _dilated_layer(h, g_mix, a_w_qkv[j], a_w_o[j], rel_bias, batch, seq)
        elif kind == 1:
            scale = jnp.concatenate([jnp.full((d,), HEAD_DIM ** -0.5, F32),
                                     jnp.ones((2 * d,), F32)])[None, :]
            qkv = _norm_proj(h, g_mix, b_w_qkv[j].astype(BF16), scale, chunk=1024)
            n_pairs = d // LANES
            o = _pair_attn(qkv.reshape(batch, seq, 3 * d), n_pairs=n_pairs,
                           k_block=lambda p: n_pairs + p, v_block=lambda p: 2 * n_pairs + p,
                           mode="diff", gb=_diff_bias_blocks(rel_bias), lam=b_lambda[j],
                           subln=b_subln_g[j][None, :], lambda_init=_lambda_init(i))
            o = o.reshape(t, d)
            w_o = b_w_o[j].astype(BF16)
        else:
            n_q = d
            n_kv = C_KV_HEADS * HEAD_DIM
            wq = c_w_qkv[j][:, :n_q]
            wk = c_w_qkv[j][:, n_q:n_q + n_kv].reshape(d, C_KV_HEADS, 1, HEAD_DIM)
            wv = c_w_qkv[j][:, n_q + n_kv:].reshape(d, C_KV_HEADS, 1, HEAD_DIM)
            wk = jnp.broadcast_to(wk, (d, C_KV_HEADS, 2, HEAD_DIM)).reshape(d, 2 * n_kv)
            wv = jnp.broadcast_to(wv, (d, C_KV_HEADS, 2, HEAD_DIM)).reshape(d, 2 * n_kv)
            w = jnp.concatenate([wq, wk, wv], axis=1).astype(BF16)
            qkv = _norm_proj_rope(h, g_mix, w, jnp.tile(c_q_norm_g[j], 2)[None, :],
                                  jnp.tile(c_k_norm_g[j], 2)[None, :], cos, sin,
                                  n_q=n_q, n_k=2 * n_kv, seq=seq)
            n_pairs = n_q // LANES
            pairs_per_kv = (n_q // n_kv) // 2
            o = _pair_attn(qkv.reshape(batch, seq, n_q + 4 * n_kv), n_pairs=n_pairs,
                           k_block=lambda p: n_pairs + p // pairs_per_kv,
                           v_block=lambda p: n_pairs + C_KV_HEADS + p // pairs_per_kv,
                           mode="gqa")
            o = o.reshape(t, d)
            w_o = c_w_o[j].astype(BF16)
        g_final = norm_final_g[None, :] if i == N_LAYERS - 1 else None
        h = _post(h, o, w_o, norm_mlp_g[i][None, :], mlp_w_in[i].astype(BF16),
                  mlp_w_out[i].astype(BF16), lse=lse, g_final=g_final)
    return h.reshape(batch, seq, d)
```

```python
import functools
import math

import jax
import jax.numpy as jnp
from jax import lax
from jax.experimental import pallas as pl
from jax.experimental.pallas import tpu as pltpu

F32 = jnp.float32
BF16 = jnp.bfloat16

D_MODEL = 1024
HEAD_DIM = 64
LANES = 128
EPS = 1e-6
NEG = -1e30
N_LAYERS = 4
D_FF = 4 * D_MODEL
A_GROUPS = ((128, 1, 6), (512, 4, 5), (2048, 16, 5))
A_PAD_HEADS = 6
A_COLS = A_PAD_HEADS * HEAD_DIM
A_WIN = 256
A_SIDE = 64
NUM_BUCKETS = 32
REL_MAX_DISTANCE = 1024
B_EBLOCKS = 13
B_ECLIP = 6
C_KV_HEADS = 4
GRID_W = 64
ROPE_THETA = 10000.0

LOG2E = math.log2(math.e)
Q_SCALE = HEAD_DIM ** -0.5
Q_SCALE_LOG2 = Q_SCALE * LOG2E

VMEM_LIMIT = 56 * 1024 * 1024


def _cparams(sem):
    return pltpu.CompilerParams(dimension_semantics=sem, vmem_limit_bytes=VMEM_LIMIT)


def _lambda_init(layer_idx):
    return 0.8 - 0.6 * math.exp(-0.3 * layer_idx)


def _rms(x, g):
    ms = jnp.mean(x * x, axis=-1, keepdims=True)
    return x * lax.rsqrt(ms + EPS) * g


def _norm_proj_kernel(x_ref, g_ref, w_ref, cs_ref, o_ref, *, chunk):
    xn = _rms(x_ref[...], g_ref[...]).astype(BF16)
    n = o_ref.shape[-1]
    for c in range(0, n, chunk):
        y = jnp.dot(xn, w_ref[:, c:c + chunk], preferred_element_type=F32)
        o_ref[:, c:c + chunk] = (y * cs_ref[:, c:c + chunk]).astype(o_ref.dtype)


def _norm_proj(x, g, w, col_scale, *, tm=512, chunk=None):
    t, d = x.shape
    n = w.shape[1]
    chunk = chunk or n
    return pl.pallas_call(
        functools.partial(_norm_proj_kernel, chunk=chunk),
        grid=(t // tm,),
        in_specs=[
            pl.BlockSpec((tm, d), lambda i: (i, 0)),
            pl.BlockSpec((1, d), lambda i: (0, 0)),
            pl.BlockSpec((d, n), lambda i: (0, 0)),
            pl.BlockSpec((1, n), lambda i: (0, 0)),
        ],
        out_specs=pl.BlockSpec((tm, n), lambda i: (i, 0)),
        out_shape=jax.ShapeDtypeStruct((t, n), BF16),
        compiler_params=_cparams(("parallel",)),
        name="norm_proj",
    )(x, g, w, col_scale)


def _head_norm_rope(y, gain, cos, sin_signed, lane):
    lo = lane < HEAD_DIM
    y2 = y * y
    s0 = jnp.sum(jnp.where(lo, y2, 0.0), axis=-1, keepdims=True)
    s1 = jnp.sum(jnp.where(lo, 0.0, y2), axis=-1, keepdims=True)
    inv = jnp.where(lo, lax.rsqrt(s0 / HEAD_DIM + EPS), lax.rsqrt(s1 / HEAD_DIM + EPS))
    yn = y * inv * gain
    first = (lane & 31) < 16
    partner = jnp.where(first, pltpu.roll(yn, LANES - 16, 1), pltpu.roll(yn, 16, 1))
    return yn * cos + partner * sin_signed


def _norm_proj_rope_kernel(x_ref, g_ref, w_ref, qg_ref, kg_ref, cos_ref, sin_ref, o_ref,
                           *, n_q, n_k, chunk):
    xn = _rms(x_ref[...], g_ref[...]).astype(BF16)
    n = o_ref.shape[-1]
    tm = x_ref.shape[0]
    lane = lax.broadcasted_iota(jnp.int32, (tm, LANES), 1)
    cos = cos_ref[...]
    sin = sin_ref[...]
    for c in range(0, n, chunk):
        y = jnp.dot(xn, w_ref[:, c:c + chunk], preferred_element_type=F32)
        for b in range(0, chunk, LANES):
            col = c + b
            blk = y[:, b:b + LANES]
            if col < n_q:
                blk = _head_norm_rope(blk, qg_ref[...], cos, sin, lane) * Q_SCALE_LOG2
            elif col < n_q + n_k:
                blk = _head_norm_rope(blk, kg_ref[...], cos, sin, lane)
            o_ref[:, col:col + LANES] = blk.astype(o_ref.dtype)


def _norm_proj_rope(x, g, w, q_gain, k_gain, cos, sin, *, n_q, n_k, seq, tm=512, chunk=512):
    t, d = x.shape
    n = w.shape[1]
    sblk = seq // tm
    return pl.pallas_call(
        functools.partial(_norm_proj_rope_kernel, n_q=n_q, n_k=n_k, chunk=chunk),
        grid=(t // tm,),
        in_specs=[
            pl.BlockSpec((tm, d), lambda i: (i, 0)),
            pl.BlockSpec((1, d), lambda i: (0, 0)),
            pl.BlockSpec((d, n), lambda i: (0, 0)),
            pl.BlockSpec((1, LANES), lambda i: (0, 0)),
            pl.BlockSpec((1, LANES), lambda i: (0, 0)),
            pl.BlockSpec((tm, LANES), lambda i: (i % sblk, 0)),
            pl.BlockSpec((tm, LANES), lambda i: (i % sblk, 0)),
        ],
        out_specs=pl.BlockSpec((tm, n), lambda i: (i, 0)),
        out_shape=jax.ShapeDtypeStruct((t, n), BF16),
        compiler_params=_cparams(("parallel",)),
        name="norm_proj_rope",
    )(x, g, w, q_gain, k_gain, cos, sin)


def _pair_attn_kernel(*refs, tq, tk, n_kt, mode, lambda_init):
    if mode == "diff":
        q_ref, k_ref, v_ref, gb_ref, lam_ref, sg_ref, o_ref = refs
    else:
        q_ref, k_ref, v_ref, o_ref = refs
    qi = pl.program_id(2)
    q = q_ref[...]
    lane = lax.broadcasted_iota(jnp.int32, (tq, LANES), 1)
    lo = lane < HEAD_DIM
    zero = jnp.zeros_like(q)
    qs = jnp.concatenate([jnp.where(lo, q, zero), jnp.where(lo, zero, q)], axis=0)
    rb_n = tq // LANES
    cb_n = tk // LANES

    def body(kt, carry):
        m, l, acc = carry
        k = k_ref[kt * tk:(kt + 1) * tk, :]
        v = v_ref[kt * tk:(kt + 1) * tk, :]
        s = lax.dot_general(qs, k, (((1,), (1,)), ((), ())), preferred_element_type=F32)
        if mode == "diff":
            base = kt * cb_n - qi * rb_n
            rows = []
            for j in range(2):
                for rb in range(rb_n):
                    blocks = []
                    for cb in range(cb_n):
                        e = jnp.clip(base + (cb - rb), -B_ECLIP, B_ECLIP) + B_ECLIP
                        blocks.append(gb_ref[j, e])
                    rows.append(jnp.concatenate(blocks, axis=1))
            s = s + jnp.concatenate(rows, axis=0)
        m_new = jnp.maximum(m, jnp.max(s, axis=-1, keepdims=True))
        alpha = jnp.exp2(m - m_new)
        p = jnp.exp2(s - m_new)
        l = alpha * l + jnp.sum(p, axis=-1, keepdims=True)
        acc = alpha * acc + jnp.dot(p.astype(BF16), v, preferred_element_type=F32)
        return m_new, l, acc

    m0 = jnp.full((2 * tq, 1), NEG, F32)
    l0 = jnp.zeros((2 * tq, 1), F32)
    acc0 = jnp.zeros((2 * tq, LANES), F32)
    carry = (m0, l0, acc0)
    for kt in range(n_kt):
        carry = body(kt, carry)
    _, l, acc = carry
    o = acc / l
    o0, o1 = o[:tq], o[tq:]
    if mode == "diff":
        lam = lam_ref[...]
        lam_full = (jnp.exp(jnp.sum(lam[0:1] * lam[1:2], axis=-1, keepdims=True))
                    - jnp.exp(jnp.sum(lam[2:3] * lam[3:4], axis=-1, keepdims=True))
                    + lambda_init)
        a = o0 - lam_full * o1
        y = _rms(a, sg_ref[...]) * (1.0 - lambda_init)
    else:
        y = jnp.where(lo, o0, o1)
    o_ref[...] = y.astype(o_ref.dtype)


def _pair_attn(qkv, *, n_pairs, k_block, v_block, mode, gb=None, lam=None, subln=None,
               lambda_init=0.0, tq=256, tk=1024):
    b, s, _ = qkv.shape
    in_specs = [
        pl.BlockSpec((None, tq, LANES), lambda bi, p, qi: (bi, qi, p)),
        pl.BlockSpec((None, s, LANES), lambda bi, p, qi: (bi, 0, k_block(p))),
        pl.BlockSpec((None, s, LANES), lambda bi, p, qi: (bi, 0, v_block(p))),
    ]
    args = [qkv, qkv, qkv]
    if mode == "diff":
        in_specs += [
            pl.BlockSpec((None, 2, B_EBLOCKS, LANES, LANES), lambda bi, p, qi: (p, 0, 0, 0, 0)),
            pl.BlockSpec((4, HEAD_DIM), lambda bi, p, qi: (0, 0)),
            pl.BlockSpec((1, LANES), lambda bi, p, qi: (0, 0)),
        ]
        args += [gb, lam, subln]
    return pl.pallas_call(
        functools.partial(_pair_attn_kernel, tq=tq, tk=tk, n_kt=s // tk, mode=mode,
                          lambda_init=lambda_init),
        grid=(b, n_pairs, s // tq),
        in_specs=in_specs,
        out_specs=pl.BlockSpec((None, tq, LANES), lambda bi, p, qi: (bi, qi, p)),
        out_shape=jax.ShapeDtypeStruct((b, s, n_pairs * LANES), BF16),
        compiler_params=_cparams(("parallel", "parallel", "arbitrary")),
        name="pair_attn_" + mode,
    )(*args)


def _window_attn_kernel(q_ref, k_ref, v_ref, ga_ref, o_ref, lse_ref, *, tq, sub_len):
    i = pl.program_id(3)
    lane = lax.broadcasted_iota(jnp.int32, (LANES, LANES), 1)
    lo = lane < HEAD_DIM
    for sb in range(tq // LANES):
        q0 = i * tq + sb * LANES
        start = pl.multiple_of(jnp.clip(q0 - A_SIDE, 0, sub_len - A_WIN), A_SIDE)
        variant = jnp.where(q0 == 0, 1, jnp.where(q0 == sub_len - LANES, 2, 0))
        q = q_ref[sb * LANES:(sb + 1) * LANES, :]
        zero = jnp.zeros_like(q)
        qs = jnp.concatenate([jnp.where(lo, q, zero), jnp.where(lo, zero, q)], axis=0)
        kw = k_ref[pl.ds(start, A_WIN), :]
        vw = v_ref[pl.ds(start, A_WIN), :]
        s = lax.dot_general(qs, kw, (((1,), (1,)), ((), ())), preferred_element_type=F32)
        s = s + jnp.concatenate([ga_ref[0, variant], ga_ref[1, variant]], axis=0)
        m = jnp.max(s, axis=-1, keepdims=True)
        p = jnp.exp(s - m)
        l = jnp.sum(p, axis=-1, keepdims=True)
        acc = jnp.dot(p.astype(BF16), vw, preferred_element_type=F32)
        o = acc / l
        lse = m + jnp.log(l)
        rows = slice(sb * LANES, (sb + 1) * LANES)
        o_ref[rows, :] = jnp.where(lo, o[:LANES], o[LANES:]).astype(o_ref.dtype)
        lse_ref[rows, :] = jnp.where(lo, lse[:LANES], lse[LANES:])


def _window_attn(slab, ga, *, tq):
    b, dil, sub_len, _ = slab.shape
    n_p = A_COLS // LANES
    qspec = pl.BlockSpec((None, None, tq, LANES), lambda bi, r, p, i: (bi, r, i, p))
    out_spec = pl.BlockSpec((None, None, tq, LANES), lambda bi, r, p, i: (bi, r, i, p))
    return pl.pallas_call(
        functools.partial(_window_attn_kernel, tq=tq, sub_len=sub_len),
        grid=(b, dil, n_p, sub_len // tq),
        in_specs=[
            qspec,
            pl.BlockSpec((None, None, sub_len, LANES), lambda bi, r, p, i: (bi, r, 0, n_p + p)),
            pl.BlockSpec((None, None, sub_len, LANES), lambda bi, r, p, i: (bi, r, 0, 2 * n_p + p)),
            pl.BlockSpec((None, 2, 3, LANES, A_WIN), lambda bi, r, p, i: (p, 0, 0, 0, 0)),
        ],
        out_specs=[out_spec, out_spec],
        out_shape=[jax.ShapeDtypeStruct((b, dil, sub_len, A_COLS), BF16),
                   jax.ShapeDtypeStruct((b, dil, sub_len, A_COLS), F32)],
        compiler_params=_cparams(("parallel", "parallel", "parallel", "arbitrary")),
        name="window_attn",
    )(slab, slab, slab, ga)


def _post_kernel(*refs, mix_groups, final_norm, ff_chunk):
    refs = list(refs)
    h_ref = refs.pop(0)
    o_ref = refs.pop(0)
    lse_ref = refs.pop(0) if mix_groups else None
    wo_ref, g_ref, win_ref, wout_ref = refs[:4]
    refs = refs[4:]
    gf_ref = refs.pop(0) if final_norm else None
    out_ref = refs[0]

    o = o_ref[...]
    if mix_groups:
        tm, n = o.shape
        lane = lax.broadcasted_iota(jnp.int32, (tm, n), 1)
        grp = (lane >= A_COLS).astype(jnp.int32) + (lane >= 2 * A_COLS).astype(jnp.int32)
        head = (lane - grp * A_COLS) >> 6
        n_heads = jnp.where(grp == 0, A_GROUPS[0][2],
                            jnp.where(grp == 1, A_GROUPS[1][2], A_GROUPS[2][2]))
        real = head < n_heads
        lse = lse_ref[...]
        mx = jnp.max(jnp.where(real, lse, NEG), axis=-1, keepdims=True)
        e = jnp.where(real, jnp.exp(lse - mx), 0.0)
        s_g = [jnp.sum(jnp.where(grp == gi, e, 0.0), axis=-1, keepdims=True)
               / (HEAD_DIM * A_GROUPS[gi][2]) for gi in range(3)]
        tot = s_g[0] + s_g[1] + s_g[2]
        alpha = jnp.where(grp == 0, s_g[0], jnp.where(grp == 1, s_g[1], s_g[2])) / tot
        o = (o.astype(F32) * (len(A_GROUPS) * alpha)).astype(BF16)
    h1 = h_ref[...] + jnp.dot(o, wo_ref[...], preferred_element_type=F32)
    xn = _rms(h1, g_ref[...]).astype(BF16)
    acc = h1
    for c in range(0, D_FF, ff_chunk):
        u = jnp.dot(xn, win_ref[:, c:c + ff_chunk], preferred_element_type=F32)
        u = jnp.maximum(u, 0.0)
        u = (u * u).astype(BF16)
        acc = acc + jnp.dot(u, wout_ref[c:c + ff_chunk, :], preferred_element_type=F32)
    if final_norm:
        acc = _rms(acc, gf_ref[...])
    out_ref[...] = acc


def _post(h, o, w_o, g_mlp, w_in, w_out, *, lse=None, g_final=None, tm=256, ff_chunk=1024):
    t, d = h.shape
    n_o = o.shape[1]
    const = lambda i: (0, 0)
    single = dict(pipeline_mode=pl.Buffered(1))
    in_specs = [pl.BlockSpec((tm, d), lambda i: (i, 0)),
                pl.BlockSpec((tm, n_o), lambda i: (i, 0))]
    args = [h, o]
    if lse is not None:
        in_specs.append(pl.BlockSpec((tm, n_o), lambda i: (i, 0)))
        args.append(lse)
    in_specs += [pl.BlockSpec((n_o, d), const, **single),
                 pl.BlockSpec((1, d), const),
                 pl.BlockSpec((d, D_FF), const, **single),
                 pl.BlockSpec((D_FF, d), const, **single)]
    args += [w_o, g_mlp, w_in, w_out]
    if g_final is not None:
        in_specs.append(pl.BlockSpec((1, d), const))
        args.append(g_final)
    return pl.pallas_call(
        functools.partial(_post_kernel, mix_groups=lse is not None,
                          final_norm=g_final is not None, ff_chunk=ff_chunk),
        grid=(t // tm,),
        in_specs=in_specs,
        out_specs=pl.BlockSpec((tm, d), lambda i: (i, 0)),
        out_shape=jax.ShapeDtypeStruct((t, d), F32),
        compiler_params=_cparams(("parallel",)),
        name="post",
    )(*args)


def _t5_bucket(rel):
    nb = NUM_BUCKETS // 2
    max_exact = nb // 2
    side = jnp.where(rel > 0, nb, 0)
    n = jnp.abs(rel)
    nf = jnp.maximum(n, 1).astype(F32)
    large = max_exact + (jnp.log(nf / max_exact) / math.log(REL_MAX_DISTANCE / max_exact)
                         * (nb - max_exact)).astype(jnp.int32)
    large = jnp.minimum(large, nb - 1)
    return side + jnp.where(n < max_exact, n, large)


def _table_lookup(bucket, tab_ref, col):
    out = jnp.zeros(bucket.shape, F32)
    for b in range(NUM_BUCKETS):
        out = jnp.where(bucket == b, tab_ref[b, col], out)
    return out


def _diff_bias_kernel(tab_ref, o_ref):
    hj = pl.program_id(0)
    r = lax.broadcasted_iota(jnp.int32, (LANES, LANES), 0)
    c = lax.broadcasted_iota(jnp.int32, (LANES, LANES), 1)
    for e in range(B_EBLOCKS):
        rel = LANES * (e - B_ECLIP) + c - r
        o_ref[e] = _table_lookup(_t5_bucket(rel), tab_ref, hj) * LOG2E


def _diff_bias_blocks(rel_bias):
    n = rel_bias.shape[1]
    out = pl.pallas_call(
        _diff_bias_kernel,
        grid=(n,),
        in_specs=[pl.BlockSpec(memory_space=pltpu.SMEM)],
        out_specs=pl.BlockSpec((None, B_EBLOCKS, LANES, LANES), lambda i: (i, 0, 0, 0)),
        out_shape=jax.ShapeDtypeStruct((n, B_EBLOCKS, LANES, LANES), F32),
        compiler_params=_cparams(("parallel",)),
        name="diff_bias",
    )(rel_bias.astype(F32))
    return out.reshape(n // 2, 2, B_EBLOCKS, LANES, LANES)


A_WINDOW_OFFSETS = (-A_SIDE, 0, -LANES)


def _window_bias_kernel(tab_ref, o_ref):
    hp = pl.program_id(0)
    dil = jnp.where(hp < A_PAD_HEADS, A_GROUPS[0][1],
                    jnp.where(hp < 2 * A_PAD_HEADS, A_GROUPS[1][1], A_GROUPS[2][1]))
    r = lax.broadcasted_iota(jnp.int32, (LANES, A_WIN), 0)
    c = lax.broadcasted_iota(jnp.int32, (LANES, A_WIN), 1)
    for v, off in enumerate(A_WINDOW_OFFSETS):
        rel = c + off - r
        bias = _table_lookup(_t5_bucket(rel * dil), tab_ref, hp)
        o_ref[v] = jnp.where(jnp.abs(rel) <= A_SIDE, bias, NEG)


def _window_bias_tiles(rel_bias):
    cols = _pad_group_cols(rel_bias.astype(F32), 1, unit=1)
    table = jnp.concatenate(cols, axis=1)
    n = table.shape[1]
    out = pl.pallas_call(
        _window_bias_kernel,
        grid=(n,),
        in_specs=[pl.BlockSpec(memory_space=pltpu.SMEM)],
        out_specs=pl.BlockSpec((None, 3, LANES, A_WIN), lambda i: (i, 0, 0, 0)),
        out_shape=jax.ShapeDtypeStruct((n, 3, LANES, A_WIN), F32),
        compiler_params=_cparams(("parallel",)),
        name="window_bias",
    )(table)
    return out.reshape(len(A_GROUPS), A_PAD_HEADS // 2, 2, 3, LANES, A_WIN)


def _rope_tables(seq):
    n_rows = seq // GRID_W
    row = jnp.repeat(jnp.arange(n_rows, dtype=jnp.int32), GRID_W).astype(F32)
    col = jnp.tile(jnp.arange(GRID_W, dtype=jnp.int32), n_rows).astype(F32)
    half = HEAD_DIM // 4
    inv = ROPE_THETA ** (-jnp.arange(half, dtype=F32) / half)
    ang = jnp.concatenate([row[:, None] * inv, col[:, None] * inv], axis=-1)
    lane = jnp.arange(LANES, dtype=jnp.int32) % HEAD_DIM
    idx = (lane // 32) * half + lane % half
    sign = jnp.where((lane % 32) < half, -1.0, 1.0).astype(F32)
    return jnp.cos(ang)[:, idx], jnp.sin(ang)[:, idx] * sign


def _pad_group_cols(w, axis, unit=HEAD_DIM):
    parts = []
    h0 = 0
    for (_, _, nh) in A_GROUPS:
        sl = [slice(None)] * w.ndim
        sl[axis] = slice(h0 * unit, (h0 + nh) * unit)
        part = w[tuple(sl)]
        if nh < A_PAD_HEADS:
            pad = [(0, 0)] * w.ndim
            pad[axis] = (0, (A_PAD_HEADS - nh) * unit)
            part = jnp.pad(part, pad)
        parts.append(part)
        h0 += nh
    return parts


def _dilated_layer(h, g_mix, w_qkv, w_o, bias_tiles, batch, seq):
    d_attn = w_qkv.shape[1] // 3
    wq, wk, wv = (w_qkv[:, i * d_attn:(i + 1) * d_attn] for i in range(3))
    qs, ks, vs = _pad_group_cols(wq, 1), _pad_group_cols(wk, 1), _pad_group_cols(wv, 1)
    w = jnp.concatenate([jnp.concatenate([qs[g], ks[g], vs[g]], axis=1) for g in range(3)],
                        axis=1).astype(BF16)
    slab_cols = 3 * A_COLS
    scale = jnp.tile(jnp.concatenate([jnp.full((A_COLS,), Q_SCALE, F32),
                                      jnp.ones((2 * A_COLS,), F32)]), 3)[None, :]
    qkv = _norm_proj(h, g_mix, w, scale, chunk=slab_cols)
    outs, lses = [], []
    for g, (_, dil, _) in enumerate(A_GROUPS):
        sub_len = seq // dil
        slab = qkv[:, g * slab_cols:(g + 1) * slab_cols]
        slab = slab.reshape(batch, sub_len, dil, slab_cols).transpose(0, 2, 1, 3)
        o, lse = _window_attn(slab, bias_tiles[g], tq=min(512, sub_len))
        outs.append(o.transpose(0, 2, 1, 3).reshape(batch * seq, A_COLS))
        lses.append(lse.transpose(0, 2, 1, 3).reshape(batch * seq, A_COLS))
    w_o_pad = jnp.concatenate(_pad_group_cols(w_o, 0), axis=0).astype(BF16)
    return jnp.concatenate(outs, axis=1), jnp.concatenate(lses, axis=1), w_o_pad


def kernel(x, rel_bias, norm_mix_g, norm_mlp_g, norm_final_g, a_w_qkv, a_w_o, b_w_qkv,
           b_lambda, b_subln_g, b_w_o, c_w_qkv, c_q_norm_g, c_k_norm_g, c_w_o, mlp_w_in,
           mlp_w_out):
    batch, seq, d = x.shape
    t = batch * seq
    h = x.reshape(t, d)
    cos, sin = _rope_tables(seq)
    window_bias = _window_bias_tiles(rel_bias)
    for i in range(N_LAYERS):
        kind, j = i % 3, i // 3
        g_mix = norm_mix_g[i][None, :]
        lse = None
        if kind == 0:
            o, lse, w_o = _dilated_layer(h, g_mix, a_w_qkv[j], a_w_o[j], window_bias, batch, seq)
        elif kind == 1:
            scale = jnp.concatenate([jnp.full((d,), Q_SCALE_LOG2, F32),
                                     jnp.ones((2 * d,), F32)])[None, :]
            qkv = _norm_proj(h, g_mix, b_w_qkv[j].astype(BF16), scale, chunk=1024)
            n_pairs = d // LANES
            o = _pair_attn(qkv.reshape(batch, seq, 3 * d), n_pairs=n_pairs,
                           k_block=lambda p: n_pairs + p, v_block=lambda p: 2 * n_pairs + p,
                           mode="diff", gb=_diff_bias_blocks(rel_bias), lam=b_lambda[j],
                           subln=b_subln_g[j][None, :], lambda_init=_lambda_init(i))
            o = o.reshape(t, d)
            w_o = b_w_o[j].astype(BF16)
        else:
            n_q = d
            n_kv = C_KV_HEADS * HEAD_DIM
            wq = c_w_qkv[j][:, :n_q]
            wk = c_w_qkv[j][:, n_q:n_q + n_kv].reshape(d, C_KV_HEADS, 1, HEAD_DIM)
            wv = c_w_qkv[j][:, n_q + n_kv:].reshape(d, C_KV_HEADS, 1, HEAD_DIM)
            wk = jnp.broadcast_to(wk, (d, C_KV_HEADS, 2, HEAD_DIM)).reshape(d, 2 * n_kv)
            wv = jnp.broadcast_to(wv, (d, C_KV_HEADS, 2, HEAD_DIM)).reshape(d, 2 * n_kv)
            w = jnp.concatenate([wq, wk, wv], axis=1).astype(BF16)
            qkv = _norm_proj_rope(h, g_mix, w, jnp.tile(c_q_norm_g[j], 2)[None, :],
                                  jnp.tile(c_k_norm_g[j], 2)[None, :], cos, sin,
                                  n_q=n_q, n_k=2 * n_kv, seq=seq)
            n_pairs = n_q // LANES
            pairs_per_kv = (n_q // n_kv) // 2
            o = _pair_attn(qkv.reshape(batch, seq, n_q + 4 * n_kv), n_pairs=n_pairs,
                           k_block=lambda p: n_pairs + p // pairs_per_kv,
                           v_block=lambda p: n_pairs + C_KV_HEADS + p // pairs_per_kv,
                           mode="gqa")
            o = o.reshape(t, d)
            w_o = c_w_o[j].astype(BF16)
        g_final = norm_final_g[None, :] if i == N_LAYERS - 1 else None
        h = _post(h, o, w_o, norm_mlp_g[i][None, :], mlp_w_in[i].astype(BF16),
                  mlp_w_out[i].astype(BF16), lse=lse, g_final=g_final)
    return h.reshape(batch, seq, d)
```

```python
import functools
import math

import jax
import jax.numpy as jnp
from jax import lax
from jax.experimental import pallas as pl
from jax.experimental.pallas import tpu as pltpu

F32 = jnp.float32
BF16 = jnp.bfloat16

D_MODEL = 1024
HEAD_DIM = 64
LANES = 128
EPS = 1e-6
NEG = -1e30
N_LAYERS = 4
D_FF = 4 * D_MODEL
A_GROUPS = ((128, 1, 6), (512, 4, 5), (2048, 16, 5))
A_PAD_HEADS = 6
A_COLS = A_PAD_HEADS * HEAD_DIM
A_WIN = 256
A_SIDE = 64
NUM_BUCKETS = 32
REL_MAX_DISTANCE = 1024
B_EBLOCKS = 13
B_ECLIP = 6
C_KV_HEADS = 4
GRID_W = 64
ROPE_THETA = 10000.0

LOG2E = math.log2(math.e)
Q_SCALE = HEAD_DIM ** -0.5
Q_SCALE_LOG2 = Q_SCALE * LOG2E

VMEM_LIMIT = 56 * 1024 * 1024


def _cparams(sem):
    return pltpu.CompilerParams(dimension_semantics=sem, vmem_limit_bytes=VMEM_LIMIT)


def _lambda_init(layer_idx):
    return 0.8 - 0.6 * math.exp(-0.3 * layer_idx)


def _rms(x, g):
    ms = jnp.mean(x * x, axis=-1, keepdims=True)
    return x * lax.rsqrt(ms + EPS) * g


def _norm_proj_kernel(x_ref, g_ref, w_ref, cs_ref, o_ref, *, chunk):
    xn = _rms(x_ref[...], g_ref[...]).astype(BF16)
    n = o_ref.shape[-1]
    for c in range(0, n, chunk):
        y = jnp.dot(xn, w_ref[:, c:c + chunk], preferred_element_type=F32)
        o_ref[:, c:c + chunk] = (y * cs_ref[:, c:c + chunk]).astype(o_ref.dtype)


def _norm_proj(x, g, w, col_scale, *, tm=512, chunk=None):
    t, d = x.shape
    n = w.shape[1]
    chunk = chunk or n
    return pl.pallas_call(
        functools.partial(_norm_proj_kernel, chunk=chunk),
        grid=(t // tm,),
        in_specs=[
            pl.BlockSpec((tm, d), lambda i: (i, 0)),
            pl.BlockSpec((1, d), lambda i: (0, 0)),
            pl.BlockSpec((d, n), lambda i: (0, 0)),
            pl.BlockSpec((1, n), lambda i: (0, 0)),
        ],
        out_specs=pl.BlockSpec((tm, n), lambda i: (i, 0)),
        out_shape=jax.ShapeDtypeStruct((t, n), BF16),
        compiler_params=_cparams(("parallel",)),
        name="norm_proj",
    )(x, g, w, col_scale)


def _norm_proj_groups_kernel(x_ref, g_ref, w_ref, cs_ref, o0_ref, o1_ref, o2_ref, y_scr):
    xn = _rms(x_ref[...], g_ref[...]).astype(BF16)
    tm = x_ref.shape[0]
    n = 3 * A_COLS
    for gi, o_ref in enumerate((o0_ref, o1_ref, o2_ref)):
        dil = A_GROUPS[gi][1]
        cols = slice(gi * n, (gi + 1) * n)
        y = jnp.dot(xn, w_ref[:, cols], preferred_element_type=F32) * cs_ref[:, cols]
        if dil == 1:
            o_ref[0] = y.astype(o_ref.dtype)
        else:
            for cb in range(n // LANES):
                y_scr[cb] = y[:, cb * LANES:(cb + 1) * LANES]
            for r in range(dil):
                for cb in range(n // LANES):
                    o_ref[r, :, cb * LANES:(cb + 1) * LANES] = (
                        y_scr[cb, pl.ds(r, tm // dil, stride=dil), :].astype(o_ref.dtype))


def _norm_proj_groups(x, g, w, col_scale, *, batch, seq, tm=512):
    t, d = x.shape
    n = 3 * A_COLS
    spb = seq // tm
    const = lambda i: (0, 0)
    out_specs, out_shape = [], []
    for (_, dil, _) in A_GROUPS:
        out_specs.append(pl.BlockSpec((None, dil, tm // dil, n),
                                      lambda i: (i // spb, 0, i % spb, 0)))
        out_shape.append(jax.ShapeDtypeStruct((batch, dil, seq // dil, n), BF16))
    return pl.pallas_call(
        _norm_proj_groups_kernel,
        grid=(t // tm,),
        in_specs=[
            pl.BlockSpec((tm, d), lambda i: (i, 0)),
            pl.BlockSpec((1, d), const),
            pl.BlockSpec((d, 3 * n), const),
            pl.BlockSpec((1, 3 * n), const),
        ],
        out_specs=out_specs,
        out_shape=out_shape,
        scratch_shapes=[pltpu.VMEM((n // LANES, tm, LANES), F32)],
        compiler_params=_cparams(("parallel",)),
        name="norm_proj_groups",
    )(x, g, w, col_scale)


def _head_norm_rope(y, gain, cos, sin_signed, lane):
    lo = lane < HEAD_DIM
    y2 = y * y
    s0 = jnp.sum(jnp.where(lo, y2, 0.0), axis=-1, keepdims=True)
    s1 = jnp.sum(jnp.where(lo, 0.0, y2), axis=-1, keepdims=True)
    inv = jnp.where(lo, lax.rsqrt(s0 / HEAD_DIM + EPS), lax.rsqrt(s1 / HEAD_DIM + EPS))
    yn = y * inv * gain
    first = (lane & 31) < 16
    partner = jnp.where(first, pltpu.roll(yn, LANES - 16, 1), pltpu.roll(yn, 16, 1))
    return yn * cos + partner * sin_signed


def _norm_proj_rope_kernel(x_ref, g_ref, w_ref, qg_ref, kg_ref, cos_ref, sin_ref, o_ref,
                           *, n_q, n_k, chunk):
    xn = _rms(x_ref[...], g_ref[...]).astype(BF16)
    n = o_ref.shape[-1]
    tm = x_ref.shape[0]
    lane = lax.broadcasted_iota(jnp.int32, (tm, LANES), 1)
    cos = cos_ref[...]
    sin = sin_ref[...]
    for c in range(0, n, chunk):
        y = jnp.dot(xn, w_ref[:, c:c + chunk], preferred_element_type=F32)
        for b in range(0, chunk, LANES):
            col = c + b
            blk = y[:, b:b + LANES]
            if col < n_q:
                blk = _head_norm_rope(blk, qg_ref[...], cos, sin, lane) * Q_SCALE_LOG2
            elif col < n_q + n_k:
                blk = _head_norm_rope(blk, kg_ref[...], cos, sin, lane)
            o_ref[:, col:col + LANES] = blk.astype(o_ref.dtype)


def _norm_proj_rope(x, g, w, q_gain, k_gain, cos, sin, *, n_q, n_k, seq, tm=512, chunk=512):
    t, d = x.shape
    n = w.shape[1]
    sblk = seq // tm
    return pl.pallas_call(
        functools.partial(_norm_proj_rope_kernel, n_q=n_q, n_k=n_k, chunk=chunk),
        grid=(t // tm,),
        in_specs=[
            pl.BlockSpec((tm, d), lambda i: (i, 0)),
            pl.BlockSpec((1, d), lambda i: (0, 0)),
            pl.BlockSpec((d, n), lambda i: (0, 0)),
            pl.BlockSpec((1, LANES), lambda i: (0, 0)),
            pl.BlockSpec((1, LANES), lambda i: (0, 0)),
            pl.BlockSpec((tm, LANES), lambda i: (i % sblk, 0)),
            pl.BlockSpec((tm, LANES), lambda i: (i % sblk, 0)),
        ],
        out_specs=pl.BlockSpec((tm, n), lambda i: (i, 0)),
        out_shape=jax.ShapeDtypeStruct((t, n), BF16),
        compiler_params=_cparams(("parallel",)),
        name="norm_proj_rope",
    )(x, g, w, q_gain, k_gain, cos, sin)


def _pair_attn_kernel(*refs, tq, tk, n_kt, mode, lambda_init):
    if mode == "diff":
        q_ref, k_ref, v_ref, gb_ref, lam_ref, sg_ref, o_ref = refs
    else:
        q_ref, k_ref, v_ref, o_ref = refs
    qi = pl.program_id(2)
    q = q_ref[...]
    lane = lax.broadcasted_iota(jnp.int32, (tq, LANES), 1)
    lo = lane < HEAD_DIM
    zero = jnp.zeros_like(q)
    qs = jnp.concatenate([jnp.where(lo, q, zero), jnp.where(lo, zero, q)], axis=0)
    rb_n = tq // LANES
    cb_n = tk // LANES

    def body(kt, carry):
        m, l, acc = carry
        k = k_ref[kt * tk:(kt + 1) * tk, :]
        v = v_ref[kt * tk:(kt + 1) * tk, :]
        s = lax.dot_general(qs, k, (((1,), (1,)), ((), ())), preferred_element_type=F32)
        if mode == "diff":
            base = kt * cb_n - qi * rb_n
            rows = []
            for j in range(2):
                for rb in range(rb_n):
                    blocks = []
                    for cb in range(cb_n):
                        e = jnp.clip(base + (cb - rb), -B_ECLIP, B_ECLIP) + B_ECLIP
                        blocks.append(gb_ref[j, e])
                    rows.append(jnp.concatenate(blocks, axis=1))
            s = s + jnp.concatenate(rows, axis=0)
        m_new = jnp.maximum(m, jnp.max(s, axis=-1, keepdims=True))
        alpha = jnp.exp2(m - m_new)
        p = jnp.exp2(s - m_new)
        l = alpha * l + jnp.sum(p, axis=-1, keepdims=True)
        acc = alpha * acc + jnp.dot(p.astype(BF16), v, preferred_element_type=F32)
        return m_new, l, acc

    m0 = jnp.full((2 * tq, 1), NEG, F32)
    l0 = jnp.zeros((2 * tq, 1), F32)
    acc0 = jnp.zeros((2 * tq, LANES), F32)
    carry = (m0, l0, acc0)
    for kt in range(n_kt):
        carry = body(kt, carry)
    _, l, acc = carry
    o = acc / l
    o0, o1 = o[:tq], o[tq:]
    if mode == "diff":
        lam = lam_ref[...]
        lam_full = (jnp.exp(jnp.sum(lam[0:1] * lam[1:2], axis=-1, keepdims=True))
                    - jnp.exp(jnp.sum(lam[2:3] * lam[3:4], axis=-1, keepdims=True))
                    + lambda_init)
        a = o0 - lam_full * o1
        y = _rms(a, sg_ref[...]) * (1.0 - lambda_init)
    else:
        y = jnp.where(lo, o0, o1)
    o_ref[...] = y.astype(o_ref.dtype)


def _pair_attn(qkv, *, n_pairs, k_block, v_block, mode, gb=None, lam=None, subln=None,
               lambda_init=0.0, tq=256, tk=1024):
    b, s, _ = qkv.shape
    in_specs = [
        pl.BlockSpec((None, tq, LANES), lambda bi, p, qi: (bi, qi, p)),
        pl.BlockSpec((None, s, LANES), lambda bi, p, qi: (bi, 0, k_block(p))),
        pl.BlockSpec((None, s, LANES), lambda bi, p, qi: (bi, 0, v_block(p))),
    ]
    args = [qkv, qkv, qkv]
    if mode == "diff":
        in_specs += [
            pl.BlockSpec((None, 2, B_EBLOCKS, LANES, LANES), lambda bi, p, qi: (p, 0, 0, 0, 0)),
            pl.BlockSpec((4, HEAD_DIM), lambda bi, p, qi: (0, 0)),
            pl.BlockSpec((1, LANES), lambda bi, p, qi: (0, 0)),
        ]
        args += [gb, lam, subln]
    return pl.pallas_call(
        functools.partial(_pair_attn_kernel, tq=tq, tk=tk, n_kt=s // tk, mode=mode,
                          lambda_init=lambda_init),
        grid=(b, n_pairs, s // tq),
        in_specs=in_specs,
        out_specs=pl.BlockSpec((None, tq, LANES), lambda bi, p, qi: (bi, qi, p)),
        out_shape=jax.ShapeDtypeStruct((b, s, n_pairs * LANES), BF16),
        compiler_params=_cparams(("parallel", "parallel", "arbitrary")),
        name="pair_attn_" + mode,
    )(*args)


def _window_attn_kernel(slab_ref, ga_ref, o_ref, lse_ref, *, sub_len):
    lane = lax.broadcasted_iota(jnp.int32, (LANES, LANES), 1)
    lo = lane < HEAD_DIM
    n_p = A_COLS // LANES

    def sub_block(sb):
        static = isinstance(sb, int)
        q0 = sb * LANES
        if static:
            start = min(max(q0 - A_SIDE, 0), sub_len - A_WIN)
            variant = 1 if q0 == 0 else (2 if q0 == sub_len - LANES else 0)
            qrows, wrows = slice(q0, q0 + LANES), slice(start, start + A_WIN)
        else:
            q0 = pl.multiple_of(q0, LANES)
            start = pl.multiple_of(jnp.clip(q0 - A_SIDE, 0, sub_len - A_WIN), A_SIDE)
            variant = jnp.where(q0 == 0, 1, jnp.where(q0 == sub_len - LANES, 2, 0))
            qrows, wrows = pl.ds(q0, LANES), pl.ds(start, A_WIN)
        for p in range(n_p):
            cols = slice(p * LANES, (p + 1) * LANES)
            q = slab_ref[qrows, cols]
            kw = slab_ref[wrows, A_COLS + p * LANES:A_COLS + (p + 1) * LANES]
            vw = slab_ref[wrows, 2 * A_COLS + p * LANES:2 * A_COLS + (p + 1) * LANES]
            zero = jnp.zeros_like(q)
            qs = jnp.concatenate([jnp.where(lo, q, zero), jnp.where(lo, zero, q)], axis=0)
            s = lax.dot_general(qs, kw, (((1,), (1,)), ((), ())), preferred_element_type=F32)
            s = s + jnp.concatenate([ga_ref[p, 0, variant], ga_ref[p, 1, variant]], axis=0)
            m = jnp.max(s, axis=-1, keepdims=True)
            e = jnp.exp(s - m)
            l = jnp.sum(e, axis=-1, keepdims=True)
            acc = jnp.dot(e.astype(BF16), vw, preferred_element_type=F32)
            o = acc / l
            lse = m + jnp.log(l)
            o_ref[qrows, cols] = jnp.where(lo, o[:LANES], o[LANES:]).astype(o_ref.dtype)
            lse_ref[qrows, cols] = jnp.where(lo, lse[:LANES], lse[LANES:])

    n_sb = sub_len // LANES
    if n_sb <= 2:
        for sb in range(n_sb):
            sub_block(sb)
    else:
        def body(i, carry):
            sub_block(2 * i)
            sub_block(2 * i + 1)
            return carry
        lax.fori_loop(0, n_sb // 2, body, 0)


def _window_attn(slab, ga):
    b, dil, sub_len, n = slab.shape
    out_spec = pl.BlockSpec((None, None, sub_len, A_COLS), lambda bi, r: (bi, r, 0, 0))
    return pl.pallas_call(
        functools.partial(_window_attn_kernel, sub_len=sub_len),
        grid=(b, dil),
        in_specs=[
            pl.BlockSpec((None, None, sub_len, n), lambda bi, r: (bi, r, 0, 0)),
            pl.BlockSpec(ga.shape, lambda bi, r: (0,) * ga.ndim),
        ],
        out_specs=[out_spec, out_spec],
        out_shape=[jax.ShapeDtypeStruct((b, dil, sub_len, A_COLS), BF16),
                   jax.ShapeDtypeStruct((b, dil, sub_len, A_COLS), F32)],
        compiler_params=_cparams(("parallel", "parallel")),
        name="window_attn",
    )(slab, ga)


def _post_kernel(*refs, mix_groups, final_norm, ff_chunk):
    refs = list(refs)
    h_ref = refs.pop(0)
    if mix_groups:
        og_refs = [refs.pop(0) for _ in A_GROUPS]
        lg_refs = [refs.pop(0) for _ in A_GROUPS]
    else:
        o_ref = refs.pop(0)
    wo_ref, g_ref, win_ref, wout_ref = refs[:4]
    refs = refs[4:]
    gf_ref = refs.pop(0) if final_norm else None
    out_ref = refs.pop(0)

    if mix_groups:
        o_scr, lse_scr = refs
        n_cb, tm, _ = o_scr.shape
        n = n_cb * LANES
        cb_per_group = A_COLS // LANES
        for gi, (_, dil, _) in enumerate(A_GROUPS):
            for r in range(dil):
                rows = pl.ds(r, tm // dil, stride=dil) if dil > 1 else slice(None)
                for cb in range(cb_per_group):
                    cols = slice(cb * LANES, (cb + 1) * LANES)
                    o_scr[gi * cb_per_group + cb, rows, :] = og_refs[gi][r, :, cols].astype(F32)
                    lse_scr[gi * cb_per_group + cb, rows, :] = lg_refs[gi][r, :, cols]
        o = jnp.concatenate([o_scr[cb] for cb in range(n_cb)], axis=1)
        lse = jnp.concatenate([lse_scr[cb] for cb in range(n_cb)], axis=1)
        lane = lax.broadcasted_iota(jnp.int32, (tm, n), 1)
        grp = (lane >= A_COLS).astype(jnp.int32) + (lane >= 2 * A_COLS).astype(jnp.int32)
        head = (lane - grp * A_COLS) >> 6
        n_heads = jnp.where(grp == 0, A_GROUPS[0][2],
                            jnp.where(grp == 1, A_GROUPS[1][2], A_GROUPS[2][2]))
        real = head < n_heads
        mx = jnp.max(jnp.where(real, lse, NEG), axis=-1, keepdims=True)
        e = jnp.where(real, jnp.exp(lse - mx), 0.0)
        s_g = [jnp.sum(jnp.where(grp == gi, e, 0.0), axis=-1, keepdims=True)
               / (HEAD_DIM * A_GROUPS[gi][2]) for gi in range(3)]
        tot = s_g[0] + s_g[1] + s_g[2]
        alpha = jnp.where(grp == 0, s_g[0], jnp.where(grp == 1, s_g[1], s_g[2])) / tot
        o = (o * (len(A_GROUPS) * alpha)).astype(BF16)
    else:
        o = o_ref[...]
    h1 = h_ref[...] + jnp.dot(o, wo_ref[...], preferred_element_type=F32)
    xn = _rms(h1, g_ref[...]).astype(BF16)
    acc = h1
    for c in range(0, D_FF, ff_chunk):
        u = jnp.dot(xn, win_ref[:, c:c + ff_chunk], preferred_element_type=F32)
        u = jnp.maximum(u, 0.0)
        u = (u * u).astype(BF16)
        acc = acc + jnp.dot(u, wout_ref[c:c + ff_chunk, :], preferred_element_type=F32)
    if final_norm:
        acc = _rms(acc, gf_ref[...])
    out_ref[...] = acc


def _post(h, o, w_o, g_mlp, w_in, w_out, *, lse=None, g_final=None, seq=None, tm=256,
          ff_chunk=1024):
    t, d = h.shape
    const = lambda i: (0, 0)
    single = dict(pipeline_mode=pl.Buffered(1))
    in_specs = [pl.BlockSpec((tm, d), lambda i: (i, 0))]
    args = [h]
    scratch = []
    if lse is not None:
        n_o = len(A_GROUPS) * A_COLS
        spb = seq // tm
        for arrs in (o, lse):
            for arr, (_, dil, _) in zip(arrs, A_GROUPS):
                in_specs.append(pl.BlockSpec((None, dil, tm // dil, A_COLS),
                                             lambda i: (i // spb, 0, i % spb, 0)))
                args.append(arr)
        scratch = [pltpu.VMEM((n_o // LANES, tm, LANES), F32)] * 2
    else:
        n_o = o.shape[1]
        in_specs.append(pl.BlockSpec((tm, n_o), lambda i: (i, 0)))
        args.append(o)
    in_specs += [pl.BlockSpec((n_o, d), const, **single),
                 pl.BlockSpec((1, d), const),
                 pl.BlockSpec((d, D_FF), const, **single),
                 pl.BlockSpec((D_FF, d), const, **single)]
    args += [w_o, g_mlp, w_in, w_out]
    if g_final is not None:
        in_specs.append(pl.BlockSpec((1, d), const))
        args.append(g_final)
    return pl.pallas_call(
        functools.partial(_post_kernel, mix_groups=lse is not None,
                          final_norm=g_final is not None, ff_chunk=ff_chunk),
        grid=(t // tm,),
        in_specs=in_specs,
        out_specs=pl.BlockSpec((tm, d), lambda i: (i, 0)),
        out_shape=jax.ShapeDtypeStruct((t, d), F32),
        scratch_shapes=scratch,
        compiler_params=_cparams(("parallel",)),
        name="post",
    )(*args)


def _t5_bucket(rel):
    nb = NUM_BUCKETS // 2
    max_exact = nb // 2
    side = jnp.where(rel > 0, nb, 0)
    n = jnp.abs(rel)
    nf = jnp.maximum(n, 1).astype(F32)
    large = max_exact + (jnp.log(nf / max_exact) / math.log(REL_MAX_DISTANCE / max_exact)
                         * (nb - max_exact)).astype(jnp.int32)
    large = jnp.minimum(large, nb - 1)
    return side + jnp.where(n < max_exact, n, large)


def _table_lookup(bucket, tab_ref, col):
    out = jnp.zeros(bucket.shape, F32)
    for b in range(NUM_BUCKETS):
        out = jnp.where(bucket == b, tab_ref[b, col], out)
    return out


def _diff_bias_kernel(tab_ref, o_ref):
    hj = pl.program_id(0)
    r = lax.broadcasted_iota(jnp.int32, (LANES, LANES), 0)
    c = lax.broadcasted_iota(jnp.int32, (LANES, LANES), 1)
    for e in range(B_EBLOCKS):
        rel = LANES * (e - B_ECLIP) + c - r
        o_ref[e] = _table_lookup(_t5_bucket(rel), tab_ref, hj) * LOG2E


def _diff_bias_blocks(rel_bias):
    n = rel_bias.shape[1]
    out = pl.pallas_call(
        _diff_bias_kernel,
        grid=(n,),
        in_specs=[pl.BlockSpec(memory_space=pltpu.SMEM)],
        out_specs=pl.BlockSpec((None, B_EBLOCKS, LANES, LANES), lambda i: (i, 0, 0, 0)),
        out_shape=jax.ShapeDtypeStruct((n, B_EBLOCKS, LANES, LANES), F32),
        compiler_params=_cparams(("parallel",)),
        name="diff_bias",
    )(rel_bias.astype(F32))
    return out.reshape(n // 2, 2, B_EBLOCKS, LANES, LANES)


A_WINDOW_OFFSETS = (-A_SIDE, 0, -LANES)


def _window_bias_kernel(tab_ref, o_ref):
    hp = pl.program_id(0)
    dil = jnp.where(hp < A_PAD_HEADS, A_GROUPS[0][1],
                    jnp.where(hp < 2 * A_PAD_HEADS, A_GROUPS[1][1], A_GROUPS[2][1]))
    r = lax.broadcasted_iota(jnp.int32, (LANES, A_WIN), 0)
    c = lax.broadcasted_iota(jnp.int32, (LANES, A_WIN), 1)
    for v, off in enumerate(A_WINDOW_OFFSETS):
        rel = c + off - r
        bias = _table_lookup(_t5_bucket(rel * dil), tab_ref, hp)
        o_ref[v] = jnp.where(jnp.abs(rel) <= A_SIDE, bias, NEG)


def _window_bias_tiles(rel_bias):
    cols = _pad_group_cols(rel_bias.astype(F32), 1, unit=1)
    table = jnp.concatenate(cols, axis=1)
    n = table.shape[1]
    out = pl.pallas_call(
        _window_bias_kernel,
        grid=(n,),
        in_specs=[pl.BlockSpec(memory_space=pltpu.SMEM)],
        out_specs=pl.BlockSpec((None, 3, LANES, A_WIN), lambda i: (i, 0, 0, 0)),
        out_shape=jax.ShapeDtypeStruct((n, 3, LANES, A_WIN), F32),
        compiler_params=_cparams(("parallel",)),
        name="window_bias",
    )(table)
    return out.reshape(len(A_GROUPS), A_PAD_HEADS // 2, 2, 3, LANES, A_WIN)


def _rope_tables(seq):
    n_rows = seq // GRID_W
    row = jnp.repeat(jnp.arange(n_rows, dtype=jnp.int32), GRID_W).astype(F32)
    col = jnp.tile(jnp.arange(GRID_W, dtype=jnp.int32), n_rows).astype(F32)
    half = HEAD_DIM // 4
    inv = ROPE_THETA ** (-jnp.arange(half, dtype=F32) / half)
    ang = jnp.concatenate([row[:, None] * inv, col[:, None] * inv], axis=-1)
    lane = jnp.arange(LANES, dtype=jnp.int32) % HEAD_DIM
    idx = (lane // 32) * half + lane % half
    sign = jnp.where((lane % 32) < half, -1.0, 1.0).astype(F32)
    return jnp.cos(ang)[:, idx], jnp.sin(ang)[:, idx] * sign


def _pad_group_cols(w, axis, unit=HEAD_DIM):
    parts = []
    h0 = 0
    for (_, _, nh) in A_GROUPS:
        sl = [slice(None)] * w.ndim
        sl[axis] = slice(h0 * unit, (h0 + nh) * unit)
        part = w[tuple(sl)]
        if nh < A_PAD_HEADS:
            pad = [(0, 0)] * w.ndim
            pad[axis] = (0, (A_PAD_HEADS - nh) * unit)
            part = jnp.pad(part, pad)
        parts.append(part)
        h0 += nh
    return parts


def _dilated_layer(h, g_mix, w_qkv, w_o, bias_tiles, batch, seq):
    d_attn = w_qkv.shape[1] // 3
    wq, wk, wv = (w_qkv[:, i * d_attn:(i + 1) * d_attn] for i in range(3))
    qs, ks, vs = _pad_group_cols(wq, 1), _pad_group_cols(wk, 1), _pad_group_cols(wv, 1)
    w = jnp.concatenate([jnp.concatenate([qs[g], ks[g], vs[g]], axis=1) for g in range(3)],
                        axis=1).astype(BF16)
    slab_cols = 3 * A_COLS
    scale = jnp.tile(jnp.concatenate([jnp.full((A_COLS,), Q_SCALE, F32),
                                      jnp.ones((2 * A_COLS,), F32)]), 3)[None, :]
    slabs = _norm_proj_groups(h, g_mix, w, scale, batch=batch, seq=seq)
    outs, lses = [], []
    for g in range(len(A_GROUPS)):
        o, lse = _window_attn(slabs[g], bias_tiles[g])
        outs.append(o)
        lses.append(lse)
    w_o_pad = jnp.concatenate(_pad_group_cols(w_o, 0), axis=0).astype(BF16)
    return outs, lses, w_o_pad


def kernel(x, rel_bias, norm_mix_g, norm_mlp_g, norm_final_g, a_w_qkv, a_w_o, b_w_qkv,
           b_lambda, b_subln_g, b_w_o, c_w_qkv, c_q_norm_g, c_k_norm_g, c_w_o, mlp_w_in,
           mlp_w_out):
    batch, seq, d = x.shape
    t = batch * seq
    h = x.reshape(t, d)
    cos, sin = _rope_tables(seq)
    window_bias = _window_bias_tiles(rel_bias)
    for i in range(N_LAYERS):
        kind, j = i % 3, i // 3
        g_mix = norm_mix_g[i][None, :]
        lse = None
        if kind == 0:
            o, lse, w_o = _dilated_layer(h, g_mix, a_w_qkv[j], a_w_o[j], window_bias, batch, seq)
        elif kind == 1:
            scale = jnp.concatenate([jnp.full((d,), Q_SCALE_LOG2, F32),
                                     jnp.ones((2 * d,), F32)])[None, :]
            qkv = _norm_proj(h, g_mix, b_w_qkv[j].astype(BF16), scale, chunk=1024)
            n_pairs = d // LANES
            o = _pair_attn(qkv.reshape(batch, seq, 3 * d), n_pairs=n_pairs,
                           k_block=lambda p: n_pairs + p, v_block=lambda p: 2 * n_pairs + p,
                           mode="diff", gb=_diff_bias_blocks(rel_bias), lam=b_lambda[j],
                           subln=b_subln_g[j][None, :], lambda_init=_lambda_init(i))
            o = o.reshape(t, d)
            w_o = b_w_o[j].astype(BF16)
        else:
            n_q = d
            n_kv = C_KV_HEADS * HEAD_DIM
            wq = c_w_qkv[j][:, :n_q]
            wk = c_w_qkv[j][:, n_q:n_q + n_kv].reshape(d, C_KV_HEADS, 1, HEAD_DIM)
            wv = c_w_qkv[j][:, n_q + n_kv:].reshape(d, C_KV_HEADS, 1, HEAD_DIM)
            wk = jnp.broadcast_to(wk, (d, C_KV_HEADS, 2, HEAD_DIM)).reshape(d, 2 * n_kv)
            wv = jnp.broadcast_to(wv, (d, C_KV_HEADS, 2, HEAD_DIM)).reshape(d, 2 * n_kv)
            w = jnp.concatenate([wq, wk, wv], axis=1).astype(BF16)
            qkv = _norm_proj_rope(h, g_mix, w, jnp.tile(c_q_norm_g[j], 2)[None, :],
                                  jnp.tile(c_k_norm_g[j], 2)[None, :], cos, sin,
                                  n_q=n_q, n_k=2 * n_kv, seq=seq)
            n_pairs = n_q // LANES
            pairs_per_kv = (n_q // n_kv) // 2
            o = _pair_attn(qkv.reshape(batch, seq, n_q + 4 * n_kv), n_pairs=n_pairs,
                           k_block=lambda p: n_pairs + p // pairs_per_kv,
                           v_block=lambda p: n_pairs + C_KV_HEADS + p // pairs_per_kv,
                           mode="gqa")
            o = o.reshape(t, d)
            w_o = c_w_o[j].astype(BF16)
        g_final = norm_final_g[None, :] if i == N_LAYERS - 1 else None
        h = _post(h, o, w_o, norm_mlp_g[i][None, :], mlp_w_in[i].astype(BF16),
                  mlp_w_out[i].astype(BF16), lse=lse, g_final=g_final, seq=seq)
    return h.reshape(batch, seq, d)
```

```python
import functools
import math

import jax
import jax.numpy as jnp
from jax import lax
from jax.experimental import pallas as pl
from jax.experimental.pallas import tpu as pltpu

F32 = jnp.float32
BF16 = jnp.bfloat16

D_MODEL = 1024
HEAD_DIM = 64
LANES = 128
EPS = 1e-6
NEG = -1e30
N_LAYERS = 4
D_FF = 4 * D_MODEL
A_GROUPS = ((128, 1, 6), (512, 4, 5), (2048, 16, 5))
A_PAD_HEADS = 6
A_COLS = A_PAD_HEADS * HEAD_DIM
A_WIN = 256
A_SIDE = 64
NUM_BUCKETS = 32
REL_MAX_DISTANCE = 1024
B_EBLOCKS = 13
B_ECLIP = 6
C_KV_HEADS = 4
GRID_W = 64
ROPE_THETA = 10000.0

LOG2E = math.log2(math.e)
Q_SCALE = HEAD_DIM ** -0.5
Q_SCALE_LOG2 = Q_SCALE * LOG2E

VMEM_LIMIT = 56 * 1024 * 1024


def _cparams(sem):
    return pltpu.CompilerParams(dimension_semantics=sem, vmem_limit_bytes=VMEM_LIMIT)


def _lambda_init(layer_idx):
    return 0.8 - 0.6 * math.exp(-0.3 * layer_idx)


def _rms(x, g):
    ms = jnp.mean(x * x, axis=-1, keepdims=True)
    return x * lax.rsqrt(ms + EPS) * g


def _norm_proj_kernel(x_ref, g_ref, w_ref, cs_ref, o_ref, *, chunk):
    xn = _rms(x_ref[...], g_ref[...]).astype(BF16)
    n = o_ref.shape[-1]
    for c in range(0, n, chunk):
        y = jnp.dot(xn, w_ref[:, c:c + chunk], preferred_element_type=F32)
        o_ref[:, c:c + chunk] = (y * cs_ref[:, c:c + chunk]).astype(o_ref.dtype)


def _norm_proj(x, g, w, col_scale, *, tm=512, chunk=None):
    t, d = x.shape
    n = w.shape[1]
    chunk = chunk or n
    return pl.pallas_call(
        functools.partial(_norm_proj_kernel, chunk=chunk),
        grid=(t // tm,),
        in_specs=[
            pl.BlockSpec((tm, d), lambda i: (i, 0)),
            pl.BlockSpec((1, d), lambda i: (0, 0)),
            pl.BlockSpec((d, n), lambda i: (0, 0)),
            pl.BlockSpec((1, n), lambda i: (0, 0)),
        ],
        out_specs=pl.BlockSpec((tm, n), lambda i: (i, 0)),
        out_shape=jax.ShapeDtypeStruct((t, n), BF16),
        compiler_params=_cparams(("parallel",)),
        name="norm_proj",
    )(x, g, w, col_scale)


def _norm_proj_groups_kernel(x_ref, g_ref, w_ref, cs_ref, o0_ref, o1_ref, o2_ref, y_scr):
    xn = _rms(x_ref[...], g_ref[...]).astype(BF16)
    tm = x_ref.shape[0]
    n = 3 * A_COLS
    for gi, o_ref in enumerate((o0_ref, o1_ref, o2_ref)):
        dil = A_GROUPS[gi][1]
        cols = slice(gi * n, (gi + 1) * n)
        y = jnp.dot(xn, w_ref[:, cols], preferred_element_type=F32) * cs_ref[:, cols]
        if dil == 1:
            o_ref[0] = y.astype(o_ref.dtype)
        else:
            for cb in range(n // LANES):
                y_scr[cb] = y[:, cb * LANES:(cb + 1) * LANES]
            for r in range(dil):
                for cb in range(n // LANES):
                    o_ref[r, :, cb * LANES:(cb + 1) * LANES] = (
                        y_scr[cb, pl.ds(r, tm // dil, stride=dil), :].astype(o_ref.dtype))


def _norm_proj_groups(x, g, w, col_scale, *, batch, seq, tm=512):
    t, d = x.shape
    n = 3 * A_COLS
    spb = seq // tm
    const = lambda i: (0, 0)
    out_specs, out_shape = [], []
    for (_, dil, _) in A_GROUPS:
        out_specs.append(pl.BlockSpec((None, dil, tm // dil, n),
                                      lambda i: (i // spb, 0, i % spb, 0)))
        out_shape.append(jax.ShapeDtypeStruct((batch, dil, seq // dil, n), BF16))
    return pl.pallas_call(
        _norm_proj_groups_kernel,
        grid=(t // tm,),
        in_specs=[
            pl.BlockSpec((tm, d), lambda i: (i, 0)),
            pl.BlockSpec((1, d), const),
            pl.BlockSpec((d, 3 * n), const),
            pl.BlockSpec((1, 3 * n), const),
        ],
        out_specs=out_specs,
        out_shape=out_shape,
        scratch_shapes=[pltpu.VMEM((n // LANES, tm, LANES), F32)],
        compiler_params=_cparams(("parallel",)),
        name="norm_proj_groups",
    )(x, g, w, col_scale)


def _head_norm_rope(y, gain, cos, sin_signed, lane):
    lo = lane < HEAD_DIM
    y2 = y * y
    s0 = jnp.sum(jnp.where(lo, y2, 0.0), axis=-1, keepdims=True)
    s1 = jnp.sum(jnp.where(lo, 0.0, y2), axis=-1, keepdims=True)
    inv = jnp.where(lo, lax.rsqrt(s0 / HEAD_DIM + EPS), lax.rsqrt(s1 / HEAD_DIM + EPS))
    yn = y * inv * gain
    first = (lane & 31) < 16
    partner = jnp.where(first, pltpu.roll(yn, LANES - 16, 1), pltpu.roll(yn, 16, 1))
    return yn * cos + partner * sin_signed


def _norm_proj_rope_kernel(x_ref, g_ref, w_ref, qg_ref, kg_ref, cos_ref, sin_ref, o_ref,
                           *, n_q, n_k, chunk):
    xn = _rms(x_ref[...], g_ref[...]).astype(BF16)
    n = o_ref.shape[-1]
    tm = x_ref.shape[0]
    lane = lax.broadcasted_iota(jnp.int32, (tm, LANES), 1)
    cos = cos_ref[...]
    sin = sin_ref[...]
    for c in range(0, n, chunk):
        y = jnp.dot(xn, w_ref[:, c:c + chunk], preferred_element_type=F32)
        for b in range(0, chunk, LANES):
            col = c + b
            blk = y[:, b:b + LANES]
            if col < n_q:
                blk = _head_norm_rope(blk, qg_ref[...], cos, sin, lane) * Q_SCALE_LOG2
            elif col < n_q + n_k:
                blk = _head_norm_rope(blk, kg_ref[...], cos, sin, lane)
            o_ref[:, col:col + LANES] = blk.astype(o_ref.dtype)


def _norm_proj_rope(x, g, w, q_gain, k_gain, cos, sin, *, n_q, n_k, seq, tm=512, chunk=512):
    t, d = x.shape
    n = w.shape[1]
    sblk = seq // tm
    return pl.pallas_call(
        functools.partial(_norm_proj_rope_kernel, n_q=n_q, n_k=n_k, chunk=chunk),
        grid=(t // tm,),
        in_specs=[
            pl.BlockSpec((tm, d), lambda i: (i, 0)),
            pl.BlockSpec((1, d), lambda i: (0, 0)),
            pl.BlockSpec((d, n), lambda i: (0, 0)),
            pl.BlockSpec((1, LANES), lambda i: (0, 0)),
            pl.BlockSpec((1, LANES), lambda i: (0, 0)),
            pl.BlockSpec((tm, LANES), lambda i: (i % sblk, 0)),
            pl.BlockSpec((tm, LANES), lambda i: (i % sblk, 0)),
        ],
        out_specs=pl.BlockSpec((tm, n), lambda i: (i, 0)),
        out_shape=jax.ShapeDtypeStruct((t, n), BF16),
        compiler_params=_cparams(("parallel",)),
        name="norm_proj_rope",
    )(x, g, w, q_gain, k_gain, cos, sin)


def _pair_attn_kernel(*refs, tq, ts, tk, n_kt, mode, lambda_init):
    vt_ref = refs[-1]
    if mode == "diff":
        q_ref, k_ref, v_ref, gb_ref, lam_ref, sg_ref, o_ref = refs[:-1]
    else:
        q_ref, k_ref, v_ref, o_ref = refs[:-1]
    qi = pl.program_id(2)
    seq = k_ref.shape[0]

    @pl.when(qi == 0)
    def _():
        for c in range(0, seq, tk):
            vt_ref[:, c:c + tk] = v_ref[c:c + tk, :].astype(F32).T.astype(BF16)

    n_st = tq // ts
    lane = lax.broadcasted_iota(jnp.int32, (ts, LANES), 1)
    lo = lane < HEAD_DIM
    qs_st = []
    for st in range(n_st):
        q = q_ref[st * ts:(st + 1) * ts, :]
        zero = jnp.zeros_like(q)
        qs_st.append(jnp.concatenate([jnp.where(lo, q, zero), jnp.where(lo, zero, q)], axis=0))
    rb_n = ts // LANES
    cb_n = tk // LANES

    def logits(kt, st):
        k = k_ref[kt * tk:(kt + 1) * tk, :]
        s = lax.dot_general(k, qs_st[st], (((1,), (1,)), ((), ())),
                            preferred_element_type=F32)
        if mode == "diff":
            base = kt * cb_n - (qi * n_st + st) * rb_n
            rows = []
            for cb in range(cb_n):
                blocks = []
                for j in range(2):
                    for rb in range(rb_n):
                        e = jnp.clip(base + (cb - rb), -B_ECLIP, B_ECLIP) + B_ECLIP
                        blocks.append(gb_ref[j, e])
                rows.append(jnp.concatenate(blocks, axis=1))
            s = s + jnp.concatenate(rows, axis=0)
        return s

    m = [jnp.full((1, 2 * ts), NEG, F32)] * n_st
    l = [jnp.zeros((1, 2 * ts), F32)] * n_st
    acc = [jnp.zeros((LANES, 2 * ts), F32)] * n_st
    s_next = [logits(0, st) for st in range(n_st)]
    for kt in range(n_kt):
        vt = vt_ref[:, kt * tk:(kt + 1) * tk]
        s_cur = s_next
        if kt + 1 < n_kt:
            s_next = [logits(kt + 1, st) for st in range(n_st)]
        for st in range(n_st):
            s = s_cur[st]
            m_new = jnp.maximum(m[st], jnp.max(s, axis=0, keepdims=True))
            alpha = jnp.exp2(m[st] - m_new)
            p = jnp.exp2(s - m_new)
            l[st] = alpha * l[st] + jnp.sum(p, axis=0, keepdims=True)
            acc[st] = alpha * acc[st] + jnp.dot(vt, p.astype(BF16),
                                                preferred_element_type=F32)
            m[st] = m_new
    if mode == "diff":
        lam = lam_ref[...]
        lam_full = (jnp.exp(jnp.sum(lam[0:1] * lam[1:2], axis=-1, keepdims=True))
                    - jnp.exp(jnp.sum(lam[2:3] * lam[3:4], axis=-1, keepdims=True))
                    + lambda_init)
    for st in range(n_st):
        o = acc[st] / l[st]
        o0, o1 = o[:, :ts].T, o[:, ts:].T
        if mode == "diff":
            a = o0 - lam_full * o1
            y = _rms(a, sg_ref[...]) * (1.0 - lambda_init)
        else:
            y = jnp.where(lo, o0, o1)
        o_ref[st * ts:(st + 1) * ts, :] = y.astype(o_ref.dtype)


def _pair_attn(qkv, *, n_pairs, k_block, v_block, mode, gb=None, lam=None, subln=None,
               lambda_init=0.0, tq=512, ts=256, tk=512):
    b, s, _ = qkv.shape
    in_specs = [
        pl.BlockSpec((None, tq, LANES), lambda bi, p, qi: (bi, qi, p)),
        pl.BlockSpec((None, s, LANES), lambda bi, p, qi: (bi, 0, k_block(p))),
        pl.BlockSpec((None, s, LANES), lambda bi, p, qi: (bi, 0, v_block(p))),
    ]
    args = [qkv, qkv, qkv]
    if mode == "diff":
        in_specs += [
            pl.BlockSpec((None, 2, B_EBLOCKS, LANES, LANES), lambda bi, p, qi: (p, 0, 0, 0, 0)),
            pl.BlockSpec((4, HEAD_DIM), lambda bi, p, qi: (0, 0)),
            pl.BlockSpec((1, LANES), lambda bi, p, qi: (0, 0)),
        ]
        args += [gb, lam, subln]
    return pl.pallas_call(
        functools.partial(_pair_attn_kernel, tq=tq, ts=ts, tk=tk, n_kt=s // tk, mode=mode,
                          lambda_init=lambda_init),
        grid=(b, n_pairs, s // tq),
        in_specs=in_specs,
        out_specs=pl.BlockSpec((None, tq, LANES), lambda bi, p, qi: (bi, qi, p)),
        out_shape=jax.ShapeDtypeStruct((b, s, n_pairs * LANES), BF16),
        scratch_shapes=[pltpu.VMEM((LANES, s), BF16)],
        compiler_params=_cparams(("parallel", "parallel", "arbitrary")),
        name="pair_attn_" + mode,
    )(*args)


def _window_attn_kernel(slab_ref, ga_ref, o_ref, lse_ref, *, sub_len):
    lane = lax.broadcasted_iota(jnp.int32, (LANES, LANES), 1)
    lo = lane < HEAD_DIM
    n_p = A_COLS // LANES

    def sub_block(sb):
        static = isinstance(sb, int)
        q0 = sb * LANES
        if static:
            start = min(max(q0 - A_SIDE, 0), sub_len - A_WIN)
            variant = 1 if q0 == 0 else (2 if q0 == sub_len - LANES else 0)
            qrows, wrows = slice(q0, q0 + LANES), slice(start, start + A_WIN)
        else:
            q0 = pl.multiple_of(q0, LANES)
            start = pl.multiple_of(jnp.clip(q0 - A_SIDE, 0, sub_len - A_WIN), A_SIDE)
            variant = jnp.where(q0 == 0, 1, jnp.where(q0 == sub_len - LANES, 2, 0))
            qrows, wrows = pl.ds(q0, LANES), pl.ds(start, A_WIN)
        for p in range(n_p):
            cols = slice(p * LANES, (p + 1) * LANES)
            q = slab_ref[qrows, cols]
            kw = slab_ref[wrows, A_COLS + p * LANES:A_COLS + (p + 1) * LANES]
            vw = slab_ref[wrows, 2 * A_COLS + p * LANES:2 * A_COLS + (p + 1) * LANES]
            zero = jnp.zeros_like(q)
            qs = jnp.concatenate([jnp.where(lo, q, zero), jnp.where(lo, zero, q)], axis=0)
            s = lax.dot_general(qs, kw, (((1,), (1,)), ((), ())), preferred_element_type=F32)
            s = s + jnp.concatenate([ga_ref[p, 0, variant], ga_ref[p, 1, variant]], axis=0)
            m = jnp.max(s, axis=-1, keepdims=True)
            e = jnp.exp(s - m)
            l = jnp.sum(e, axis=-1, keepdims=True)
            acc = jnp.dot(e.astype(BF16), vw, preferred_element_type=F32)
            o = acc / l
            lse = m + jnp.log(l)
            o_ref[qrows, cols] = jnp.where(lo, o[:LANES], o[LANES:]).astype(o_ref.dtype)
            lse_ref[qrows, cols] = jnp.where(lo, lse[:LANES], lse[LANES:])

    n_sb = sub_len // LANES
    if n_sb <= 2:
        for sb in range(n_sb):
            sub_block(sb)
    else:
        def body(i, carry):
            sub_block(2 * i)
            sub_block(2 * i + 1)
            return carry
        lax.fori_loop(0, n_sb // 2, body, 0)


def _window_attn(slab, ga):
    b, dil, sub_len, n = slab.shape
    out_spec = pl.BlockSpec((None, None, sub_len, A_COLS), lambda bi, r: (bi, r, 0, 0))
    return pl.pallas_call(
        functools.partial(_window_attn_kernel, sub_len=sub_len),
        grid=(b, dil),
        in_specs=[
            pl.BlockSpec((None, None, sub_len, n), lambda bi, r: (bi, r, 0, 0)),
            pl.BlockSpec(ga.shape, lambda bi, r: (0,) * ga.ndim),
        ],
        out_specs=[out_spec, out_spec],
        out_shape=[jax.ShapeDtypeStruct((b, dil, sub_len, A_COLS), BF16),
                   jax.ShapeDtypeStruct((b, dil, sub_len, A_COLS), F32)],
        compiler_params=_cparams(("parallel", "parallel")),
        name="window_attn",
    )(slab, ga)


def _post_kernel(*refs, mix_groups, final_norm, ff_chunk):
    refs = list(refs)
    h_ref = refs.pop(0)
    if mix_groups:
        og_refs = [refs.pop(0) for _ in A_GROUPS]
        lg_refs = [refs.pop(0) for _ in A_GROUPS]
    else:
        o_ref = refs.pop(0)
    wo_ref, g_ref, win_ref, wout_ref = refs[:4]
    refs = refs[4:]
    gf_ref = refs.pop(0) if final_norm else None
    out_ref = refs.pop(0)

    if mix_groups:
        o_scr, lse_scr = refs
        n_cb, tm, _ = o_scr.shape
        n = n_cb * LANES
        cb_per_group = A_COLS // LANES
        for gi, (_, dil, _) in enumerate(A_GROUPS):
            for r in range(dil):
                rows = pl.ds(r, tm // dil, stride=dil) if dil > 1 else slice(None)
                for cb in range(cb_per_group):
                    cols = slice(cb * LANES, (cb + 1) * LANES)
                    o_scr[gi * cb_per_group + cb, rows, :] = og_refs[gi][r, :, cols].astype(F32)
                    lse_scr[gi * cb_per_group + cb, rows, :] = lg_refs[gi][r, :, cols]
        o = jnp.concatenate([o_scr[cb] for cb in range(n_cb)], axis=1)
        lse = jnp.concatenate([lse_scr[cb] for cb in range(n_cb)], axis=1)
        lane = lax.broadcasted_iota(jnp.int32, (tm, n), 1)
        grp = (lane >= A_COLS).astype(jnp.int32) + (lane >= 2 * A_COLS).astype(jnp.int32)
        head = (lane - grp * A_COLS) >> 6
        n_heads = jnp.where(grp == 0, A_GROUPS[0][2],
                            jnp.where(grp == 1, A_GROUPS[1][2], A_GROUPS[2][2]))
        real = head < n_heads
        mx = jnp.max(jnp.where(real, lse, NEG), axis=-1, keepdims=True)
        e = jnp.where(real, jnp.exp(lse - mx), 0.0)
        s_g = [jnp.sum(jnp.where(grp == gi, e, 0.0), axis=-1, keepdims=True)
               / (HEAD_DIM * A_GROUPS[gi][2]) for gi in range(3)]
        tot = s_g[0] + s_g[1] + s_g[2]
        alpha = jnp.where(grp == 0, s_g[0], jnp.where(grp == 1, s_g[1], s_g[2])) / tot
        o = (o * (len(A_GROUPS) * alpha)).astype(BF16)
    else:
        o = o_ref[...]
    h1 = h_ref[...] + jnp.dot(o, wo_ref[...], preferred_element_type=F32)
    xn = _rms(h1, g_ref[...]).astype(BF16)
    acc = h1
    for c in range(0, D_FF, ff_chunk):
        u = jnp.dot(xn, win_ref[:, c:c + ff_chunk], preferred_element_type=F32)
        u = jnp.maximum(u, 0.0)
        u = (u * u).astype(BF16)
        acc = acc + jnp.dot(u, wout_ref[c:c + ff_chunk, :], preferred_element_type=F32)
    if final_norm:
        acc = _rms(acc, gf_ref[...])
    out_ref[...] = acc


def _post(h, o, w_o, g_mlp, w_in, w_out, *, lse=None, g_final=None, seq=None, tm=256,
          ff_chunk=1024):
    t, d = h.shape
    const = lambda i: (0, 0)
    single = dict(pipeline_mode=pl.Buffered(1))
    in_specs = [pl.BlockSpec((tm, d), lambda i: (i, 0))]
    args = [h]
    scratch = []
    if lse is not None:
        n_o = len(A_GROUPS) * A_COLS
        spb = seq // tm
        for arrs in (o, lse):
            for arr, (_, dil, _) in zip(arrs, A_GROUPS):
                in_specs.append(pl.BlockSpec((None, dil, tm // dil, A_COLS),
                                             lambda i: (i // spb, 0, i % spb, 0)))
                args.append(arr)
        scratch = [pltpu.VMEM((n_o // LANES, tm, LANES), F32)] * 2
    else:
        n_o = o.shape[1]
        in_specs.append(pl.BlockSpec((tm, n_o), lambda i: (i, 0)))
        args.append(o)
    in_specs += [pl.BlockSpec((n_o, d), const, **single),
                 pl.BlockSpec((1, d), const),
                 pl.BlockSpec((d, D_FF), const, **single),
                 pl.BlockSpec((D_FF, d), const, **single)]
    args += [w_o, g_mlp, w_in, w_out]
    if g_final is not None:
        in_specs.append(pl.BlockSpec((1, d), const))
        args.append(g_final)
    return pl.pallas_call(
        functools.partial(_post_kernel, mix_groups=lse is not None,
                          final_norm=g_final is not None, ff_chunk=ff_chunk),
        grid=(t // tm,),
        in_specs=in_specs,
        out_specs=pl.BlockSpec((tm, d), lambda i: (i, 0)),
        out_shape=jax.ShapeDtypeStruct((t, d), F32),
        scratch_shapes=scratch,
        compiler_params=_cparams(("parallel",)),
        name="post",
    )(*args)


def _t5_bucket(rel):
    nb = NUM_BUCKETS // 2
    max_exact = nb // 2
    side = jnp.where(rel > 0, nb, 0)
    n = jnp.abs(rel)
    nf = jnp.maximum(n, 1).astype(F32)
    large = max_exact + (jnp.log(nf / max_exact) / math.log(REL_MAX_DISTANCE / max_exact)
                         * (nb - max_exact)).astype(jnp.int32)
    large = jnp.minimum(large, nb - 1)
    return side + jnp.where(n < max_exact, n, large)


def _table_lookup(bucket, tab_ref, col):
    out = jnp.zeros(bucket.shape, F32)
    for b in range(NUM_BUCKETS):
        out = jnp.where(bucket == b, tab_ref[b, col], out)
    return out


def _diff_bias_kernel(tab_ref, o_ref):
    hj = pl.program_id(0)
    r = lax.broadcasted_iota(jnp.int32, (LANES, LANES), 0)
    c = lax.broadcasted_iota(jnp.int32, (LANES, LANES), 1)
    for e in range(B_EBLOCKS):
        rel = LANES * (e - B_ECLIP) + r - c
        o_ref[e] = _table_lookup(_t5_bucket(rel), tab_ref, hj) * LOG2E


def _diff_bias_blocks(rel_bias):
    n = rel_bias.shape[1]
    out = pl.pallas_call(
        _diff_bias_kernel,
        grid=(n,),
        in_specs=[pl.BlockSpec(memory_space=pltpu.SMEM)],
        out_specs=pl.BlockSpec((None, B_EBLOCKS, LANES, LANES), lambda i: (i, 0, 0, 0)),
        out_shape=jax.ShapeDtypeStruct((n, B_EBLOCKS, LANES, LANES), F32),
        compiler_params=_cparams(("parallel",)),
        name="diff_bias",
    )(rel_bias.astype(F32))
    return out.reshape(n // 2, 2, B_EBLOCKS, LANES, LANES)


A_WINDOW_OFFSETS = (-A_SIDE, 0, -LANES)


def _window_bias_kernel(tab_ref, o_ref):
    hp = pl.program_id(0)
    dil = jnp.where(hp < A_PAD_HEADS, A_GROUPS[0][1],
                    jnp.where(hp < 2 * A_PAD_HEADS, A_GROUPS[1][1], A_GROUPS[2][1]))
    r = lax.broadcasted_iota(jnp.int32, (LANES, A_WIN), 0)
    c = lax.broadcasted_iota(jnp.int32, (LANES, A_WIN), 1)
    for v, off in enumerate(A_WINDOW_OFFSETS):
        rel = c + off - r
        bias = _table_lookup(_t5_bucket(rel * dil), tab_ref, hp)
        o_ref[v] = jnp.where(jnp.abs(rel) <= A_SIDE, bias, NEG)


def _window_bias_tiles(rel_bias):
    cols = _pad_group_cols(rel_bias.astype(F32), 1, unit=1)
    table = jnp.concatenate(cols, axis=1)
    n = table.shape[1]
    out = pl.pallas_call(
        _window_bias_kernel,
        grid=(n,),
        in_specs=[pl.BlockSpec(memory_space=pltpu.SMEM)],
        out_specs=pl.BlockSpec((None, 3, LANES, A_WIN), lambda i: (i, 0, 0, 0)),
        out_shape=jax.ShapeDtypeStruct((n, 3, LANES, A_WIN), F32),
        compiler_params=_cparams(("parallel",)),
        name="window_bias",
    )(table)
    return out.reshape(len(A_GROUPS), A_PAD_HEADS // 2, 2, 3, LANES, A_WIN)


def _rope_tables(seq):
    n_rows = seq // GRID_W
    row = jnp.repeat(jnp.arange(n_rows, dtype=jnp.int32), GRID_W).astype(F32)
    col = jnp.tile(jnp.arange(GRID_W, dtype=jnp.int32), n_rows).astype(F32)
    half = HEAD_DIM // 4
    inv = ROPE_THETA ** (-jnp.arange(half, dtype=F32) / half)
    ang = jnp.concatenate([row[:, None] * inv, col[:, None] * inv], axis=-1)
    lane = jnp.arange(LANES, dtype=jnp.int32) % HEAD_DIM
    idx = (lane // 32) * half + lane % half
    sign = jnp.where((lane % 32) < half, -1.0, 1.0).astype(F32)
    return jnp.cos(ang)[:, idx], jnp.sin(ang)[:, idx] * sign


def _pad_group_cols(w, axis, unit=HEAD_DIM):
    parts = []
    h0 = 0
    for (_, _, nh) in A_GROUPS:
        sl = [slice(None)] * w.ndim
        sl[axis] = slice(h0 * unit, (h0 + nh) * unit)
        part = w[tuple(sl)]
        if nh < A_PAD_HEADS:
            pad = [(0, 0)] * w.ndim
            pad[axis] = (0, (A_PAD_HEADS - nh) * unit)
            part = jnp.pad(part, pad)
        parts.append(part)
        h0 += nh
    return parts


def _dilated_layer(h, g_mix, w_qkv, w_o, bias_tiles, batch, seq):
    d_attn = w_qkv.shape[1] // 3
    wq, wk, wv = (w_qkv[:, i * d_attn:(i + 1) * d_attn] for i in range(3))
    qs, ks, vs = _pad_group_cols(wq, 1), _pad_group_cols(wk, 1), _pad_group_cols(wv, 1)
    w = jnp.concatenate([jnp.concatenate([qs[g], ks[g], vs[g]], axis=1) for g in range(3)],
                        axis=1).astype(BF16)
    slab_cols = 3 * A_COLS
    scale = jnp.tile(jnp.concatenate([jnp.full((A_COLS,), Q_SCALE, F32),
                                      jnp.ones((2 * A_COLS,), F32)]), 3)[None, :]
    slabs = _norm_proj_groups(h, g_mix, w, scale, batch=batch, seq=seq)
    outs, lses = [], []
    for g in range(len(A_GROUPS)):
        o, lse = _window_attn(slabs[g], bias_tiles[g])
        outs.append(o)
        lses.append(lse)
    w_o_pad = jnp.concatenate(_pad_group_cols(w_o, 0), axis=0).astype(BF16)
    return outs, lses, w_o_pad


def kernel(x, rel_bias, norm_mix_g, norm_mlp_g, norm_final_g, a_w_qkv, a_w_o, b_w_qkv,
           b_lambda, b_subln_g, b_w_o, c_w_qkv, c_q_norm_g, c_k_norm_g, c_w_o, mlp_w_in,
           mlp_w_out):
    batch, seq, d = x.shape
    t = batch * seq
    h = x.reshape(t, d)
    cos, sin = _rope_tables(seq)
    window_bias = _window_bias_tiles(rel_bias)
    for i in range(N_LAYERS):
        kind, j = i % 3, i // 3
        g_mix = norm_mix_g[i][None, :]
        lse = None
        if kind == 0:
            o, lse, w_o = _dilated_layer(h, g_mix, a_w_qkv[j], a_w_o[j], window_bias, batch, seq)
        elif kind == 1:
            scale = jnp.concatenate([jnp.full((d,), Q_SCALE_LOG2, F32),
                                     jnp.ones((2 * d,), F32)])[None, :]
            qkv = _norm_proj(h, g_mix, b_w_qkv[j].astype(BF16), scale, chunk=1024)
            n_pairs = d // LANES
            o = _pair_attn(qkv.reshape(batch, seq, 3 * d), n_pairs=n_pairs,
                           k_block=lambda p: n_pairs + p, v_block=lambda p: 2 * n_pairs + p,
                           mode="diff", gb=_diff_bias_blocks(rel_bias), lam=b_lambda[j],
                           subln=b_subln_g[j][None, :], lambda_init=_lambda_init(i))
            o = o.reshape(t, d)
            w_o = b_w_o[j].astype(BF16)
        else:
            n_q = d
            n_kv = C_KV_HEADS * HEAD_DIM
            wq = c_w_qkv[j][:, :n_q]
            wk = c_w_qkv[j][:, n_q:n_q + n_kv].reshape(d, C_KV_HEADS, 1, HEAD_DIM)
            wv = c_w_qkv[j][:, n_q + n_kv:].reshape(d, C_KV_HEADS, 1, HEAD_DIM)
            wk = jnp.broadcast_to(wk, (d, C_KV_HEADS, 2, HEAD_DIM)).reshape(d, 2 * n_kv)
            wv = jnp.broadcast_to(wv, (d, C_KV_HEADS, 2, HEAD_DIM)).reshape(d, 2 * n_kv)
            w = jnp.concatenate([wq, wk, wv], axis=1).astype(BF16)
            qkv = _norm_proj_rope(h, g_mix, w, jnp.tile(c_q_norm_g[j], 2)[None, :],
                                  jnp.tile(c_k_norm_g[j], 2)[None, :], cos, sin,
                                  n_q=n_q, n_k=2 * n_kv, seq=seq)
            n_pairs = n_q // LANES
            pairs_per_kv = (n_q // n_kv) // 2
            o = _pair_attn(qkv.reshape(batch, seq, n_q + 4 * n_kv), n_pairs=n_pairs,
                           k_block=lambda p: n_pairs + p // pairs_per_kv,
                           v_block=lambda p: n_pairs + C_KV_HEADS + p // pairs_per_kv,
                           mode="gqa")
            o = o.reshape(t, d)
            w_o = c_w_o[j].astype(BF16)
        g_final = norm_final_g[None, :] if i == N_LAYERS - 1 else None
        h = _post(h, o, w_o, norm_mlp_g[i][None, :], mlp_w_in[i].astype(BF16),
                  mlp_w_out[i].astype(BF16), lse=lse, g_final=g_final, seq=seq)
    return h.reshape(batch, seq, d)
```

```python
import functools
import math

import jax
import jax.numpy as jnp
from jax import lax
from jax.experimental import pallas as pl
from jax.experimental.pallas import tpu as pltpu

F32 = jnp.float32
BF16 = jnp.bfloat16

D_MODEL = 1024
HEAD_DIM = 64
LANES = 128
EPS = 1e-6
NEG = -1e30
N_LAYERS = 4
D_FF = 4 * D_MODEL
A_GROUPS = ((128, 1, 6), (512, 4, 5), (2048, 16, 5))
A_PAD_HEADS = 6
A_COLS = A_PAD_HEADS * HEAD_DIM
A_WIN = 256
A_SIDE = 64
A_UNROLL = 8
NUM_BUCKETS = 32
REL_MAX_DISTANCE = 1024
B_EBLOCKS = 13
B_ECLIP = 6
C_KV_HEADS = 4
GRID_W = 64
ROPE_THETA = 10000.0

LOG2E = math.log2(math.e)
Q_SCALE = HEAD_DIM ** -0.5
Q_SCALE_LOG2 = Q_SCALE * LOG2E

VMEM_LIMIT = 56 * 1024 * 1024


def _cparams(sem):
    return pltpu.CompilerParams(dimension_semantics=sem, vmem_limit_bytes=VMEM_LIMIT)


def _lambda_init(layer_idx):
    return 0.8 - 0.6 * math.exp(-0.3 * layer_idx)


def _rms(x, g):
    ms = jnp.mean(x * x, axis=-1, keepdims=True)
    return x * lax.rsqrt(ms + EPS) * g


def _norm_proj_kernel(x_ref, g_ref, w_ref, cs_ref, o_ref, *, chunk):
    xn = _rms(x_ref[...], g_ref[...]).astype(BF16)
    n = o_ref.shape[-1]
    for c in range(0, n, chunk):
        y = jnp.dot(xn, w_ref[:, c:c + chunk], preferred_element_type=F32)
        o_ref[:, c:c + chunk] = (y * cs_ref[:, c:c + chunk]).astype(o_ref.dtype)


def _norm_proj(x, g, w, col_scale, *, tm=512, chunk=None):
    t, d = x.shape
    n = w.shape[1]
    chunk = chunk or n
    return pl.pallas_call(
        functools.partial(_norm_proj_kernel, chunk=chunk),
        grid=(t // tm,),
        in_specs=[
            pl.BlockSpec((tm, d), lambda i: (i, 0)),
            pl.BlockSpec((1, d), lambda i: (0, 0)),
            pl.BlockSpec((d, n), lambda i: (0, 0)),
            pl.BlockSpec((1, n), lambda i: (0, 0)),
        ],
        out_specs=pl.BlockSpec((tm, n), lambda i: (i, 0)),
        out_shape=jax.ShapeDtypeStruct((t, n), BF16),
        compiler_params=_cparams(("parallel",)),
        name="norm_proj",
    )(x, g, w, col_scale)


def _norm_proj_groups_kernel(x_ref, g_ref, w_ref, cs_ref, o0_ref, o1_ref, o2_ref, y_scr):
    xn = _rms(x_ref[...], g_ref[...]).astype(BF16)
    tm = x_ref.shape[0]
    n = 3 * A_COLS
    for gi, o_ref in enumerate((o0_ref, o1_ref, o2_ref)):
        dil = A_GROUPS[gi][1]
        cols = slice(gi * n, (gi + 1) * n)
        y = jnp.dot(xn, w_ref[:, cols], preferred_element_type=F32) * cs_ref[:, cols]
        if dil == 1:
            o_ref[0] = y.astype(o_ref.dtype)
        else:
            for cb in range(n // LANES):
                y_scr[cb] = y[:, cb * LANES:(cb + 1) * LANES]
            for r in range(dil):
                for cb in range(n // LANES):
                    o_ref[r, :, cb * LANES:(cb + 1) * LANES] = (
                        y_scr[cb, pl.ds(r, tm // dil, stride=dil), :].astype(o_ref.dtype))


def _norm_proj_groups(x, g, w, col_scale, *, batch, seq, tm=512):
    t, d = x.shape
    n = 3 * A_COLS
    spb = seq // tm
    const = lambda i: (0, 0)
    out_specs, out_shape = [], []
    for (_, dil, _) in A_GROUPS:
        out_specs.append(pl.BlockSpec((None, dil, tm // dil, n),
                                      lambda i: (i // spb, 0, i % spb, 0)))
        out_shape.append(jax.ShapeDtypeStruct((batch, dil, seq // dil, n), BF16))
    return pl.pallas_call(
        _norm_proj_groups_kernel,
        grid=(t // tm,),
        in_specs=[
            pl.BlockSpec((tm, d), lambda i: (i, 0)),
            pl.BlockSpec((1, d), const),
            pl.BlockSpec((d, 3 * n), const),
            pl.BlockSpec((1, 3 * n), const),
        ],
        out_specs=out_specs,
        out_shape=out_shape,
        scratch_shapes=[pltpu.VMEM((n // LANES, tm, LANES), F32)],
        compiler_params=_cparams(("parallel",)),
        name="norm_proj_groups",
    )(x, g, w, col_scale)


def _head_norm_rope(y, gain, cos, sin_signed, lane):
    lo = lane < HEAD_DIM
    y2 = y * y
    s0 = jnp.sum(jnp.where(lo, y2, 0.0), axis=-1, keepdims=True)
    s1 = jnp.sum(jnp.where(lo, 0.0, y2), axis=-1, keepdims=True)
    inv = jnp.where(lo, lax.rsqrt(s0 / HEAD_DIM + EPS), lax.rsqrt(s1 / HEAD_DIM + EPS))
    yn = y * inv * gain
    first = (lane & 31) < 16
    partner = jnp.where(first, pltpu.roll(yn, LANES - 16, 1), pltpu.roll(yn, 16, 1))
    return yn * cos + partner * sin_signed


def _norm_proj_rope_kernel(x_ref, g_ref, w_ref, qg_ref, kg_ref, cos_ref, sin_ref, o_ref,
                           *, n_q, n_k, chunk):
    xn = _rms(x_ref[...], g_ref[...]).astype(BF16)
    n = o_ref.shape[-1]
    tm = x_ref.shape[0]
    lane = lax.broadcasted_iota(jnp.int32, (tm, LANES), 1)
    cos = cos_ref[...]
    sin = sin_ref[...]
    for c in range(0, n, chunk):
        y = jnp.dot(xn, w_ref[:, c:c + chunk], preferred_element_type=F32)
        for b in range(0, chunk, LANES):
            col = c + b
            blk = y[:, b:b + LANES]
            if col < n_q:
                blk = _head_norm_rope(blk, qg_ref[...], cos, sin, lane) * Q_SCALE_LOG2
            elif col < n_q + n_k:
                blk = _head_norm_rope(blk, kg_ref[...], cos, sin, lane)
            o_ref[:, col:col + LANES] = blk.astype(o_ref.dtype)


def _norm_proj_rope(x, g, w, q_gain, k_gain, cos, sin, *, n_q, n_k, seq, tm=512, chunk=512):
    t, d = x.shape
    n = w.shape[1]
    sblk = seq // tm
    return pl.pallas_call(
        functools.partial(_norm_proj_rope_kernel, n_q=n_q, n_k=n_k, chunk=chunk),
        grid=(t // tm,),
        in_specs=[
            pl.BlockSpec((tm, d), lambda i: (i, 0)),
            pl.BlockSpec((1, d), lambda i: (0, 0)),
            pl.BlockSpec((d, n), lambda i: (0, 0)),
            pl.BlockSpec((1, LANES), lambda i: (0, 0)),
            pl.BlockSpec((1, LANES), lambda i: (0, 0)),
            pl.BlockSpec((tm, LANES), lambda i: (i % sblk, 0)),
            pl.BlockSpec((tm, LANES), lambda i: (i % sblk, 0)),
        ],
        out_specs=pl.BlockSpec((tm, n), lambda i: (i, 0)),
        out_shape=jax.ShapeDtypeStruct((t, n), BF16),
        compiler_params=_cparams(("parallel",)),
        name="norm_proj_rope",
    )(x, g, w, q_gain, k_gain, cos, sin)


def _pair_attn_kernel(*refs, tq, ts, tk, n_kt, mode, lambda_init):
    stagger = mode == "diff"
    vt_ref = refs[-1]
    if mode == "diff":
        q_ref, k_ref, v_ref, gb_ref, lam_ref, sg_ref, o_ref = refs[:-1]
    else:
        q_ref, k_ref, v_ref, o_ref = refs[:-1]
    qi = pl.program_id(2)
    seq = k_ref.shape[0]

    @pl.when(qi == 0)
    def _():
        for c in range(0, seq, tk):
            vt_ref[:, c:c + tk] = v_ref[c:c + tk, :].astype(F32).T.astype(BF16)

    n_st = tq // ts
    lane = lax.broadcasted_iota(jnp.int32, (ts, LANES), 1)
    lo = lane < HEAD_DIM
    qs_st = []
    for st in range(n_st):
        q = q_ref[st * ts:(st + 1) * ts, :]
        zero = jnp.zeros_like(q)
        qs_st.append(jnp.concatenate([jnp.where(lo, q, zero), jnp.where(lo, zero, q)], axis=0))
    rb_n = ts // LANES
    cb_n = tk // LANES

    def logits(kt, st):
        k = k_ref[kt * tk:(kt + 1) * tk, :]
        s = lax.dot_general(k, qs_st[st], (((1,), (1,)), ((), ())),
                            preferred_element_type=F32)
        if mode == "diff":
            base = kt * cb_n - (qi * n_st + st) * rb_n
            rows = []
            for cb in range(cb_n):
                blocks = []
                for j in range(2):
                    for rb in range(rb_n):
                        e = jnp.clip(base + (cb - rb), -B_ECLIP, B_ECLIP) + B_ECLIP
                        blocks.append(gb_ref[j, e])
                rows.append(jnp.concatenate(blocks, axis=1))
            s = s + jnp.concatenate(rows, axis=0)
        return s

    m = [jnp.full((1, 2 * ts), NEG, F32)] * n_st
    l = [jnp.zeros((1, 2 * ts), F32)] * n_st
    acc = [jnp.zeros((LANES, 2 * ts), F32)] * n_st
    s_tile = {}

    def qk(kt, st):
        if kt < n_kt:
            s_tile[kt, st] = logits(kt, st)

    def softmax_pv(kt, st):
        if kt >= n_kt:
            return
        s = s_tile.pop((kt, st))
        m_new = jnp.maximum(m[st], jnp.max(s, axis=0, keepdims=True))
        alpha = jnp.exp2(m[st] - m_new)
        p = jnp.exp2(s - m_new)
        l[st] = alpha * l[st] + jnp.sum(p, axis=0, keepdims=True)
        vt = vt_ref[:, kt * tk:(kt + 1) * tk]
        acc[st] = alpha * acc[st] + jnp.dot(vt, p.astype(BF16), preferred_element_type=F32)
        m[st] = m_new

    assert n_st == 2
    qk(0, 0)
    qk(0, 1)
    if stagger:
        softmax_pv(0, 0)
        for kt in range(n_kt):
            qk(kt + 1, 0)
            softmax_pv(kt, 1)
            qk(kt + 1, 1)
            softmax_pv(kt + 1, 0)
    else:
        for kt in range(n_kt):
            qk(kt + 1, 0)
            qk(kt + 1, 1)
            softmax_pv(kt, 0)
            softmax_pv(kt, 1)
    if mode == "diff":
        lam = lam_ref[...]
        lam_full = (jnp.exp(jnp.sum(lam[0:1] * lam[1:2], axis=-1, keepdims=True))
                    - jnp.exp(jnp.sum(lam[2:3] * lam[3:4], axis=-1, keepdims=True))
                    + lambda_init)
    for st in range(n_st):
        o = acc[st] / l[st]
        o0, o1 = o[:, :ts].T, o[:, ts:].T
        if mode == "diff":
            a = o0 - lam_full * o1
            y = _rms(a, sg_ref[...]) * (1.0 - lambda_init)
        else:
            y = jnp.where(lo, o0, o1)
        o_ref[st * ts:(st + 1) * ts, :] = y.astype(o_ref.dtype)


def _pair_attn(qkv, *, n_pairs, k_block, v_block, mode, gb=None, lam=None, subln=None,
               lambda_init=0.0, tq=512, ts=256, tk=512):
    b, s, _ = qkv.shape
    in_specs = [
        pl.BlockSpec((None, tq, LANES), lambda bi, p, qi: (bi, qi, p)),
        pl.BlockSpec((None, s, LANES), lambda bi, p, qi: (bi, 0, k_block(p))),
        pl.BlockSpec((None, s, LANES), lambda bi, p, qi: (bi, 0, v_block(p))),
    ]
    args = [qkv, qkv, qkv]
    if mode == "diff":
        in_specs += [
            pl.BlockSpec((None, 2, B_EBLOCKS, LANES, LANES), lambda bi, p, qi: (p, 0, 0, 0, 0)),
            pl.BlockSpec((4, HEAD_DIM), lambda bi, p, qi: (0, 0)),
            pl.BlockSpec((1, LANES), lambda bi, p, qi: (0, 0)),
        ]
        args += [gb, lam, subln]
    return pl.pallas_call(
        functools.partial(_pair_attn_kernel, tq=tq, ts=ts, tk=tk, n_kt=s // tk, mode=mode,
                          lambda_init=lambda_init),
        grid=(b, n_pairs, s // tq),
        in_specs=in_specs,
        out_specs=pl.BlockSpec((None, tq, LANES), lambda bi, p, qi: (bi, qi, p)),
        out_shape=jax.ShapeDtypeStruct((b, s, n_pairs * LANES), BF16),
        scratch_shapes=[pltpu.VMEM((LANES, s), BF16)],
        compiler_params=_cparams(("parallel", "parallel", "arbitrary")),
        name="pair_attn_" + mode,
    )(*args)


def _window_attn_kernel(slab_ref, ga_ref, o_ref, lse_ref, *, sub_len):
    lane = lax.broadcasted_iota(jnp.int32, (LANES, LANES), 1)
    lo = lane < HEAD_DIM
    n_p = A_COLS // LANES

    def sub_block(res, sb):
        static = isinstance(sb, int)
        q0 = sb * LANES
        if static:
            start = min(max(q0 - A_SIDE, 0), sub_len - A_WIN)
            variant = 1 if q0 == 0 else (2 if q0 == sub_len - LANES else 0)
            qrows, wrows = slice(q0, q0 + LANES), slice(start, start + A_WIN)
        else:
            q0 = pl.multiple_of(q0, LANES)
            start = pl.multiple_of(jnp.clip(q0 - A_SIDE, 0, sub_len - A_WIN), A_SIDE)
            variant = jnp.where(q0 == 0, 1, jnp.where(q0 == sub_len - LANES, 2, 0))
            qrows, wrows = pl.ds(q0, LANES), pl.ds(start, A_WIN)
        for p in range(n_p):
            cols = slice(p * LANES, (p + 1) * LANES)
            q = slab_ref[res, qrows, cols]
            kw = slab_ref[res, wrows, A_COLS + p * LANES:A_COLS + (p + 1) * LANES]
            vw = slab_ref[res, wrows, 2 * A_COLS + p * LANES:2 * A_COLS + (p + 1) * LANES]
            zero = jnp.zeros_like(q)
            qs = jnp.concatenate([jnp.where(lo, q, zero), jnp.where(lo, zero, q)], axis=0)
            s = lax.dot_general(qs, kw, (((1,), (1,)), ((), ())), preferred_element_type=F32)
            s = s + jnp.concatenate([ga_ref[p, 0, variant], ga_ref[p, 1, variant]], axis=0)
            m = jnp.max(s, axis=-1, keepdims=True)
            e = jnp.exp(s - m)
            l = jnp.sum(e, axis=-1, keepdims=True)
            acc = jnp.dot(e.astype(BF16), vw, preferred_element_type=F32)
            o = acc / l
            lse = m + jnp.log(l)
            o_ref[res, qrows, cols] = jnp.where(lo, o[:LANES], o[LANES:]).astype(o_ref.dtype)
            lse_ref[res, qrows, cols] = jnp.where(lo, lse[:LANES], lse[LANES:])

    n_res = slab_ref.shape[0]
    n_sb = sub_len // LANES
    if n_res * n_sb <= A_UNROLL:
        for res in range(n_res):
            for sb in range(n_sb):
                sub_block(res, sb)
    else:
        def body(i, carry):
            for u in range(A_UNROLL):
                sub_block(0, A_UNROLL * i + u)
            return carry
        lax.fori_loop(0, n_sb // A_UNROLL, body, 0)


def _window_attn(slab, ga):
    b, dil, sub_len, n = slab.shape
    n_sb = sub_len // LANES
    n_res = max(1, min(dil, A_UNROLL // n_sb))
    assert n_res == 1 or n_res * n_sb <= A_UNROLL
    assert n_res > 1 or n_sb <= A_UNROLL or n_sb % A_UNROLL == 0
    out_spec = pl.BlockSpec((None, n_res, sub_len, A_COLS), lambda bi, r: (bi, r, 0, 0))
    return pl.pallas_call(
        functools.partial(_window_attn_kernel, sub_len=sub_len),
        grid=(b, dil // n_res),
        in_specs=[
            pl.BlockSpec((None, n_res, sub_len, n), lambda bi, r: (bi, r, 0, 0)),
            pl.BlockSpec(ga.shape, lambda bi, r: (0,) * ga.ndim),
        ],
        out_specs=[out_spec, out_spec],
        out_shape=[jax.ShapeDtypeStruct((b, dil, sub_len, A_COLS), BF16),
                   jax.ShapeDtypeStruct((b, dil, sub_len, A_COLS), F32)],
        compiler_params=_cparams(("parallel", "parallel")),
        name="window_attn",
    )(slab, ga)


def _post_kernel(*refs, mix_groups, final_norm, ff_chunk):
    refs = list(refs)
    h_ref = refs.pop(0)
    if mix_groups:
        og_refs = [refs.pop(0) for _ in A_GROUPS]
        lg_refs = [refs.pop(0) for _ in A_GROUPS]
    else:
        o_ref = refs.pop(0)
    wo_ref, g_ref, win_ref, wout_ref = refs[:4]
    refs = refs[4:]
    gf_ref = refs.pop(0) if final_norm else None
    out_ref = refs.pop(0)

    if mix_groups:
        o_scr, lse_scr = refs
        n_cb, tm, _ = o_scr.shape
        n = n_cb * LANES
        cb_per_group = A_COLS // LANES
        for gi, (_, dil, _) in enumerate(A_GROUPS):
            for r in range(dil):
                rows = pl.ds(r, tm // dil, stride=dil) if dil > 1 else slice(None)
                for cb in range(cb_per_group):
                    cols = slice(cb * LANES, (cb + 1) * LANES)
                    o_scr[gi * cb_per_group + cb, rows, :] = og_refs[gi][r, :, cols].astype(F32)
                    lse_scr[gi * cb_per_group + cb, rows, :] = lg_refs[gi][r, :, cols]
        o = jnp.concatenate([o_scr[cb] for cb in range(n_cb)], axis=1)
        lse = jnp.concatenate([lse_scr[cb] for cb in range(n_cb)], axis=1)
        lane = lax.broadcasted_iota(jnp.int32, (tm, n), 1)
        grp = (lane >= A_COLS).astype(jnp.int32) + (lane >= 2 * A_COLS).astype(jnp.int32)
        head = (lane - grp * A_COLS) >> 6
        n_heads = jnp.where(grp == 0, A_GROUPS[0][2],
                            jnp.where(grp == 1, A_GROUPS[1][2], A_GROUPS[2][2]))
        real = head < n_heads
        mx = jnp.max(jnp.where(real, lse, NEG), axis=-1, keepdims=True)
        e = jnp.where(real, jnp.exp(lse - mx), 0.0)
        s_g = [jnp.sum(jnp.where(grp == gi, e, 0.0), axis=-1, keepdims=True)
               / (HEAD_DIM * A_GROUPS[gi][2]) for gi in range(3)]
        tot = s_g[0] + s_g[1] + s_g[2]
        alpha = jnp.where(grp == 0, s_g[0], jnp.where(grp == 1, s_g[1], s_g[2])) / tot
        o = (o * (len(A_GROUPS) * alpha)).astype(BF16)
    else:
        o = o_ref[...]
    h1 = h_ref[...] + jnp.dot(o, wo_ref[...], preferred_element_type=F32)
    xn = _rms(h1, g_ref[...]).astype(BF16)
    acc = h1
    for c in range(0, D_FF, ff_chunk):
        u = jnp.dot(xn, win_ref[:, c:c + ff_chunk], preferred_element_type=F32)
        u = jnp.maximum(u, 0.0)
        u = (u * u).astype(BF16)
        acc = acc + jnp.dot(u, wout_ref[c:c + ff_chunk, :], preferred_element_type=F32)
    if final_norm:
        acc = _rms(acc, gf_ref[...])
    out_ref[...] = acc


def _post(h, o, w_o, g_mlp, w_in, w_out, *, lse=None, g_final=None, seq=None, tm=256,
          ff_chunk=1024):
    t, d = h.shape
    const = lambda i: (0, 0)
    single = dict(pipeline_mode=pl.Buffered(1))
    in_specs = [pl.BlockSpec((tm, d), lambda i: (i, 0))]
    args = [h]
    scratch = []
    if lse is not None:
        n_o = len(A_GROUPS) * A_COLS
        spb = seq // tm
        for arrs in (o, lse):
            for arr, (_, dil, _) in zip(arrs, A_GROUPS):
                in_specs.append(pl.BlockSpec((None, dil, tm // dil, A_COLS),
                                             lambda i: (i // spb, 0, i % spb, 0)))
                args.append(arr)
        scratch = [pltpu.VMEM((n_o // LANES, tm, LANES), F32)] * 2
    else:
        n_o = o.shape[1]
        in_specs.append(pl.BlockSpec((tm, n_o), lambda i: (i, 0)))
        args.append(o)
    in_specs += [pl.BlockSpec((n_o, d), const, **single),
                 pl.BlockSpec((1, d), const),
                 pl.BlockSpec((d, D_FF), const, **single),
                 pl.BlockSpec((D_FF, d), const, **single)]
    args += [w_o, g_mlp, w_in, w_out]
    if g_final is not None:
        in_specs.append(pl.BlockSpec((1, d), const))
        args.append(g_final)
    return pl.pallas_call(
        functools.partial(_post_kernel, mix_groups=lse is not None,
                          final_norm=g_final is not None, ff_chunk=ff_chunk),
        grid=(t // tm,),
        in_specs=in_specs,
        out_specs=pl.BlockSpec((tm, d), lambda i: (i, 0)),
        out_shape=jax.ShapeDtypeStruct((t, d), F32),
        scratch_shapes=scratch,
        compiler_params=_cparams(("parallel",)),
        name="post",
    )(*args)


def _t5_bucket(rel):
    nb = NUM_BUCKETS // 2
    max_exact = nb // 2
    side = jnp.where(rel > 0, nb, 0)
    n = jnp.abs(rel)
    nf = jnp.maximum(n, 1).astype(F32)
    large = max_exact + (jnp.log(nf / max_exact) / math.log(REL_MAX_DISTANCE / max_exact)
                         * (nb - max_exact)).astype(jnp.int32)
    large = jnp.minimum(large, nb - 1)
    return side + jnp.where(n < max_exact, n, large)


def _table_lookup(bucket, tab_ref, col):
    out = jnp.zeros(bucket.shape, F32)
    for b in range(NUM_BUCKETS):
        out = jnp.where(bucket == b, tab_ref[b, col], out)
    return out


def _diff_bias_kernel(tab_ref, o_ref):
    hj = pl.program_id(0)
    r = lax.broadcasted_iota(jnp.int32, (LANES, LANES), 0)
    c = lax.broadcasted_iota(jnp.int32, (LANES, LANES), 1)
    for e in range(B_EBLOCKS):
        rel = LANES * (e - B_ECLIP) + r - c
        o_ref[e] = _table_lookup(_t5_bucket(rel), tab_ref, hj) * LOG2E


def _diff_bias_blocks(rel_bias):
    n = rel_bias.shape[1]
    out = pl.pallas_call(
        _diff_bias_kernel,
        grid=(n,),
        in_specs=[pl.BlockSpec(memory_space=pltpu.SMEM)],
        out_specs=pl.BlockSpec((None, B_EBLOCKS, LANES, LANES), lambda i: (i, 0, 0, 0)),
        out_shape=jax.ShapeDtypeStruct((n, B_EBLOCKS, LANES, LANES), F32),
        compiler_params=_cparams(("parallel",)),
        name="diff_bias",
    )(rel_bias.astype(F32))
    return out.reshape(n // 2, 2, B_EBLOCKS, LANES, LANES)


A_WINDOW_OFFSETS = (-A_SIDE, 0, -LANES)


def _window_bias_kernel(tab_ref, o_ref):
    hp = pl.program_id(0)
    dil = jnp.where(hp < A_PAD_HEADS, A_GROUPS[0][1],
                    jnp.where(hp < 2 * A_PAD_HEADS, A_GROUPS[1][1], A_GROUPS[2][1]))
    r = lax.broadcasted_iota(jnp.int32, (LANES, A_WIN), 0)
    c = lax.broadcasted_iota(jnp.int32, (LANES, A_WIN), 1)
    for v, off in enumerate(A_WINDOW_OFFSETS):
        rel = c + off - r
        bias = _table_lookup(_t5_bucket(rel * dil), tab_ref, hp)
        o_ref[v] = jnp.where(jnp.abs(rel) <= A_SIDE, bias, NEG)


def _window_bias_tiles(rel_bias):
    cols = _pad_group_cols(rel_bias.astype(F32), 1, unit=1)
    table = jnp.concatenate(cols, axis=1)
    n = table.shape[1]
    out = pl.pallas_call(
        _window_bias_kernel,
        grid=(n,),
        in_specs=[pl.BlockSpec(memory_space=pltpu.SMEM)],
        out_specs=pl.BlockSpec((None, 3, LANES, A_WIN), lambda i: (i, 0, 0, 0)),
        out_shape=jax.ShapeDtypeStruct((n, 3, LANES, A_WIN), F32),
        compiler_params=_cparams(("parallel",)),
        name="window_bias",
    )(table)
    return out.reshape(len(A_GROUPS), A_PAD_HEADS // 2, 2, 3, LANES, A_WIN)


def _rope_tables(seq):
    n_rows = seq // GRID_W
    row = jnp.repeat(jnp.arange(n_rows, dtype=jnp.int32), GRID_W).astype(F32)
    col = jnp.tile(jnp.arange(GRID_W, dtype=jnp.int32), n_rows).astype(F32)
    half = HEAD_DIM // 4
    inv = ROPE_THETA ** (-jnp.arange(half, dtype=F32) / half)
    ang = jnp.concatenate([row[:, None] * inv, col[:, None] * inv], axis=-1)
    lane = jnp.arange(LANES, dtype=jnp.int32) % HEAD_DIM
    idx = (lane // 32) * half + lane % half
    sign = jnp.where((lane % 32) < half, -1.0, 1.0).astype(F32)
    return jnp.cos(ang)[:, idx], jnp.sin(ang)[:, idx] * sign


def _pad_group_cols(w, axis, unit=HEAD_DIM):
    parts = []
    h0 = 0
    for (_, _, nh) in A_GROUPS:
        sl = [slice(None)] * w.ndim
        sl[axis] = slice(h0 * unit, (h0 + nh) * unit)
        part = w[tuple(sl)]
        if nh < A_PAD_HEADS:
            pad = [(0, 0)] * w.ndim
            pad[axis] = (0, (A_PAD_HEADS - nh) * unit)
            part = jnp.pad(part, pad)
        parts.append(part)
        h0 += nh
    return parts


def _dilated_layer(h, g_mix, w_qkv, w_o, bias_tiles, batch, seq):
    d_attn = w_qkv.shape[1] // 3
    wq, wk, wv = (w_qkv[:, i * d_attn:(i + 1) * d_attn] for i in range(3))
    qs, ks, vs = _pad_group_cols(wq, 1), _pad_group_cols(wk, 1), _pad_group_cols(wv, 1)
    w = jnp.concatenate([jnp.concatenate([qs[g], ks[g], vs[g]], axis=1) for g in range(3)],
                        axis=1).astype(BF16)
    slab_cols = 3 * A_COLS
    scale = jnp.tile(jnp.concatenate([jnp.full((A_COLS,), Q_SCALE, F32),
                                      jnp.ones((2 * A_COLS,), F32)]), 3)[None, :]
    slabs = _norm_proj_groups(h, g_mix, w, scale, batch=batch, seq=seq)
    outs, lses = [], []
    for g in range(len(A_GROUPS)):
        o, lse = _window_attn(slabs[g], bias_tiles[g])
        outs.append(o)
        lses.append(lse)
    w_o_pad = jnp.concatenate(_pad_group_cols(w_o, 0), axis=0).astype(BF16)
    return outs, lses, w_o_pad


def kernel(x, rel_bias, norm_mix_g, norm_mlp_g, norm_final_g, a_w_qkv, a_w_o, b_w_qkv,
           b_lambda, b_subln_g, b_w_o, c_w_qkv, c_q_norm_g, c_k_norm_g, c_w_o, mlp_w_in,
           mlp_w_out):
    batch, seq, d = x.shape
    t = batch * seq
    h = x.reshape(t, d)
    cos, sin = _rope_tables(seq)
    window_bias = _window_bias_tiles(rel_bias)
    for i in range(N_LAYERS):
        kind, j = i % 3, i // 3
        g_mix = norm_mix_g[i][None, :]
        lse = None
        if kind == 0:
            o, lse, w_o = _dilated_layer(h, g_mix, a_w_qkv[j], a_w_o[j], window_bias, batch, seq)
        elif kind == 1:
            scale = jnp.concatenate([jnp.full((d,), Q_SCALE_LOG2, F32),
                                     jnp.ones((2 * d,), F32)])[None, :]
            qkv = _norm_proj(h, g_mix, b_w_qkv[j].astype(BF16), scale, chunk=1024)
            n_pairs = d // LANES
            o = _pair_attn(qkv.reshape(batch, seq, 3 * d), n_pairs=n_pairs,
                           k_block=lambda p: n_pairs + p, v_block=lambda p: 2 * n_pairs + p,
                           mode="diff", gb=_diff_bias_blocks(rel_bias), lam=b_lambda[j],
                           subln=b_subln_g[j][None, :], lambda_init=_lambda_init(i))
            o = o.reshape(t, d)
            w_o = b_w_o[j].astype(BF16)
        else:
            n_q = d
            n_kv = C_KV_HEADS * HEAD_DIM
            wq = c_w_qkv[j][:, :n_q]
            wk = c_w_qkv[j][:, n_q:n_q + n_kv].reshape(d, C_KV_HEADS, 1, HEAD_DIM)
            wv = c_w_qkv[j][:, n_q + n_kv:].reshape(d, C_KV_HEADS, 1, HEAD_DIM)
            wk = jnp.broadcast_to(wk, (d, C_KV_HEADS, 2, HEAD_DIM)).reshape(d, 2 * n_kv)
            wv = jnp.broadcast_to(wv, (d, C_KV_HEADS, 2, HEAD_DIM)).reshape(d, 2 * n_kv)
            w = jnp.concatenate([wq, wk, wv], axis=1).astype(BF16)
            qkv = _norm_proj_rope(h, g_mix, w, jnp.tile(c_q_norm_g[j], 2)[None, :],
                                  jnp.tile(c_k_norm_g[j], 2)[None, :], cos, sin,
                                  n_q=n_q, n_k=2 * n_kv, seq=seq)
            n_pairs = n_q // LANES
            pairs_per_kv = (n_q // n_kv) // 2
            o = _pair_attn(qkv.reshape(batch, seq, n_q + 4 * n_kv), n_pairs=n_pairs,
                           k_block=lambda p: n_pairs + p // pairs_per_kv,
                           v_block=lambda p: n_pairs + C_KV_HEADS + p // pairs_per_kv,
                           mode="gqa")
            o = o.reshape(t, d)
            w_o = c_w_o[j].astype(BF16)
        g_final = norm_final_g[None, :] if i == N_LAYERS - 1 else None
        h = _post(h, o, w_o, norm_mlp_g[i][None, :], mlp_w_in[i].astype(BF16),
                  mlp_w_out[i].astype(BF16), lse=lse, g_final=g_final, seq=seq)
    return h.reshape(batch, seq, d)
```

```python
import functools
import math

import jax
import jax.numpy as jnp
from jax import lax
from jax.experimental import pallas as pl
from jax.experimental.pallas import tpu as pltpu

F32 = jnp.float32
BF16 = jnp.bfloat16

D_MODEL = 1024
HEAD_DIM = 64
LANES = 128
EPS = 1e-6
NEG = -1e30
N_LAYERS = 4
D_FF = 4 * D_MODEL
A_GROUPS = ((128, 1, 6), (512, 4, 5), (2048, 16, 5))
A_PAD_HEADS = 6
A_COLS = A_PAD_HEADS * HEAD_DIM
A_WIN = 256
A_SIDE = 64
A_UNROLL = 8
NUM_BUCKETS = 32
REL_MAX_DISTANCE = 1024
B_EBLOCKS = 13
B_ECLIP = 6
C_KV_HEADS = 4
GRID_W = 64
ROPE_THETA = 10000.0

LOG2E = math.log2(math.e)
Q_SCALE = HEAD_DIM ** -0.5
Q_SCALE_LOG2 = Q_SCALE * LOG2E

ONES_ROWS = 16

VMEM_LIMIT = 56 * 1024 * 1024


def _cparams(sem):
    return pltpu.CompilerParams(dimension_semantics=sem, vmem_limit_bytes=VMEM_LIMIT)


def _lambda_init(layer_idx):
    return 0.8 - 0.6 * math.exp(-0.3 * layer_idx)


def _rms(x, g):
    ms = jnp.mean(x * x, axis=-1, keepdims=True)
    return x * lax.rsqrt(ms + EPS) * g


def _norm_proj_kernel(x_ref, g_ref, w_ref, cs_ref, o_ref, *, chunk):
    xn = _rms(x_ref[...], g_ref[...]).astype(BF16)
    n = o_ref.shape[-1]
    for c in range(0, n, chunk):
        y = jnp.dot(xn, w_ref[:, c:c + chunk], preferred_element_type=F32)
        o_ref[:, c:c + chunk] = (y * cs_ref[:, c:c + chunk]).astype(o_ref.dtype)


def _norm_proj(x, g, w, col_scale, *, tm=512, chunk=None):
    t, d = x.shape
    n = w.shape[1]
    chunk = chunk or n
    return pl.pallas_call(
        functools.partial(_norm_proj_kernel, chunk=chunk),
        grid=(t // tm,),
        in_specs=[
            pl.BlockSpec((tm, d), lambda i: (i, 0)),
            pl.BlockSpec((1, d), lambda i: (0, 0)),
            pl.BlockSpec((d, n), lambda i: (0, 0)),
            pl.BlockSpec((1, n), lambda i: (0, 0)),
        ],
        out_specs=pl.BlockSpec((tm, n), lambda i: (i, 0)),
        out_shape=jax.ShapeDtypeStruct((t, n), BF16),
        compiler_params=_cparams(("parallel",)),
        name="norm_proj",
    )(x, g, w, col_scale)


def _norm_proj_groups_kernel(x_ref, g_ref, w_ref, cs_ref, o0_ref, o1_ref, o2_ref, y_scr):
    xn = _rms(x_ref[...], g_ref[...]).astype(BF16)
    tm = x_ref.shape[0]
    n = 3 * A_COLS
    for gi, o_ref in enumerate((o0_ref, o1_ref, o2_ref)):
        dil = A_GROUPS[gi][1]
        cols = slice(gi * n, (gi + 1) * n)
        y = jnp.dot(xn, w_ref[:, cols], preferred_element_type=F32) * cs_ref[:, cols]
        if dil == 1:
            o_ref[0] = y.astype(o_ref.dtype)
        else:
            for cb in range(n // LANES):
                y_scr[cb] = y[:, cb * LANES:(cb + 1) * LANES]
            for r in range(dil):
                for cb in range(n // LANES):
                    o_ref[r, :, cb * LANES:(cb + 1) * LANES] = (
                        y_scr[cb, pl.ds(r, tm // dil, stride=dil), :].astype(o_ref.dtype))


def _norm_proj_groups(x, g, w, col_scale, *, batch, seq, tm=512):
    t, d = x.shape
    n = 3 * A_COLS
    spb = seq // tm
    const = lambda i: (0, 0)
    out_specs, out_shape = [], []
    for (_, dil, _) in A_GROUPS:
        out_specs.append(pl.BlockSpec((None, dil, tm // dil, n),
                                      lambda i: (i // spb, 0, i % spb, 0)))
        out_shape.append(jax.ShapeDtypeStruct((batch, dil, seq // dil, n), BF16))
    return pl.pallas_call(
        _norm_proj_groups_kernel,
        grid=(t // tm,),
        in_specs=[
            pl.BlockSpec((tm, d), lambda i: (i, 0)),
            pl.BlockSpec((1, d), const),
            pl.BlockSpec((d, 3 * n), const),
            pl.BlockSpec((1, 3 * n), const),
        ],
        out_specs=out_specs,
        out_shape=out_shape,
        scratch_shapes=[pltpu.VMEM((n // LANES, tm, LANES), F32)],
        compiler_params=_cparams(("parallel",)),
        name="norm_proj_groups",
    )(x, g, w, col_scale)


def _head_norm_rope(y, gain, cos, sin_signed, lane):
    lo = lane < HEAD_DIM
    y2 = y * y
    s0 = jnp.sum(jnp.where(lo, y2, 0.0), axis=-1, keepdims=True)
    s1 = jnp.sum(jnp.where(lo, 0.0, y2), axis=-1, keepdims=True)
    inv = jnp.where(lo, lax.rsqrt(s0 / HEAD_DIM + EPS), lax.rsqrt(s1 / HEAD_DIM + EPS))
    yn = y * inv * gain
    first = (lane & 31) < 16
    partner = jnp.where(first, pltpu.roll(yn, LANES - 16, 1), pltpu.roll(yn, 16, 1))
    return yn * cos + partner * sin_signed


def _norm_proj_rope_kernel(x_ref, g_ref, w_ref, qg_ref, kg_ref, cos_ref, sin_ref, o_ref,
                           *, n_q, n_k, chunk):
    xn = _rms(x_ref[...], g_ref[...]).astype(BF16)
    n = o_ref.shape[-1]
    tm = x_ref.shape[0]
    lane = lax.broadcasted_iota(jnp.int32, (tm, LANES), 1)
    cos = cos_ref[...]
    sin = sin_ref[...]
    for c in range(0, n, chunk):
        y = jnp.dot(xn, w_ref[:, c:c + chunk], preferred_element_type=F32)
        for b in range(0, chunk, LANES):
            col = c + b
            blk = y[:, b:b + LANES]
            if col < n_q:
                blk = _head_norm_rope(blk, qg_ref[...], cos, sin, lane) * Q_SCALE_LOG2
            elif col < n_q + n_k:
                blk = _head_norm_rope(blk, kg_ref[...], cos, sin, lane)
            o_ref[:, col:col + LANES] = blk.astype(o_ref.dtype)


def _norm_proj_rope(x, g, w, q_gain, k_gain, cos, sin, *, n_q, n_k, seq, tm=512, chunk=512):
    t, d = x.shape
    n = w.shape[1]
    sblk = seq // tm
    return pl.pallas_call(
        functools.partial(_norm_proj_rope_kernel, n_q=n_q, n_k=n_k, chunk=chunk),
        grid=(t // tm,),
        in_specs=[
            pl.BlockSpec((tm, d), lambda i: (i, 0)),
            pl.BlockSpec((1, d), lambda i: (0, 0)),
            pl.BlockSpec((d, n), lambda i: (0, 0)),
            pl.BlockSpec((1, LANES), lambda i: (0, 0)),
            pl.BlockSpec((1, LANES), lambda i: (0, 0)),
            pl.BlockSpec((tm, LANES), lambda i: (i % sblk, 0)),
            pl.BlockSpec((tm, LANES), lambda i: (i % sblk, 0)),
        ],
        out_specs=pl.BlockSpec((tm, n), lambda i: (i, 0)),
        out_shape=jax.ShapeDtypeStruct((t, n), BF16),
        compiler_params=_cparams(("parallel",)),
        name="norm_proj_rope",
    )(x, g, w, q_gain, k_gain, cos, sin)


def _pair_attn_kernel(*refs, tq, ts, tk, n_kt, mode, lambda_init):
    stagger = mode == "diff"
    vt_ref = refs[-1]
    if mode == "diff":
        q_ref, k_ref, v_ref, gb_ref, lam_ref, sg_ref, o_ref = refs[:-1]
    else:
        q_ref, k_ref, v_ref, o_ref = refs[:-1]
    qi = pl.program_id(2)
    seq = k_ref.shape[0]

    n_v = HEAD_DIM if mode == "gqa" else LANES
    kv_half = (pl.program_id(1) >> 1) & 1

    @pl.when(qi == 0)
    def _():
        for c in range(0, seq, tk):
            vt = v_ref[c:c + tk, :].astype(F32).T
            if mode == "gqa":
                vt = jnp.where(kv_half == 1, vt[HEAD_DIM:], vt[:HEAD_DIM])
            vt_ref[:n_v, c:c + tk] = vt.astype(BF16)
        vt_ref[n_v:, :] = jnp.ones((ONES_ROWS, seq), BF16)

    n_st = tq // ts
    lane = lax.broadcasted_iota(jnp.int32, (ts, LANES), 1)
    lo = lane < HEAD_DIM
    qs_st = []
    for st in range(n_st):
        q = q_ref[st * ts:(st + 1) * ts, :]
        zero = jnp.zeros_like(q)
        if mode == "gqa":
            q_sw = pltpu.roll(q.astype(F32), HEAD_DIM, 1).astype(q.dtype)
            in_half = lo == (kv_half == 0)
            h0 = jnp.where(kv_half == 0, q, q_sw)
            h1 = jnp.where(kv_half == 0, q_sw, q)
            qs = jnp.concatenate([jnp.where(in_half, h0, zero), jnp.where(in_half, h1, zero)],
                                 axis=0)
        else:
            qs = jnp.concatenate([jnp.where(lo, q, zero), jnp.where(lo, zero, q)], axis=0)
        qs_st.append(qs)
    rb_n = ts // LANES
    cb_n = tk // LANES

    def logits(kt, st):
        k = k_ref[kt * tk:(kt + 1) * tk, :]
        s = lax.dot_general(k, qs_st[st], (((1,), (1,)), ((), ())),
                            preferred_element_type=F32)
        if mode == "diff":
            base = kt * cb_n - (qi * n_st + st) * rb_n
            rows = []
            for cb in range(cb_n):
                blocks = []
                for j in range(2):
                    for rb in range(rb_n):
                        e = jnp.clip(base + (cb - rb), -B_ECLIP, B_ECLIP) + B_ECLIP
                        blocks.append(gb_ref[j, e])
                rows.append(jnp.concatenate(blocks, axis=1))
            s = s + jnp.concatenate(rows, axis=0)
        return s

    m = [jnp.full((1, 2 * ts), NEG, F32)] * n_st
    acc = [jnp.zeros((n_v + ONES_ROWS, 2 * ts), F32)] * n_st
    s_tile = {}

    def qk(kt, st):
        if kt < n_kt:
            s_tile[kt, st] = logits(kt, st)

    def softmax_pv(kt, st):
        if kt >= n_kt:
            return
        s = s_tile.pop((kt, st))
        m_new = jnp.maximum(m[st], jnp.max(s, axis=0, keepdims=True))
        alpha = jnp.exp2(m[st] - m_new)
        p = jnp.exp2(s - m_new).astype(BF16)
        vt = vt_ref[:, kt * tk:(kt + 1) * tk]
        acc[st] = alpha * acc[st] + jnp.dot(vt, p, preferred_element_type=F32)
        m[st] = m_new

    assert n_st == 2
    qk(0, 0)
    qk(0, 1)
    if stagger:
        softmax_pv(0, 0)
        for kt in range(n_kt):
            qk(kt + 1, 0)
            softmax_pv(kt, 1)
            qk(kt + 1, 1)
            softmax_pv(kt + 1, 0)
    else:
        for kt in range(n_kt):
            qk(kt + 1, 0)
            qk(kt + 1, 1)
            softmax_pv(kt, 0)
            softmax_pv(kt, 1)
    if mode == "diff":
        lam = lam_ref[...]
        lam_full = (jnp.exp(jnp.sum(lam[0:1] * lam[1:2], axis=-1, keepdims=True))
                    - jnp.exp(jnp.sum(lam[2:3] * lam[3:4], axis=-1, keepdims=True))
                    + lambda_init)
    for st in range(n_st):
        o = acc[st][:n_v] / acc[st][n_v:n_v + 1]
        if mode == "diff":
            a = o[:, :ts].T - lam_full * o[:, ts:].T
            y = _rms(a, sg_ref[...]) * (1.0 - lambda_init)
        else:
            y = jnp.concatenate([o[:, :ts], o[:, ts:]], axis=0).T
        o_ref[st * ts:(st + 1) * ts, :] = y.astype(o_ref.dtype)


def _pair_attn(qkv, *, n_pairs, k_block, v_block, mode, gb=None, lam=None, subln=None,
               lambda_init=0.0, tq=512, ts=256, tk=512):
    b, s, _ = qkv.shape
    in_specs = [
        pl.BlockSpec((None, tq, LANES), lambda bi, p, qi: (bi, qi, p)),
        pl.BlockSpec((None, s, LANES), lambda bi, p, qi: (bi, 0, k_block(p))),
        pl.BlockSpec((None, s, LANES), lambda bi, p, qi: (bi, 0, v_block(p))),
    ]
    args = [qkv, qkv, qkv]
    if mode == "diff":
        in_specs += [
            pl.BlockSpec((None, 2, B_EBLOCKS, LANES, LANES), lambda bi, p, qi: (p, 0, 0, 0, 0)),
            pl.BlockSpec((4, HEAD_DIM), lambda bi, p, qi: (0, 0)),
            pl.BlockSpec((1, LANES), lambda bi, p, qi: (0, 0)),
        ]
        args += [gb, lam, subln]
    return pl.pallas_call(
        functools.partial(_pair_attn_kernel, tq=tq, ts=ts, tk=tk, n_kt=s // tk, mode=mode,
                          lambda_init=lambda_init),
        grid=(b, n_pairs, s // tq),
        in_specs=in_specs,
        out_specs=pl.BlockSpec((None, tq, LANES), lambda bi, p, qi: (bi, qi, p)),
        out_shape=jax.ShapeDtypeStruct((b, s, n_pairs * LANES), BF16),
        scratch_shapes=[pltpu.VMEM(((HEAD_DIM if mode == "gqa" else LANES) + ONES_ROWS, s),
                                   BF16)],
        compiler_params=_cparams(("parallel", "parallel", "arbitrary")),
        name="pair_attn_" + mode,
    )(*args)


def _window_attn_kernel(slab_ref, ga_ref, o_ref, lse_ref, *, sub_len):
    lane = lax.broadcasted_iota(jnp.int32, (LANES, LANES), 1)
    lo = lane < HEAD_DIM
    n_p = A_COLS // LANES

    def sub_block(res, sb):
        static = isinstance(sb, int)
        q0 = sb * LANES
        if static:
            start = min(max(q0 - A_SIDE, 0), sub_len - A_WIN)
            variant = 1 if q0 == 0 else (2 if q0 == sub_len - LANES else 0)
            qrows, wrows = slice(q0, q0 + LANES), slice(start, start + A_WIN)
        else:
            q0 = pl.multiple_of(q0, LANES)
            start = pl.multiple_of(jnp.clip(q0 - A_SIDE, 0, sub_len - A_WIN), A_SIDE)
            variant = jnp.where(q0 == 0, 1, jnp.where(q0 == sub_len - LANES, 2, 0))
            qrows, wrows = pl.ds(q0, LANES), pl.ds(start, A_WIN)
        for p in range(n_p):
            cols = slice(p * LANES, (p + 1) * LANES)
            q = slab_ref[res, qrows, cols]
            kw = slab_ref[res, wrows, A_COLS + p * LANES:A_COLS + (p + 1) * LANES]
            vw = slab_ref[res, wrows, 2 * A_COLS + p * LANES:2 * A_COLS + (p + 1) * LANES]
            zero = jnp.zeros_like(q)
            qs = jnp.concatenate([jnp.where(lo, q, zero), jnp.where(lo, zero, q)], axis=0)
            s = lax.dot_general(qs, kw, (((1,), (1,)), ((), ())), preferred_element_type=F32)
            s = s + jnp.concatenate([ga_ref[p, 0, variant], ga_ref[p, 1, variant]], axis=0)
            m = jnp.max(s, axis=-1, keepdims=True)
            e = jnp.exp(s - m)
            l = jnp.sum(e, axis=-1, keepdims=True)
            acc = jnp.dot(e.astype(BF16), vw, preferred_element_type=F32)
            o = acc / l
            lse = m + jnp.log(l)
            o_ref[res, qrows, cols] = jnp.where(lo, o[:LANES], o[LANES:]).astype(o_ref.dtype)
            lse_ref[res, qrows, cols] = jnp.where(lo, lse[:LANES], lse[LANES:])

    n_res = slab_ref.shape[0]
    n_sb = sub_len // LANES
    if n_res * n_sb <= A_UNROLL:
        for res in range(n_res):
            for sb in range(n_sb):
                sub_block(res, sb)
    else:
        def body(i, carry):
            for u in range(A_UNROLL):
                sub_block(0, A_UNROLL * i + u)
            return carry
        lax.fori_loop(0, n_sb // A_UNROLL, body, 0)


def _window_attn(slab, ga):
    b, dil, sub_len, n = slab.shape
    n_sb = sub_len // LANES
    n_res = max(1, min(dil, A_UNROLL // n_sb))
    assert n_res == 1 or n_res * n_sb <= A_UNROLL
    assert n_res > 1 or n_sb <= A_UNROLL or n_sb % A_UNROLL == 0
    out_spec = pl.BlockSpec((None, n_res, sub_len, A_COLS), lambda bi, r: (bi, r, 0, 0))
    return pl.pallas_call(
        functools.partial(_window_attn_kernel, sub_len=sub_len),
        grid=(b, dil // n_res),
        in_specs=[
            pl.BlockSpec((None, n_res, sub_len, n), lambda bi, r: (bi, r, 0, 0)),
            pl.BlockSpec(ga.shape, lambda bi, r: (0,) * ga.ndim),
        ],
        out_specs=[out_spec, out_spec],
        out_shape=[jax.ShapeDtypeStruct((b, dil, sub_len, A_COLS), BF16),
                   jax.ShapeDtypeStruct((b, dil, sub_len, A_COLS), F32)],
        compiler_params=_cparams(("parallel", "parallel")),
        name="window_attn",
    )(slab, ga)


def _post_kernel(*refs, mix_groups, final_norm, ff_chunk):
    refs = list(refs)
    h_ref = refs.pop(0)
    if mix_groups:
        og_refs = [refs.pop(0) for _ in A_GROUPS]
        lg_refs = [refs.pop(0) for _ in A_GROUPS]
    else:
        o_ref = refs.pop(0)
    wo_ref, g_ref, win_ref, wout_ref = refs[:4]
    refs = refs[4:]
    gf_ref = refs.pop(0) if final_norm else None
    out_ref = refs.pop(0)

    if mix_groups:
        o_scr, lse_scr = refs
        n_cb, tm, _ = o_scr.shape
        n = n_cb * LANES
        cb_per_group = A_COLS // LANES
        for gi, (_, dil, _) in enumerate(A_GROUPS):
            for r in range(dil):
                rows = pl.ds(r, tm // dil, stride=dil) if dil > 1 else slice(None)
                for cb in range(cb_per_group):
                    cols = slice(cb * LANES, (cb + 1) * LANES)
                    o_scr[gi * cb_per_group + cb, rows, :] = og_refs[gi][r, :, cols].astype(F32)
                    lse_scr[gi * cb_per_group + cb, rows, :] = lg_refs[gi][r, :, cols]
        o = jnp.concatenate([o_scr[cb] for cb in range(n_cb)], axis=1)
        lse = jnp.concatenate([lse_scr[cb] for cb in range(n_cb)], axis=1)
        lane = lax.broadcasted_iota(jnp.int32, (tm, n), 1)
        grp = (lane >= A_COLS).astype(jnp.int32) + (lane >= 2 * A_COLS).astype(jnp.int32)
        head = (lane - grp * A_COLS) >> 6
        n_heads = jnp.where(grp == 0, A_GROUPS[0][2],
                            jnp.where(grp == 1, A_GROUPS[1][2], A_GROUPS[2][2]))
        real = head < n_heads
        mx = jnp.max(jnp.where(real, lse, NEG), axis=-1, keepdims=True)
        e = jnp.where(real, jnp.exp(lse - mx), 0.0)
        s_g = [jnp.sum(jnp.where(grp == gi, e, 0.0), axis=-1, keepdims=True)
               / (HEAD_DIM * A_GROUPS[gi][2]) for gi in range(3)]
        tot = s_g[0] + s_g[1] + s_g[2]
        alpha = jnp.where(grp == 0, s_g[0], jnp.where(grp == 1, s_g[1], s_g[2])) / tot
        o = (o * (len(A_GROUPS) * alpha)).astype(BF16)
    else:
        o = o_ref[...]
    h1 = h_ref[...] + jnp.dot(o, wo_ref[...], preferred_element_type=F32)
    xn = _rms(h1, g_ref[...]).astype(BF16)
    acc = h1
    for c in range(0, D_FF, ff_chunk):
        u = jnp.dot(xn, win_ref[:, c:c + ff_chunk], preferred_element_type=F32)
        u = jnp.maximum(u, 0.0)
        u = (u * u).astype(BF16)
        acc = acc + jnp.dot(u, wout_ref[c:c + ff_chunk, :], preferred_element_type=F32)
    if final_norm:
        acc = _rms(acc, gf_ref[...])
    out_ref[...] = acc


def _post(h, o, w_o, g_mlp, w_in, w_out, *, lse=None, g_final=None, seq=None, tm=512,
          ff_chunk=1024):
    t, d = h.shape
    const = lambda i: (0, 0)
    single = dict(pipeline_mode=pl.Buffered(1))
    in_specs = [pl.BlockSpec((tm, d), lambda i: (i, 0))]
    args = [h]
    scratch = []
    if lse is not None:
        n_o = len(A_GROUPS) * A_COLS
        spb = seq // tm
        for arrs in (o, lse):
            for arr, (_, dil, _) in zip(arrs, A_GROUPS):
                in_specs.append(pl.BlockSpec((None, dil, tm // dil, A_COLS),
                                             lambda i: (i // spb, 0, i % spb, 0)))
                args.append(arr)
        scratch = [pltpu.VMEM((n_o // LANES, tm, LANES), F32)] * 2
    else:
        n_o = o.shape[1]
        in_specs.append(pl.BlockSpec((tm, n_o), lambda i: (i, 0)))
        args.append(o)
    in_specs += [pl.BlockSpec((n_o, d), const, **single),
                 pl.BlockSpec((1, d), const),
                 pl.BlockSpec((d, D_FF), const, **single),
                 pl.BlockSpec((D_FF, d), const, **single)]
    args += [w_o, g_mlp, w_in, w_out]
    if g_final is not None:
        in_specs.append(pl.BlockSpec((1, d), const))
        args.append(g_final)
    return pl.pallas_call(
        functools.partial(_post_kernel, mix_groups=lse is not None,
                          final_norm=g_final is not None, ff_chunk=ff_chunk),
        grid=(t // tm,),
        in_specs=in_specs,
        out_specs=pl.BlockSpec((tm, d), lambda i: (i, 0)),
        out_shape=jax.ShapeDtypeStruct((t, d), F32),
        scratch_shapes=scratch,
        compiler_params=_cparams(("parallel",)),
        name="post",
    )(*args)


def _t5_bucket(rel):
    nb = NUM_BUCKETS // 2
    max_exact = nb // 2
    side = jnp.where(rel > 0, nb, 0)
    n = jnp.abs(rel)
    nf = jnp.maximum(n, 1).astype(F32)
    large = max_exact + (jnp.log(nf / max_exact) / math.log(REL_MAX_DISTANCE / max_exact)
                         * (nb - max_exact)).astype(jnp.int32)
    large = jnp.minimum(large, nb - 1)
    return side + jnp.where(n < max_exact, n, large)


def _table_lookup(bucket, tab_ref, col):
    out = jnp.zeros(bucket.shape, F32)
    for b in range(NUM_BUCKETS):
        out = jnp.where(bucket == b, tab_ref[b, col], out)
    return out


def _diff_bias_kernel(tab_ref, o_ref):
    hj = pl.program_id(0)
    r = lax.broadcasted_iota(jnp.int32, (LANES, LANES), 0)
    c = lax.broadcasted_iota(jnp.int32, (LANES, LANES), 1)
    for e in range(B_EBLOCKS):
        rel = LANES * (e - B_ECLIP) + r - c
        o_ref[e] = _table_lookup(_t5_bucket(rel), tab_ref, hj) * LOG2E


def _diff_bias_blocks(rel_bias):
    n = rel_bias.shape[1]
    out = pl.pallas_call(
        _diff_bias_kernel,
        grid=(n,),
        in_specs=[pl.BlockSpec(memory_space=pltpu.SMEM)],
        out_specs=pl.BlockSpec((None, B_EBLOCKS, LANES, LANES), lambda i: (i, 0, 0, 0)),
        out_shape=jax.ShapeDtypeStruct((n, B_EBLOCKS, LANES, LANES), F32),
        compiler_params=_cparams(("parallel",)),
        name="diff_bias",
    )(rel_bias.astype(F32))
    return out.reshape(n // 2, 2, B_EBLOCKS, LANES, LANES)


A_WINDOW_OFFSETS = (-A_SIDE, 0, -LANES)


def _window_bias_kernel(tab_ref, o_ref):
    hp = pl.program_id(0)
    dil = jnp.where(hp < A_PAD_HEADS, A_GROUPS[0][1],
                    jnp.where(hp < 2 * A_PAD_HEADS, A_GROUPS[1][1], A_GROUPS[2][1]))
    r = lax.broadcasted_iota(jnp.int32, (LANES, A_WIN), 0)
    c = lax.broadcasted_iota(jnp.int32, (LANES, A_WIN), 1)
    for v, off in enumerate(A_WINDOW_OFFSETS):
        rel = c + off - r
        bias = _table_lookup(_t5_bucket(rel * dil), tab_ref, hp)
        o_ref[v] = jnp.where(jnp.abs(rel) <= A_SIDE, bias, NEG)


def _window_bias_tiles(rel_bias):
    cols = _pad_group_cols(rel_bias.astype(F32), 1, unit=1)
    table = jnp.concatenate(cols, axis=1)
    n = table.shape[1]
    out = pl.pallas_call(
        _window_bias_kernel,
        grid=(n,),
        in_specs=[pl.BlockSpec(memory_space=pltpu.SMEM)],
        out_specs=pl.BlockSpec((None, 3, LANES, A_WIN), lambda i: (i, 0, 0, 0)),
        out_shape=jax.ShapeDtypeStruct((n, 3, LANES, A_WIN), F32),
        compiler_params=_cparams(("parallel",)),
        name="window_bias",
    )(table)
    return out.reshape(len(A_GROUPS), A_PAD_HEADS // 2, 2, 3, LANES, A_WIN)


def _rope_tables(seq):
    n_rows = seq // GRID_W
    row = jnp.repeat(jnp.arange(n_rows, dtype=jnp.int32), GRID_W).astype(F32)
    col = jnp.tile(jnp.arange(GRID_W, dtype=jnp.int32), n_rows).astype(F32)
    half = HEAD_DIM // 4
    inv = ROPE_THETA ** (-jnp.arange(half, dtype=F32) / half)
    ang = jnp.concatenate([row[:, None] * inv, col[:, None] * inv], axis=-1)
    lane = jnp.arange(LANES, dtype=jnp.int32) % HEAD_DIM
    idx = (lane // 32) * half + lane % half
    sign = jnp.where((lane % 32) < half, -1.0, 1.0).astype(F32)
    return jnp.cos(ang)[:, idx], jnp.sin(ang)[:, idx] * sign


def _pad_group_cols(w, axis, unit=HEAD_DIM):
    parts = []
    h0 = 0
    for (_, _, nh) in A_GROUPS:
        sl = [slice(None)] * w.ndim
        sl[axis] = slice(h0 * unit, (h0 + nh) * unit)
        part = w[tuple(sl)]
        if nh < A_PAD_HEADS:
            pad = [(0, 0)] * w.ndim
            pad[axis] = (0, (A_PAD_HEADS - nh) * unit)
            part = jnp.pad(part, pad)
        parts.append(part)
        h0 += nh
    return parts


def _dilated_layer(h, g_mix, w_qkv, w_o, bias_tiles, batch, seq):
    d_attn = w_qkv.shape[1] // 3
    wq, wk, wv = (w_qkv[:, i * d_attn:(i + 1) * d_attn] for i in range(3))
    qs, ks, vs = _pad_group_cols(wq, 1), _pad_group_cols(wk, 1), _pad_group_cols(wv, 1)
    w = jnp.concatenate([jnp.concatenate([qs[g], ks[g], vs[g]], axis=1) for g in range(3)],
                        axis=1).astype(BF16)
    slab_cols = 3 * A_COLS
    scale = jnp.tile(jnp.concatenate([jnp.full((A_COLS,), Q_SCALE, F32),
                                      jnp.ones((2 * A_COLS,), F32)]), 3)[None, :]
    slabs = _norm_proj_groups(h, g_mix, w, scale, batch=batch, seq=seq)
    outs, lses = [], []
    for g in range(len(A_GROUPS)):
        o, lse = _window_attn(slabs[g], bias_tiles[g])
        outs.append(o)
        lses.append(lse)
    w_o_pad = jnp.concatenate(_pad_group_cols(w_o, 0), axis=0).astype(BF16)
    return outs, lses, w_o_pad


def kernel(x, rel_bias, norm_mix_g, norm_mlp_g, norm_final_g, a_w_qkv, a_w_o, b_w_qkv,
           b_lambda, b_subln_g, b_w_o, c_w_qkv, c_q_norm_g, c_k_norm_g, c_w_o, mlp_w_in,
           mlp_w_out):
    batch, seq, d = x.shape
    t = batch * seq
    h = x.reshape(t, d)
    cos, sin = _rope_tables(seq)
    window_bias = _window_bias_tiles(rel_bias)
    for i in range(N_LAYERS):
        kind, j = i % 3, i // 3
        g_mix = norm_mix_g[i][None, :]
        lse = None
        if kind == 0:
            o, lse, w_o = _dilated_layer(h, g_mix, a_w_qkv[j], a_w_o[j], window_bias, batch, seq)
        elif kind == 1:
            scale = jnp.concatenate([jnp.full((d,), Q_SCALE_LOG2, F32),
                                     jnp.ones((2 * d,), F32)])[None, :]
            qkv = _norm_proj(h, g_mix, b_w_qkv[j].astype(BF16), scale, chunk=1024)
            n_pairs = d // LANES
            o = _pair_attn(qkv.reshape(batch, seq, 3 * d), n_pairs=n_pairs,
                           k_block=lambda p: n_pairs + p, v_block=lambda p: 2 * n_pairs + p,
                           mode="diff", gb=_diff_bias_blocks(rel_bias), lam=b_lambda[j],
                           subln=b_subln_g[j][None, :], lambda_init=_lambda_init(i))
            o = o.reshape(t, d)
            w_o = b_w_o[j].astype(BF16)
        else:
            n_q = d
            n_kv = C_KV_HEADS * HEAD_DIM
            qkv = _norm_proj_rope(h, g_mix, c_w_qkv[j].astype(BF16),
                                  jnp.tile(c_q_norm_g[j], 2)[None, :],
                                  jnp.tile(c_k_norm_g[j], 2)[None, :], cos, sin,
                                  n_q=n_q, n_k=n_kv, seq=seq, chunk=512)
            n_pairs = n_q // LANES
            kv_blocks = n_kv // LANES
            o = _pair_attn(qkv.reshape(batch, seq, n_q + 2 * n_kv), n_pairs=n_pairs,
                           k_block=lambda p: n_pairs + p // 4,
                           v_block=lambda p: n_pairs + kv_blocks + p // 4,
                           mode="gqa")
            o = o.reshape(t, d)
            w_o = c_w_o[j].astype(BF16)
        g_final = norm_final_g[None, :] if i == N_LAYERS - 1 else None
        h = _post(h, o, w_o, norm_mlp_g[i][None, :], mlp_w_in[i].astype(BF16),
                  mlp_w_out[i].astype(BF16), lse=lse, g_final=g_final, seq=seq)
    return h.reshape(batch, seq, d)
```

```python
import functools
import math

import jax
import jax.numpy as jnp
from jax import lax
from jax.experimental import pallas as pl
from jax.experimental.pallas import tpu as pltpu

F32 = jnp.float32
BF16 = jnp.bfloat16

D_MODEL = 1024
HEAD_DIM = 64
LANES = 128
EPS = 1e-6
NEG = -1e30
N_LAYERS = 4
D_FF = 4 * D_MODEL
A_GROUPS = ((128, 1, 6), (512, 4, 5), (2048, 16, 5))
A_PAD_HEADS = 6
A_COLS = A_PAD_HEADS * HEAD_DIM
A_WIN = 256
A_SIDE = 64
A_UNROLL = 8
NUM_BUCKETS = 32
REL_MAX_DISTANCE = 1024
B_EBLOCKS = 13
B_ECLIP = 6
C_KV_HEADS = 4
GRID_W = 64
ROPE_THETA = 10000.0

LOG2E = math.log2(math.e)
Q_SCALE = HEAD_DIM ** -0.5
Q_SCALE_LOG2 = Q_SCALE * LOG2E

ONES_ROWS = 16

VMEM_LIMIT = 56 * 1024 * 1024


def _cparams(sem):
    return pltpu.CompilerParams(dimension_semantics=sem, vmem_limit_bytes=VMEM_LIMIT)


def _lambda_init(layer_idx):
    return 0.8 - 0.6 * math.exp(-0.3 * layer_idx)


def _rms(x, g):
    ms = jnp.mean(x * x, axis=-1, keepdims=True)
    return x * lax.rsqrt(ms + EPS) * g


def _norm_proj_kernel(x_ref, g_ref, w_ref, cs_ref, o_ref, *, chunk):
    xn = _rms(x_ref[...], g_ref[...]).astype(BF16)
    n = o_ref.shape[-1]
    for c in range(0, n, chunk):
        y = jnp.dot(xn, w_ref[:, c:c + chunk], preferred_element_type=F32)
        o_ref[:, c:c + chunk] = (y * cs_ref[:, c:c + chunk]).astype(o_ref.dtype)


def _norm_proj(x, g, w, col_scale, *, tm=512, chunk=None):
    t, d = x.shape
    n = w.shape[1]
    chunk = chunk or n
    return pl.pallas_call(
        functools.partial(_norm_proj_kernel, chunk=chunk),
        grid=(t // tm,),
        in_specs=[
            pl.BlockSpec((tm, d), lambda i: (i, 0)),
            pl.BlockSpec((1, d), lambda i: (0, 0)),
            pl.BlockSpec((d, n), lambda i: (0, 0)),
            pl.BlockSpec((1, n), lambda i: (0, 0)),
        ],
        out_specs=pl.BlockSpec((tm, n), lambda i: (i, 0)),
        out_shape=jax.ShapeDtypeStruct((t, n), BF16),
        compiler_params=_cparams(("parallel",)),
        name="norm_proj",
    )(x, g, w, col_scale)


def _norm_proj_groups_kernel(x_ref, g_ref, w_ref, cs_ref, o0_ref, o1_ref, o2_ref, y_scr):
    xn = _rms(x_ref[...], g_ref[...]).astype(BF16)
    tm = x_ref.shape[0]
    n = 3 * A_COLS
    for gi, o_ref in enumerate((o0_ref, o1_ref, o2_ref)):
        dil = A_GROUPS[gi][1]
        cols = slice(gi * n, (gi + 1) * n)
        y = jnp.dot(xn, w_ref[:, cols], preferred_element_type=F32) * cs_ref[:, cols]
        if dil == 1:
            o_ref[0] = y.astype(o_ref.dtype)
        else:
            for cb in range(n // LANES):
                y_scr[cb] = y[:, cb * LANES:(cb + 1) * LANES]
            for r in range(dil):
                for cb in range(n // LANES):
                    o_ref[r, :, cb * LANES:(cb + 1) * LANES] = (
                        y_scr[cb, pl.ds(r, tm // dil, stride=dil), :].astype(o_ref.dtype))


def _norm_proj_groups(x, g, w, col_scale, *, batch, seq, tm=512):
    t, d = x.shape
    n = 3 * A_COLS
    spb = seq // tm
    const = lambda i: (0, 0)
    out_specs, out_shape = [], []
    for (_, dil, _) in A_GROUPS:
        out_specs.append(pl.BlockSpec((None, dil, tm // dil, n),
                                      lambda i: (i // spb, 0, i % spb, 0)))
        out_shape.append(jax.ShapeDtypeStruct((batch, dil, seq // dil, n), BF16))
    return pl.pallas_call(
        _norm_proj_groups_kernel,
        grid=(t // tm,),
        in_specs=[
            pl.BlockSpec((tm, d), lambda i: (i, 0)),
            pl.BlockSpec((1, d), const),
            pl.BlockSpec((d, 3 * n), const),
            pl.BlockSpec((1, 3 * n), const),
        ],
        out_specs=out_specs,
        out_shape=out_shape,
        scratch_shapes=[pltpu.VMEM((n // LANES, tm, LANES), F32)],
        compiler_params=_cparams(("parallel",)),
        name="norm_proj_groups",
    )(x, g, w, col_scale)


def _head_norm_rope(y, gain, cos, sin_signed, lane):
    lo = lane < HEAD_DIM
    y2 = y * y
    s0 = jnp.sum(jnp.where(lo, y2, 0.0), axis=-1, keepdims=True)
    s1 = jnp.sum(jnp.where(lo, 0.0, y2), axis=-1, keepdims=True)
    inv = jnp.where(lo, lax.rsqrt(s0 / HEAD_DIM + EPS), lax.rsqrt(s1 / HEAD_DIM + EPS))
    yn = y * inv * gain
    first = (lane & 31) < 16
    partner = jnp.where(first, pltpu.roll(yn, LANES - 16, 1), pltpu.roll(yn, 16, 1))
    return yn * cos + partner * sin_signed


def _norm_proj_rope_kernel(x_ref, g_ref, w_ref, qg_ref, kg_ref, cos_ref, sin_ref, o_ref,
                           *, n_q, n_k, chunk):
    xn = _rms(x_ref[...], g_ref[...]).astype(BF16)
    n = o_ref.shape[-1]
    tm = x_ref.shape[0]
    lane = lax.broadcasted_iota(jnp.int32, (tm, LANES), 1)
    cos = cos_ref[...]
    sin = sin_ref[...]
    for c in range(0, n, chunk):
        y = jnp.dot(xn, w_ref[:, c:c + chunk], preferred_element_type=F32)
        for b in range(0, chunk, LANES):
            col = c + b
            blk = y[:, b:b + LANES]
            if col < n_q:
                blk = _head_norm_rope(blk, qg_ref[...], cos, sin, lane) * Q_SCALE_LOG2
            elif col < n_q + n_k:
                blk = _head_norm_rope(blk, kg_ref[...], cos, sin, lane)
            o_ref[:, col:col + LANES] = blk.astype(o_ref.dtype)


def _norm_proj_rope(x, g, w, q_gain, k_gain, cos, sin, *, n_q, n_k, seq, tm=512, chunk=512):
    t, d = x.shape
    n = w.shape[1]
    sblk = seq // tm
    return pl.pallas_call(
        functools.partial(_norm_proj_rope_kernel, n_q=n_q, n_k=n_k, chunk=chunk),
        grid=(t // tm,),
        in_specs=[
            pl.BlockSpec((tm, d), lambda i: (i, 0)),
            pl.BlockSpec((1, d), lambda i: (0, 0)),
            pl.BlockSpec((d, n), lambda i: (0, 0)),
            pl.BlockSpec((1, LANES), lambda i: (0, 0)),
            pl.BlockSpec((1, LANES), lambda i: (0, 0)),
            pl.BlockSpec((tm, LANES), lambda i: (i % sblk, 0)),
            pl.BlockSpec((tm, LANES), lambda i: (i % sblk, 0)),
        ],
        out_specs=pl.BlockSpec((tm, n), lambda i: (i, 0)),
        out_shape=jax.ShapeDtypeStruct((t, n), BF16),
        compiler_params=_cparams(("parallel",)),
        name="norm_proj_rope",
    )(x, g, w, q_gain, k_gain, cos, sin)


def _pair_attn_kernel(*refs, tq, ts, tk, n_kt, mode, lambda_init):
    vt_ref = refs[-1]
    if mode == "diff":
        q_ref, k_ref, v_ref, gb_ref, lam_ref, sg_ref, o_ref = refs[:-1]
    else:
        q_ref, k_ref, v_ref, o_ref = refs[:-1]
    qi = pl.program_id(2)
    seq = k_ref.shape[0]

    n_v = HEAD_DIM if mode == "gqa" else LANES
    kv_half = (pl.program_id(1) >> 1) & 1

    @pl.when(qi == 0)
    def _():
        for c in range(0, seq, tk):
            vt = v_ref[c:c + tk, :].astype(F32).T
            if mode == "gqa":
                vt = jnp.where(kv_half == 1, vt[HEAD_DIM:], vt[:HEAD_DIM])
            vt_ref[:n_v, c:c + tk] = vt.astype(BF16)
        vt_ref[n_v:, :] = jnp.ones((ONES_ROWS, seq), BF16)

    n_st = tq // ts
    lane = lax.broadcasted_iota(jnp.int32, (ts, LANES), 1)
    lo = lane < HEAD_DIM
    qs_st = []
    for st in range(n_st):
        q = q_ref[st * ts:(st + 1) * ts, :]
        zero = jnp.zeros_like(q)
        if mode == "gqa":
            q_sw = pltpu.roll(q.astype(F32), HEAD_DIM, 1).astype(q.dtype)
            in_half = lo == (kv_half == 0)
            h0 = jnp.where(kv_half == 0, q, q_sw)
            h1 = jnp.where(kv_half == 0, q_sw, q)
            qs = jnp.concatenate([jnp.where(in_half, h0, zero), jnp.where(in_half, h1, zero)],
                                 axis=0)
        else:
            qs = jnp.concatenate([jnp.where(lo, q, zero), jnp.where(lo, zero, q)], axis=0)
        qs_st.append(qs)
    rb_n = ts // LANES
    cb_n = tk // LANES

    def logits(kt, st):
        k = k_ref[kt * tk:(kt + 1) * tk, :]
        s = lax.dot_general(k, qs_st[st], (((1,), (1,)), ((), ())),
                            preferred_element_type=F32)
        if mode == "diff":
            base = kt * cb_n - (qi * n_st + st) * rb_n
            rows = []
            for cb in range(cb_n):
                blocks = []
                for j in range(2):
                    for rb in range(rb_n):
                        e = jnp.clip(base + (cb - rb), -B_ECLIP, B_ECLIP) + B_ECLIP
                        blocks.append(gb_ref[j, e])
                rows.append(jnp.concatenate(blocks, axis=1))
            s = s + jnp.concatenate(rows, axis=0)
        return s

    m = [jnp.full((1, 2 * ts), NEG, F32)] * n_st
    acc = [jnp.zeros((n_v + ONES_ROWS, 2 * ts), F32)] * n_st
    s_tile = {}

    def qk(kt, st):
        if kt < n_kt:
            s_tile[kt, st] = logits(kt, st)

    def softmax_pv(kt, st):
        if kt >= n_kt:
            return
        s = s_tile.pop((kt, st))
        m_new = jnp.maximum(m[st], jnp.max(s, axis=0, keepdims=True))
        alpha = jnp.exp2(m[st] - m_new)
        p = jnp.exp2(s - m_new).astype(BF16)
        vt = vt_ref[:, kt * tk:(kt + 1) * tk]
        acc[st] = alpha * acc[st] + jnp.dot(vt, p, preferred_element_type=F32)
        m[st] = m_new

    for st in range(n_st):
        qk(0, st)
    softmax_pv(0, 0)
    for kt in range(n_kt):
        for st in range(n_st):
            qk(kt + 1, st)
            if st + 1 < n_st:
                softmax_pv(kt, st + 1)
            else:
                softmax_pv(kt + 1, 0)
    if mode == "diff":
        lam = lam_ref[...]
        lam_full = (jnp.exp(jnp.sum(lam[0:1] * lam[1:2], axis=-1, keepdims=True))
                    - jnp.exp(jnp.sum(lam[2:3] * lam[3:4], axis=-1, keepdims=True))
                    + lambda_init)
    for st in range(n_st):
        o = acc[st][:n_v] / acc[st][n_v:n_v + 1]
        if mode == "diff":
            a = o[:, :ts].T - lam_full * o[:, ts:].T
            y = _rms(a, sg_ref[...]) * (1.0 - lambda_init)
        else:
            y = jnp.concatenate([o[:, :ts], o[:, ts:]], axis=0).T
        o_ref[st * ts:(st + 1) * ts, :] = y.astype(o_ref.dtype)


def _pair_attn(qkv, *, n_pairs, k_block, v_block, mode, gb=None, lam=None, subln=None,
               lambda_init=0.0, tq=1024, ts=256, tk=512):
    b, s, _ = qkv.shape
    in_specs = [
        pl.BlockSpec((None, tq, LANES), lambda bi, p, qi: (bi, qi, p)),
        pl.BlockSpec((None, s, LANES), lambda bi, p, qi: (bi, 0, k_block(p))),
        pl.BlockSpec((None, s, LANES), lambda bi, p, qi: (bi, 0, v_block(p))),
    ]
    args = [qkv, qkv, qkv]
    if mode == "diff":
        in_specs += [
            pl.BlockSpec((None, 2, B_EBLOCKS, LANES, LANES), lambda bi, p, qi: (p, 0, 0, 0, 0)),
            pl.BlockSpec((4, HEAD_DIM), lambda bi, p, qi: (0, 0)),
            pl.BlockSpec((1, LANES), lambda bi, p, qi: (0, 0)),
        ]
        args += [gb, lam, subln]
    return pl.pallas_call(
        functools.partial(_pair_attn_kernel, tq=tq, ts=ts, tk=tk, n_kt=s // tk, mode=mode,
                          lambda_init=lambda_init),
        grid=(b, n_pairs, s // tq),
        in_specs=in_specs,
        out_specs=pl.BlockSpec((None, tq, LANES), lambda bi, p, qi: (bi, qi, p)),
        out_shape=jax.ShapeDtypeStruct((b, s, n_pairs * LANES), BF16),
        scratch_shapes=[pltpu.VMEM(((HEAD_DIM if mode == "gqa" else LANES) + ONES_ROWS, s),
                                   BF16)],
        compiler_params=_cparams(("parallel", "parallel", "arbitrary")),
        name="pair_attn_" + mode,
    )(*args)


def _window_attn_kernel(slab_ref, ga_ref, o_ref, lse_ref, *, sub_len):
    lane = lax.broadcasted_iota(jnp.int32, (LANES, LANES), 1)
    lo = lane < HEAD_DIM
    n_p = A_COLS // LANES

    def sub_block(res, sb):
        static = isinstance(sb, int)
        q0 = sb * LANES
        if static:
            start = min(max(q0 - A_SIDE, 0), sub_len - A_WIN)
            variant = 1 if q0 == 0 else (2 if q0 == sub_len - LANES else 0)
            qrows, wrows = slice(q0, q0 + LANES), slice(start, start + A_WIN)
        else:
            q0 = pl.multiple_of(q0, LANES)
            start = pl.multiple_of(jnp.clip(q0 - A_SIDE, 0, sub_len - A_WIN), A_SIDE)
            variant = jnp.where(q0 == 0, 1, jnp.where(q0 == sub_len - LANES, 2, 0))
            qrows, wrows = pl.ds(q0, LANES), pl.ds(start, A_WIN)
        for p in range(n_p):
            cols = slice(p * LANES, (p + 1) * LANES)
            q = slab_ref[res, qrows, cols]
            kw = slab_ref[res, wrows, A_COLS + p * LANES:A_COLS + (p + 1) * LANES]
            vw = slab_ref[res, wrows, 2 * A_COLS + p * LANES:2 * A_COLS + (p + 1) * LANES]
            zero = jnp.zeros_like(q)
            qs = jnp.concatenate([jnp.where(lo, q, zero), jnp.where(lo, zero, q)], axis=0)
            s = lax.dot_general(qs, kw, (((1,), (1,)), ((), ())), preferred_element_type=F32)
            s = s + jnp.concatenate([ga_ref[p, 0, variant], ga_ref[p, 1, variant]], axis=0)
            m = jnp.max(s, axis=-1, keepdims=True)
            e = jnp.exp(s - m)
            l = jnp.sum(e, axis=-1, keepdims=True)
            acc = jnp.dot(e.astype(BF16), vw, preferred_element_type=F32)
            o = acc / l
            lse = m + jnp.log(l)
            o_ref[res, qrows, cols] = jnp.where(lo, o[:LANES], o[LANES:]).astype(o_ref.dtype)
            lse_ref[res, qrows, cols] = jnp.where(lo, lse[:LANES], lse[LANES:])

    n_res = slab_ref.shape[0]
    n_sb = sub_len // LANES
    if n_res * n_sb <= A_UNROLL:
        for res in range(n_res):
            for sb in range(n_sb):
                sub_block(res, sb)
    else:
        def body(i, carry):
            for u in range(A_UNROLL):
                sub_block(0, A_UNROLL * i + u)
            return carry
        lax.fori_loop(0, n_sb // A_UNROLL, body, 0)


def _window_attn(slab, ga):
    b, dil, sub_len, n = slab.shape
    n_sb = sub_len // LANES
    n_res = max(1, min(dil, A_UNROLL // n_sb))
    assert n_res == 1 or n_res * n_sb <= A_UNROLL
    assert n_res > 1 or n_sb <= A_UNROLL or n_sb % A_UNROLL == 0
    out_spec = pl.BlockSpec((None, n_res, sub_len, A_COLS), lambda bi, r: (bi, r, 0, 0))
    return pl.pallas_call(
        functools.partial(_window_attn_kernel, sub_len=sub_len),
        grid=(b, dil // n_res),
        in_specs=[
            pl.BlockSpec((None, n_res, sub_len, n), lambda bi, r: (bi, r, 0, 0)),
            pl.BlockSpec(ga.shape, lambda bi, r: (0,) * ga.ndim),
        ],
        out_specs=[out_spec, out_spec],
        out_shape=[jax.ShapeDtypeStruct((b, dil, sub_len, A_COLS), BF16),
                   jax.ShapeDtypeStruct((b, dil, sub_len, A_COLS), F32)],
        compiler_params=_cparams(("parallel", "parallel")),
        name="window_attn",
    )(slab, ga)


def _post_kernel(*refs, mix_groups, final_norm, ff_chunk):
    refs = list(refs)
    h_ref = refs.pop(0)
    if mix_groups:
        og_refs = [refs.pop(0) for _ in A_GROUPS]
        lg_refs = [refs.pop(0) for _ in A_GROUPS]
    else:
        o_ref = refs.pop(0)
    wo_ref, g_ref, win_ref, wout_ref = refs[:4]
    refs = refs[4:]
    gf_ref = refs.pop(0) if final_norm else None
    out_ref = refs.pop(0)

    if mix_groups:
        o_scr, lse_scr = refs
        n_cb, tm, _ = o_scr.shape
        n = n_cb * LANES
        cb_per_group = A_COLS // LANES
        for gi, (_, dil, _) in enumerate(A_GROUPS):
            for r in range(dil):
                rows = pl.ds(r, tm // dil, stride=dil) if dil > 1 else slice(None)
                for cb in range(cb_per_group):
                    cols = slice(cb * LANES, (cb + 1) * LANES)
                    o_scr[gi * cb_per_group + cb, rows, :] = og_refs[gi][r, :, cols].astype(F32)
                    lse_scr[gi * cb_per_group + cb, rows, :] = lg_refs[gi][r, :, cols]
        o = jnp.concatenate([o_scr[cb] for cb in range(n_cb)], axis=1)
        lse = jnp.concatenate([lse_scr[cb] for cb in range(n_cb)], axis=1)
        lane = lax.broadcasted_iota(jnp.int32, (tm, n), 1)
        grp = (lane >= A_COLS).astype(jnp.int32) + (lane >= 2 * A_COLS).astype(jnp.int32)
        head = (lane - grp * A_COLS) >> 6
        n_heads = jnp.where(grp == 0, A_GROUPS[0][2],
                            jnp.where(grp == 1, A_GROUPS[1][2], A_GROUPS[2][2]))
        real = head < n_heads
        mx = jnp.max(jnp.where(real, lse, NEG), axis=-1, keepdims=True)
        e = jnp.where(real, jnp.exp(lse - mx), 0.0)
        s_g = [jnp.sum(jnp.where(grp == gi, e, 0.0), axis=-1, keepdims=True)
               / (HEAD_DIM * A_GROUPS[gi][2]) for gi in range(3)]
        tot = s_g[0] + s_g[1] + s_g[2]
        alpha = jnp.where(grp == 0, s_g[0], jnp.where(grp == 1, s_g[1], s_g[2])) / tot
        o = (o * (len(A_GROUPS) * alpha)).astype(BF16)
    else:
        o = o_ref[...]
    h1 = h_ref[...] + jnp.dot(o, wo_ref[...], preferred_element_type=F32)
    xn = _rms(h1, g_ref[...]).astype(BF16)
    acc = h1
    for c in range(0, D_FF, ff_chunk):
        u = jnp.dot(xn, win_ref[:, c:c + ff_chunk], preferred_element_type=F32)
        u = jnp.maximum(u, 0.0)
        u = (u * u).astype(BF16)
        acc = acc + jnp.dot(u, wout_ref[c:c + ff_chunk, :], preferred_element_type=F32)
    if final_norm:
        acc = _rms(acc, gf_ref[...])
    out_ref[...] = acc


def _post(h, o, w_o, g_mlp, w_in, w_out, *, lse=None, g_final=None, seq=None, tm=512,
          ff_chunk=1024):
    t, d = h.shape
    const = lambda i: (0, 0)
    single = dict(pipeline_mode=pl.Buffered(1))
    in_specs = [pl.BlockSpec((tm, d), lambda i: (i, 0))]
    args = [h]
    scratch = []
    if lse is not None:
        n_o = len(A_GROUPS) * A_COLS
        spb = seq // tm
        for arrs in (o, lse):
            for arr, (_, dil, _) in zip(arrs, A_GROUPS):
                in_specs.append(pl.BlockSpec((None, dil, tm // dil, A_COLS),
                                             lambda i: (i // spb, 0, i % spb, 0)))
                args.append(arr)
        scratch = [pltpu.VMEM((n_o // LANES, tm, LANES), F32)] * 2
    else:
        n_o = o.shape[1]
        in_specs.append(pl.BlockSpec((tm, n_o), lambda i: (i, 0)))
        args.append(o)
    in_specs += [pl.BlockSpec((n_o, d), const, **single),
                 pl.BlockSpec((1, d), const),
                 pl.BlockSpec((d, D_FF), const, **single),
                 pl.BlockSpec((D_FF, d), const, **single)]
    args += [w_o, g_mlp, w_in, w_out]
    if g_final is not None:
        in_specs.append(pl.BlockSpec((1, d), const))
        args.append(g_final)
    return pl.pallas_call(
        functools.partial(_post_kernel, mix_groups=lse is not None,
                          final_norm=g_final is not None, ff_chunk=ff_chunk),
        grid=(t // tm,),
        in_specs=in_specs,
        out_specs=pl.BlockSpec((tm, d), lambda i: (i, 0)),
        out_shape=jax.ShapeDtypeStruct((t, d), F32),
        scratch_shapes=scratch,
        compiler_params=_cparams(("parallel",)),
        name="post",
    )(*args)


def _t5_bucket(rel):
    nb = NUM_BUCKETS // 2
    max_exact = nb // 2
    side = jnp.where(rel > 0, nb, 0)
    n = jnp.abs(rel)
    nf = jnp.maximum(n, 1).astype(F32)
    large = max_exact + (jnp.log(nf / max_exact) / math.log(REL_MAX_DISTANCE / max_exact)
                         * (nb - max_exact)).astype(jnp.int32)
    large = jnp.minimum(large, nb - 1)
    return side + jnp.where(n < max_exact, n, large)


def _table_lookup(bucket, tab_ref, col):
    out = jnp.zeros(bucket.shape, F32)
    for b in range(NUM_BUCKETS):
        out = jnp.where(bucket == b, tab_ref[b, col], out)
    return out


def _diff_bias_kernel(tab_ref, o_ref):
    hj = pl.program_id(0)
    r = lax.broadcasted_iota(jnp.int32, (LANES, LANES), 0)
    c = lax.broadcasted_iota(jnp.int32, (LANES, LANES), 1)
    for e in range(B_EBLOCKS):
        rel = LANES * (e - B_ECLIP) + r - c
        o_ref[e] = _table_lookup(_t5_bucket(rel), tab_ref, hj) * LOG2E


def _diff_bias_blocks(rel_bias):
    n = rel_bias.shape[1]
    out = pl.pallas_call(
        _diff_bias_kernel,
        grid=(n,),
        in_specs=[pl.BlockSpec(memory_space=pltpu.SMEM)],
        out_specs=pl.BlockSpec((None, B_EBLOCKS, LANES, LANES), lambda i: (i, 0, 0, 0)),
        out_shape=jax.ShapeDtypeStruct((n, B_EBLOCKS, LANES, LANES), F32),
        compiler_params=_cparams(("parallel",)),
        name="diff_bias",
    )(rel_bias.astype(F32))
    return out.reshape(n // 2, 2, B_EBLOCKS, LANES, LANES)


A_WINDOW_OFFSETS = (-A_SIDE, 0, -LANES)


def _window_bias_kernel(tab_ref, o_ref):
    hp = pl.program_id(0)
    dil = jnp.where(hp < A_PAD_HEADS, A_GROUPS[0][1],
                    jnp.where(hp < 2 * A_PAD_HEADS, A_GROUPS[1][1], A_GROUPS[2][1]))
    r = lax.broadcasted_iota(jnp.int32, (LANES, A_WIN), 0)
    c = lax.broadcasted_iota(jnp.int32, (LANES, A_WIN), 1)
    for v, off in enumerate(A_WINDOW_OFFSETS):
        rel = c + off - r
        bias = _table_lookup(_t5_bucket(rel * dil), tab_ref, hp)
        o_ref[v] = jnp.where(jnp.abs(rel) <= A_SIDE, bias, NEG)


def _window_bias_tiles(rel_bias):
    cols = _pad_group_cols(rel_bias.astype(F32), 1, unit=1)
    table = jnp.concatenate(cols, axis=1)
    n = table.shape[1]
    out = pl.pallas_call(
        _window_bias_kernel,
        grid=(n,),
        in_specs=[pl.BlockSpec(memory_space=pltpu.SMEM)],
        out_specs=pl.BlockSpec((None, 3, LANES, A_WIN), lambda i: (i, 0, 0, 0)),
        out_shape=jax.ShapeDtypeStruct((n, 3, LANES, A_WIN), F32),
        compiler_params=_cparams(("parallel",)),
        name="window_bias",
    )(table)
    return out.reshape(len(A_GROUPS), A_PAD_HEADS // 2, 2, 3, LANES, A_WIN)


def _rope_tables(seq):
    n_rows = seq // GRID_W
    row = jnp.repeat(jnp.arange(n_rows, dtype=jnp.int32), GRID_W).astype(F32)
    col = jnp.tile(jnp.arange(GRID_W, dtype=jnp.int32), n_rows).astype(F32)
    half = HEAD_DIM // 4
    inv = ROPE_THETA ** (-jnp.arange(half, dtype=F32) / half)
    ang = jnp.concatenate([row[:, None] * inv, col[:, None] * inv], axis=-1)
    lane = jnp.arange(LANES, dtype=jnp.int32) % HEAD_DIM
    idx = (lane // 32) * half + lane % half
    sign = jnp.where((lane % 32) < half, -1.0, 1.0).astype(F32)
    return jnp.cos(ang)[:, idx], jnp.sin(ang)[:, idx] * sign


def _pad_group_cols(w, axis, unit=HEAD_DIM):
    parts = []
    h0 = 0
    for (_, _, nh) in A_GROUPS:
        sl = [slice(None)] * w.ndim
        sl[axis] = slice(h0 * unit, (h0 + nh) * unit)
        part = w[tuple(sl)]
        if nh < A_PAD_HEADS:
            pad = [(0, 0)] * w.ndim
            pad[axis] = (0, (A_PAD_HEADS - nh) * unit)
            part = jnp.pad(part, pad)
        parts.append(part)
        h0 += nh
    return parts


def _dilated_layer(h, g_mix, w_qkv, w_o, bias_tiles, batch, seq):
    d_attn = w_qkv.shape[1] // 3
    wq, wk, wv = (w_qkv[:, i * d_attn:(i + 1) * d_attn] for i in range(3))
    qs, ks, vs = _pad_group_cols(wq, 1), _pad_group_cols(wk, 1), _pad_group_cols(wv, 1)
    w = jnp.concatenate([jnp.concatenate([qs[g], ks[g], vs[g]], axis=1) for g in range(3)],
                        axis=1).astype(BF16)
    slab_cols = 3 * A_COLS
    scale = jnp.tile(jnp.concatenate([jnp.full((A_COLS,), Q_SCALE, F32),
                                      jnp.ones((2 * A_COLS,), F32)]), 3)[None, :]
    slabs = _norm_proj_groups(h, g_mix, w, scale, batch=batch, seq=seq)
    outs, lses = [], []
    for g in range(len(A_GROUPS)):
        o, lse = _window_attn(slabs[g], bias_tiles[g])
        outs.append(o)
        lses.append(lse)
    w_o_pad = jnp.concatenate(_pad_group_cols(w_o, 0), axis=0).astype(BF16)
    return outs, lses, w_o_pad


def kernel(x, rel_bias, norm_mix_g, norm_mlp_g, norm_final_g, a_w_qkv, a_w_o, b_w_qkv,
           b_lambda, b_subln_g, b_w_o, c_w_qkv, c_q_norm_g, c_k_norm_g, c_w_o, mlp_w_in,
           mlp_w_out):
    batch, seq, d = x.shape
    t = batch * seq
    h = x.reshape(t, d)
    cos, sin = _rope_tables(seq)
    window_bias = _window_bias_tiles(rel_bias)
    for i in range(N_LAYERS):
        kind, j = i % 3, i // 3
        g_mix = norm_mix_g[i][None, :]
        lse = None
        if kind == 0:
            o, lse, w_o = _dilated_layer(h, g_mix, a_w_qkv[j], a_w_o[j], window_bias, batch, seq)
        elif kind == 1:
            scale = jnp.concatenate([jnp.full((d,), Q_SCALE_LOG2, F32),
                                     jnp.ones((2 * d,), F32)])[None, :]
            qkv = _norm_proj(h, g_mix, b_w_qkv[j].astype(BF16), scale, chunk=1024)
            n_pairs = d // LANES
            o = _pair_attn(qkv.reshape(batch, seq, 3 * d), n_pairs=n_pairs,
                           k_block=lambda p: n_pairs + p, v_block=lambda p: 2 * n_pairs + p,
                           mode="diff", gb=_diff_bias_blocks(rel_bias), lam=b_lambda[j],
                           subln=b_subln_g[j][None, :], lambda_init=_lambda_init(i))
            o = o.reshape(t, d)
            w_o = b_w_o[j].astype(BF16)
        else:
            n_q = d
            n_kv = C_KV_HEADS * HEAD_DIM
            qkv = _norm_proj_rope(h, g_mix, c_w_qkv[j].astype(BF16),
                                  jnp.tile(c_q_norm_g[j], 2)[None, :],
                                  jnp.tile(c_k_norm_g[j], 2)[None, :], cos, sin,
                                  n_q=n_q, n_k=n_kv, seq=seq, chunk=512)
            n_pairs = n_q // LANES
            kv_blocks = n_kv // LANES
            o = _pair_attn(qkv.reshape(batch, seq, n_q + 2 * n_kv), n_pairs=n_pairs,
                           k_block=lambda p: n_pairs + p // 4,
                           v_block=lambda p: n_pairs + kv_blocks + p // 4,
                           mode="gqa", ts=512)
            o = o.reshape(t, d)
            w_o = c_w_o[j].astype(BF16)
        g_final = norm_final_g[None, :] if i == N_LAYERS - 1 else None
        h = _post(h, o, w_o, norm_mlp_g[i][None, :], mlp_w_in[i].astype(BF16),
                  mlp_w_out[i].astype(BF16), lse=lse, g_final=g_final, seq=seq)
    return h.reshape(batch, seq, d)
```

```python
import functools
import math

import jax
import jax.numpy as jnp
from jax import lax
from jax.experimental import pallas as pl
from jax.experimental.pallas import tpu as pltpu

F32 = jnp.float32
BF16 = jnp.bfloat16

D_MODEL = 1024
HEAD_DIM = 64
LANES = 128
EPS = 1e-6
NEG = -1e30
N_LAYERS = 4
D_FF = 4 * D_MODEL
A_GROUPS = ((128, 1, 6), (512, 4, 5), (2048, 16, 5))
A_PAD_HEADS = 6
A_COLS = A_PAD_HEADS * HEAD_DIM
A_WIN = 256
A_SIDE = 64
A_UNROLL = 8
NUM_BUCKETS = 32
REL_MAX_DISTANCE = 1024
B_EBLOCKS = 13
B_ECLIP = 6
C_KV_HEADS = 4
GRID_W = 64
ROPE_THETA = 10000.0

LOG2E = math.log2(math.e)
Q_SCALE = HEAD_DIM ** -0.5
Q_SCALE_LOG2 = Q_SCALE * LOG2E

ONES_ROWS = 16

VMEM_LIMIT = 56 * 1024 * 1024


def _cparams(sem):
    return pltpu.CompilerParams(dimension_semantics=sem, vmem_limit_bytes=VMEM_LIMIT)


def _lambda_init(layer_idx):
    return 0.8 - 0.6 * math.exp(-0.3 * layer_idx)


def _rms(x, g):
    ms = jnp.mean(x * x, axis=-1, keepdims=True)
    return x * lax.rsqrt(ms + EPS) * g


def _norm_proj_kernel(x_ref, g_ref, w_ref, cs_ref, o_ref, *, chunk):
    xn = _rms(x_ref[...], g_ref[...]).astype(BF16)
    n = o_ref.shape[-1]
    for c in range(0, n, chunk):
        y = jnp.dot(xn, w_ref[:, c:c + chunk], preferred_element_type=F32)
        o_ref[:, c:c + chunk] = (y * cs_ref[:, c:c + chunk]).astype(o_ref.dtype)


def _norm_proj(x, g, w, col_scale, *, tm=512, chunk=None):
    t, d = x.shape
    n = w.shape[1]
    chunk = chunk or n
    return pl.pallas_call(
        functools.partial(_norm_proj_kernel, chunk=chunk),
        grid=(t // tm,),
        in_specs=[
            pl.BlockSpec((tm, d), lambda i: (i, 0)),
            pl.BlockSpec((1, d), lambda i: (0, 0)),
            pl.BlockSpec((d, n), lambda i: (0, 0)),
            pl.BlockSpec((1, n), lambda i: (0, 0)),
        ],
        out_specs=pl.BlockSpec((tm, n), lambda i: (i, 0)),
        out_shape=jax.ShapeDtypeStruct((t, n), BF16),
        compiler_params=_cparams(("parallel",)),
        name="norm_proj",
    )(x, g, w, col_scale)


def _norm_proj_groups_kernel(x_ref, g_ref, w_ref, cs_ref, o0_ref, o1_ref, o2_ref, y_scr):
    xn = _rms(x_ref[...], g_ref[...]).astype(BF16)
    tm = x_ref.shape[0]
    n = 3 * A_COLS
    for gi, o_ref in enumerate((o0_ref, o1_ref, o2_ref)):
        dil = A_GROUPS[gi][1]
        cols = slice(gi * n, (gi + 1) * n)
        y = jnp.dot(xn, w_ref[:, cols], preferred_element_type=F32) * cs_ref[:, cols]
        if dil == 1:
            o_ref[0] = y.astype(o_ref.dtype)
        else:
            for cb in range(n // LANES):
                y_scr[cb] = y[:, cb * LANES:(cb + 1) * LANES]
            for r in range(dil):
                for cb in range(n // LANES):
                    o_ref[r, :, cb * LANES:(cb + 1) * LANES] = (
                        y_scr[cb, pl.ds(r, tm // dil, stride=dil), :].astype(o_ref.dtype))


def _norm_proj_groups(x, g, w, col_scale, *, batch, seq, tm=512):
    t, d = x.shape
    n = 3 * A_COLS
    spb = seq // tm
    const = lambda i: (0, 0)
    out_specs, out_shape = [], []
    for (_, dil, _) in A_GROUPS:
        out_specs.append(pl.BlockSpec((None, dil, tm // dil, n),
                                      lambda i: (i // spb, 0, i % spb, 0)))
        out_shape.append(jax.ShapeDtypeStruct((batch, dil, seq // dil, n), BF16))
    return pl.pallas_call(
        _norm_proj_groups_kernel,
        grid=(t // tm,),
        in_specs=[
            pl.BlockSpec((tm, d), lambda i: (i, 0)),
            pl.BlockSpec((1, d), const),
            pl.BlockSpec((d, 3 * n), const),
            pl.BlockSpec((1, 3 * n), const),
        ],
        out_specs=out_specs,
        out_shape=out_shape,
        scratch_shapes=[pltpu.VMEM((n // LANES, tm, LANES), F32)],
        compiler_params=_cparams(("parallel",)),
        name="norm_proj_groups",
    )(x, g, w, col_scale)


def _head_norm_rope(y, gain, cos, sin_signed, lane):
    lo = lane < HEAD_DIM
    y2 = y * y
    s0 = jnp.sum(jnp.where(lo, y2, 0.0), axis=-1, keepdims=True)
    s1 = jnp.sum(jnp.where(lo, 0.0, y2), axis=-1, keepdims=True)
    inv = jnp.where(lo, lax.rsqrt(s0 / HEAD_DIM + EPS), lax.rsqrt(s1 / HEAD_DIM + EPS))
    yn = y * inv * gain
    first = (lane & 31) < 16
    partner = jnp.where(first, pltpu.roll(yn, LANES - 16, 1), pltpu.roll(yn, 16, 1))
    return yn * cos + partner * sin_signed


def _norm_proj_rope_kernel(x_ref, g_ref, w_ref, qg_ref, kg_ref, cos_ref, sin_ref, o_ref,
                           *, n_q, n_k, chunk):
    xn = _rms(x_ref[...], g_ref[...]).astype(BF16)
    n = o_ref.shape[-1]
    tm = x_ref.shape[0]
    lane = lax.broadcasted_iota(jnp.int32, (tm, LANES), 1)
    cos = cos_ref[...]
    sin = sin_ref[...]
    for c in range(0, n, chunk):
        y = jnp.dot(xn, w_ref[:, c:c + chunk], preferred_element_type=F32)
        for b in range(0, chunk, LANES):
            col = c + b
            blk = y[:, b:b + LANES]
            if col < n_q:
                blk = _head_norm_rope(blk, qg_ref[...], cos, sin, lane) * Q_SCALE_LOG2
            elif col < n_q + n_k:
                blk = _head_norm_rope(blk, kg_ref[...], cos, sin, lane)
            o_ref[:, col:col + LANES] = blk.astype(o_ref.dtype)


def _norm_proj_rope(x, g, w, q_gain, k_gain, cos, sin, *, n_q, n_k, seq, tm=512, chunk=512):
    t, d = x.shape
    n = w.shape[1]
    sblk = seq // tm
    return pl.pallas_call(
        functools.partial(_norm_proj_rope_kernel, n_q=n_q, n_k=n_k, chunk=chunk),
        grid=(t // tm,),
        in_specs=[
            pl.BlockSpec((tm, d), lambda i: (i, 0)),
            pl.BlockSpec((1, d), lambda i: (0, 0)),
            pl.BlockSpec((d, n), lambda i: (0, 0)),
            pl.BlockSpec((1, LANES), lambda i: (0, 0)),
            pl.BlockSpec((1, LANES), lambda i: (0, 0)),
            pl.BlockSpec((tm, LANES), lambda i: (i % sblk, 0)),
            pl.BlockSpec((tm, LANES), lambda i: (i % sblk, 0)),
        ],
        out_specs=pl.BlockSpec((tm, n), lambda i: (i, 0)),
        out_shape=jax.ShapeDtypeStruct((t, n), BF16),
        compiler_params=_cparams(("parallel",)),
        name="norm_proj_rope",
    )(x, g, w, q_gain, k_gain, cos, sin)


def _pair_attn_kernel(*refs, tq, ts, tk, n_kt, mode, lambda_init):
    vt_ref = refs[-1]
    if mode == "diff":
        q_ref, k_ref, v_ref, gb_ref, lam_ref, sg_ref, o_ref = refs[:-1]
    else:
        q_ref, k_ref, v_ref, o_ref = refs[:-1]
    qi = pl.program_id(2)
    seq = k_ref.shape[0]

    n_v = HEAD_DIM if mode == "gqa" else LANES
    kv_half = (pl.program_id(1) >> 1) & 1

    @pl.when(qi == 0)
    def _():
        for c in range(0, seq, tk):
            vt = v_ref[c:c + tk, :].astype(F32).T
            if mode == "gqa":
                vt = jnp.where(kv_half == 1, vt[HEAD_DIM:], vt[:HEAD_DIM])
            vt_ref[:n_v, c:c + tk] = vt.astype(BF16)
        vt_ref[n_v:, :] = jnp.ones((ONES_ROWS, seq), BF16)

    n_st = tq // ts
    lane = lax.broadcasted_iota(jnp.int32, (ts, LANES), 1)
    lo = lane < HEAD_DIM
    qs_st = []
    for st in range(n_st):
        q = q_ref[st * ts:(st + 1) * ts, :]
        zero = jnp.zeros_like(q)
        if mode == "gqa":
            q_sw = pltpu.roll(q.astype(F32), HEAD_DIM, 1).astype(q.dtype)
            in_half = lo == (kv_half == 0)
            h0 = jnp.where(kv_half == 0, q, q_sw)
            h1 = jnp.where(kv_half == 0, q_sw, q)
            qs = jnp.concatenate([jnp.where(in_half, h0, zero), jnp.where(in_half, h1, zero)],
                                 axis=0)
        else:
            qs = jnp.concatenate([jnp.where(lo, q, zero), jnp.where(lo, zero, q)], axis=0)
        qs_st.append(qs)
    rb_n = ts // LANES
    cb_n = tk // LANES

    def logits(kt, st):
        k = k_ref[kt * tk:(kt + 1) * tk, :]
        s = lax.dot_general(k, qs_st[st], (((1,), (1,)), ((), ())),
                            preferred_element_type=F32)
        if mode == "diff":
            base = kt * cb_n - (qi * n_st + st) * rb_n
            rows = []
            for cb in range(cb_n):
                blocks = []
                for j in range(2):
                    for rb in range(rb_n):
                        e = jnp.clip(base + (cb - rb), -B_ECLIP, B_ECLIP) + B_ECLIP
                        blocks.append(gb_ref[j, e])
                rows.append(jnp.concatenate(blocks, axis=1))
            s = s + jnp.concatenate(rows, axis=0)
        return s

    m = [jnp.full((1, 2 * ts), NEG, F32)] * n_st
    acc = [jnp.zeros((n_v + ONES_ROWS, 2 * ts), F32)] * n_st
    s_tile = {}

    def qk(kt, st):
        if kt < n_kt:
            s_tile[kt, st] = logits(kt, st)

    def softmax_pv(kt, st):
        if kt >= n_kt:
            return
        s = s_tile.pop((kt, st))
        m_new = jnp.maximum(m[st], jnp.max(s, axis=0, keepdims=True))
        alpha = jnp.exp2(m[st] - m_new)
        p = jnp.exp2(s - m_new).astype(BF16)
        vt = vt_ref[:, kt * tk:(kt + 1) * tk]
        acc[st] = alpha * acc[st] + jnp.dot(vt, p, preferred_element_type=F32)
        m[st] = m_new

    for st in range(n_st):
        qk(0, st)
    softmax_pv(0, 0)
    for kt in range(n_kt):
        for st in range(n_st):
            qk(kt + 1, st)
            if st + 1 < n_st:
                softmax_pv(kt, st + 1)
            else:
                softmax_pv(kt + 1, 0)
    if mode == "diff":
        lam = lam_ref[...]
        lam_full = (jnp.exp(jnp.sum(lam[0:1] * lam[1:2], axis=-1, keepdims=True))
                    - jnp.exp(jnp.sum(lam[2:3] * lam[3:4], axis=-1, keepdims=True))
                    + lambda_init)
    for st in range(n_st):
        o = acc[st][:n_v] / acc[st][n_v:n_v + 1]
        if mode == "diff":
            a = o[:, :ts].T - lam_full * o[:, ts:].T
            y = _rms(a, sg_ref[...]) * (1.0 - lambda_init)
        else:
            y = jnp.concatenate([o[:, :ts], o[:, ts:]], axis=0).T
        o_ref[st * ts:(st + 1) * ts, :] = y.astype(o_ref.dtype)


def _pair_attn(qkv, *, n_pairs, k_block, v_block, mode, gb=None, lam=None, subln=None,
               lambda_init=0.0, tq=1024, ts=256, tk=512):
    b, s, _ = qkv.shape
    in_specs = [
        pl.BlockSpec((None, tq, LANES), lambda bi, p, qi: (bi, qi, p)),
        pl.BlockSpec((None, s, LANES), lambda bi, p, qi: (bi, 0, k_block(p))),
        pl.BlockSpec((None, s, LANES), lambda bi, p, qi: (bi, 0, v_block(p))),
    ]
    args = [qkv, qkv, qkv]
    if mode == "diff":
        in_specs += [
            pl.BlockSpec((None, 2, B_EBLOCKS, LANES, LANES), lambda bi, p, qi: (p, 0, 0, 0, 0)),
            pl.BlockSpec((4, HEAD_DIM), lambda bi, p, qi: (0, 0)),
            pl.BlockSpec((1, LANES), lambda bi, p, qi: (0, 0)),
        ]
        args += [gb, lam, subln]
    return pl.pallas_call(
        functools.partial(_pair_attn_kernel, tq=tq, ts=ts, tk=tk, n_kt=s // tk, mode=mode,
                          lambda_init=lambda_init),
        grid=(b, n_pairs, s // tq),
        in_specs=in_specs,
        out_specs=pl.BlockSpec((None, tq, LANES), lambda bi, p, qi: (bi, qi, p)),
        out_shape=jax.ShapeDtypeStruct((b, s, n_pairs * LANES), BF16),
        scratch_shapes=[pltpu.VMEM(((HEAD_DIM if mode == "gqa" else LANES) + ONES_ROWS, s),
                                   BF16)],
        compiler_params=_cparams(("parallel", "parallel", "arbitrary")),
        name="pair_attn_" + mode,
    )(*args)


def _window_attn_kernel(slab_ref, ga_ref, o_ref, lse_ref, *, sub_len):
    lane = lax.broadcasted_iota(jnp.int32, (LANES, LANES), 1)
    lo = lane < HEAD_DIM
    n_p = A_COLS // LANES

    def sub_block(res, sb):
        static = isinstance(sb, int)
        q0 = sb * LANES
        if static:
            start = min(max(q0 - A_SIDE, 0), sub_len - A_WIN)
            variant = 1 if q0 == 0 else (2 if q0 == sub_len - LANES else 0)
            qrows, wrows = slice(q0, q0 + LANES), slice(start, start + A_WIN)
        else:
            q0 = pl.multiple_of(q0, LANES)
            start = pl.multiple_of(jnp.clip(q0 - A_SIDE, 0, sub_len - A_WIN), A_SIDE)
            variant = jnp.where(q0 == 0, 1, jnp.where(q0 == sub_len - LANES, 2, 0))
            qrows, wrows = pl.ds(q0, LANES), pl.ds(start, A_WIN)
        lse_all = jnp.full((LANES, LANES), NEG, F32)
        for p in range(n_p):
            cols = slice(p * LANES, (p + 1) * LANES)
            q = slab_ref[res, qrows, cols]
            kw = slab_ref[res, wrows, A_COLS + p * LANES:A_COLS + (p + 1) * LANES]
            vw = slab_ref[res, wrows, 2 * A_COLS + p * LANES:2 * A_COLS + (p + 1) * LANES]
            zero = jnp.zeros_like(q)
            qs = jnp.concatenate([jnp.where(lo, q, zero), jnp.where(lo, zero, q)], axis=0)
            s = lax.dot_general(qs, kw, (((1,), (1,)), ((), ())), preferred_element_type=F32)
            s = s + jnp.concatenate([ga_ref[p, 0, variant], ga_ref[p, 1, variant]], axis=0)
            m = jnp.max(s, axis=-1, keepdims=True)
            e = jnp.exp(s - m)
            l = jnp.sum(e, axis=-1, keepdims=True)
            acc = jnp.dot(e.astype(BF16), vw, preferred_element_type=F32)
            o = acc / l
            lse = m + jnp.log(l)
            o_ref[res, qrows, cols] = jnp.where(lo, o[:LANES], o[LANES:]).astype(o_ref.dtype)
            lse_all = jnp.where(lane == 2 * p, lse[:LANES],
                                jnp.where(lane == 2 * p + 1, lse[LANES:], lse_all))
        lse_ref[res, qrows, :] = lse_all

    n_res = slab_ref.shape[0]
    n_sb = sub_len // LANES
    if n_res * n_sb <= A_UNROLL:
        for res in range(n_res):
            for sb in range(n_sb):
                sub_block(res, sb)
    else:
        def body(i, carry):
            for u in range(A_UNROLL):
                sub_block(0, A_UNROLL * i + u)
            return carry
        lax.fori_loop(0, n_sb // A_UNROLL, body, 0)


def _window_attn(slab, ga):
    b, dil, sub_len, n = slab.shape
    n_sb = sub_len // LANES
    n_res = max(1, min(dil, A_UNROLL // n_sb))
    assert n_res == 1 or n_res * n_sb <= A_UNROLL
    assert n_res > 1 or n_sb <= A_UNROLL or n_sb % A_UNROLL == 0
    out_spec = pl.BlockSpec((None, n_res, sub_len, A_COLS), lambda bi, r: (bi, r, 0, 0))
    lse_spec = pl.BlockSpec((None, n_res, sub_len, LANES), lambda bi, r: (bi, r, 0, 0))
    return pl.pallas_call(
        functools.partial(_window_attn_kernel, sub_len=sub_len),
        grid=(b, dil // n_res),
        in_specs=[
            pl.BlockSpec((None, n_res, sub_len, n), lambda bi, r: (bi, r, 0, 0)),
            pl.BlockSpec(ga.shape, lambda bi, r: (0,) * ga.ndim),
        ],
        out_specs=[out_spec, lse_spec],
        out_shape=[jax.ShapeDtypeStruct((b, dil, sub_len, A_COLS), BF16),
                   jax.ShapeDtypeStruct((b, dil, sub_len, LANES), F32)],
        compiler_params=_cparams(("parallel", "parallel")),
        name="window_attn",
    )(slab, ga)


def _post_kernel(*refs, mix_groups, final_norm, ff_chunk):
    refs = list(refs)
    h_ref = refs.pop(0)
    if mix_groups:
        og_refs = [refs.pop(0) for _ in A_GROUPS]
        lg_refs = [refs.pop(0) for _ in A_GROUPS]
    else:
        o_ref = refs.pop(0)
    wo_ref, g_ref, win_ref, wout_ref = refs[:4]
    refs = refs[4:]
    gf_ref = refs.pop(0) if final_norm else None
    out_ref = refs.pop(0)

    if mix_groups:
        o_scr, lse_scr = refs
        n_cb, tm, _ = o_scr.shape
        cb_per_group = A_COLS // LANES
        for gi, (_, dil, _) in enumerate(A_GROUPS):
            for r in range(dil):
                rows = pl.ds(r, tm // dil, stride=dil) if dil > 1 else slice(None)
                for cb in range(cb_per_group):
                    cols = slice(cb * LANES, (cb + 1) * LANES)
                    o_scr[gi * cb_per_group + cb, rows, :] = og_refs[gi][r, :, cols].astype(F32)
                lse_scr[gi, rows, :] = lg_refs[gi][r]
        lane = lax.broadcasted_iota(jnp.int32, (tm, LANES), 1)
        real = [lane < nh for (_, _, nh) in A_GROUPS]
        lse = [jnp.where(real[gi], lse_scr[gi], NEG) for gi in range(len(A_GROUPS))]
        mx = jnp.max(jnp.maximum(jnp.maximum(lse[0], lse[1]), lse[2]), axis=-1, keepdims=True)
        s_g = [jnp.sum(jnp.where(real[gi], jnp.exp(lse[gi] - mx), 0.0), axis=-1, keepdims=True)
               / A_GROUPS[gi][2] for gi in range(len(A_GROUPS))]
        tot = s_g[0] + s_g[1] + s_g[2]
        parts = []
        for gi in range(len(A_GROUPS)):
            scale = len(A_GROUPS) * (s_g[gi] / tot)
            parts += [(o_scr[gi * cb_per_group + cb] * scale).astype(BF16)
                      for cb in range(cb_per_group)]
        o = jnp.concatenate(parts, axis=1)
    else:
        o = o_ref[...]
    h1 = h_ref[...] + jnp.dot(o, wo_ref[...], preferred_element_type=F32)
    xn = _rms(h1, g_ref[...]).astype(BF16)
    acc = h1
    for c in range(0, D_FF, ff_chunk):
        u = jnp.dot(xn, win_ref[:, c:c + ff_chunk], preferred_element_type=F32)
        u = jnp.maximum(u, 0.0)
        u = (u * u).astype(BF16)
        acc = acc + jnp.dot(u, wout_ref[c:c + ff_chunk, :], preferred_element_type=F32)
    if final_norm:
        acc = _rms(acc, gf_ref[...])
    out_ref[...] = acc


def _post(h, o, w_o, g_mlp, w_in, w_out, *, lse=None, g_final=None, seq=None, tm=512,
          ff_chunk=1024):
    t, d = h.shape
    const = lambda i: (0, 0)
    single = dict(pipeline_mode=pl.Buffered(1))
    in_specs = [pl.BlockSpec((tm, d), lambda i: (i, 0))]
    args = [h]
    scratch = []
    if lse is not None:
        n_o = len(A_GROUPS) * A_COLS
        spb = seq // tm
        for arrs in (o, lse):
            for arr, (_, dil, _) in zip(arrs, A_GROUPS):
                in_specs.append(pl.BlockSpec((None, dil, tm // dil, arr.shape[-1]),
                                             lambda i: (i // spb, 0, i % spb, 0)))
                args.append(arr)
        scratch = [pltpu.VMEM((n_o // LANES, tm, LANES), F32),
                   pltpu.VMEM((len(A_GROUPS), tm, LANES), F32)]
    else:
        n_o = o.shape[1]
        in_specs.append(pl.BlockSpec((tm, n_o), lambda i: (i, 0)))
        args.append(o)
    in_specs += [pl.BlockSpec((n_o, d), const, **single),
                 pl.BlockSpec((1, d), const),
                 pl.BlockSpec((d, D_FF), const, **single),
                 pl.BlockSpec((D_FF, d), const, **single)]
    args += [w_o, g_mlp, w_in, w_out]
    if g_final is not None:
        in_specs.append(pl.BlockSpec((1, d), const))
        args.append(g_final)
    return pl.pallas_call(
        functools.partial(_post_kernel, mix_groups=lse is not None,
                          final_norm=g_final is not None, ff_chunk=ff_chunk),
        grid=(t // tm,),
        in_specs=in_specs,
        out_specs=pl.BlockSpec((tm, d), lambda i: (i, 0)),
        out_shape=jax.ShapeDtypeStruct((t, d), F32),
        scratch_shapes=scratch,
        compiler_params=_cparams(("parallel",)),
        name="post",
    )(*args)


def _t5_bucket(rel):
    nb = NUM_BUCKETS // 2
    max_exact = nb // 2
    side = jnp.where(rel > 0, nb, 0)
    n = jnp.abs(rel)
    nf = jnp.maximum(n, 1).astype(F32)
    large = max_exact + (jnp.log(nf / max_exact) / math.log(REL_MAX_DISTANCE / max_exact)
                         * (nb - max_exact)).astype(jnp.int32)
    large = jnp.minimum(large, nb - 1)
    return side + jnp.where(n < max_exact, n, large)


def _table_lookup(bucket, tab_ref, col):
    out = jnp.zeros(bucket.shape, F32)
    for b in range(NUM_BUCKETS):
        out = jnp.where(bucket == b, tab_ref[b, col], out)
    return out


def _diff_bias_kernel(tab_ref, o_ref):
    hj = pl.program_id(0)
    r = lax.broadcasted_iota(jnp.int32, (LANES, LANES), 0)
    c = lax.broadcasted_iota(jnp.int32, (LANES, LANES), 1)
    for e in range(B_EBLOCKS):
        rel = LANES * (e - B_ECLIP) + r - c
        o_ref[e] = _table_lookup(_t5_bucket(rel), tab_ref, hj) * LOG2E


def _diff_bias_blocks(rel_bias):
    n = rel_bias.shape[1]
    out = pl.pallas_call(
        _diff_bias_kernel,
        grid=(n,),
        in_specs=[pl.BlockSpec(memory_space=pltpu.SMEM)],
        out_specs=pl.BlockSpec((None, B_EBLOCKS, LANES, LANES), lambda i: (i, 0, 0, 0)),
        out_shape=jax.ShapeDtypeStruct((n, B_EBLOCKS, LANES, LANES), F32),
        compiler_params=_cparams(("parallel",)),
        name="diff_bias",
    )(rel_bias.astype(F32))
    return out.reshape(n // 2, 2, B_EBLOCKS, LANES, LANES)


A_WINDOW_OFFSETS = (-A_SIDE, 0, -LANES)


def _window_bias_kernel(tab_ref, o_ref):
    hp = pl.program_id(0)
    dil = jnp.where(hp < A_PAD_HEADS, A_GROUPS[0][1],
                    jnp.where(hp < 2 * A_PAD_HEADS, A_GROUPS[1][1], A_GROUPS[2][1]))
    r = lax.broadcasted_iota(jnp.int32, (LANES, A_WIN), 0)
    c = lax.broadcasted_iota(jnp.int32, (LANES, A_WIN), 1)
    for v, off in enumerate(A_WINDOW_OFFSETS):
        rel = c + off - r
        bias = _table_lookup(_t5_bucket(rel * dil), tab_ref, hp)
        o_ref[v] = jnp.where(jnp.abs(rel) <= A_SIDE, bias, NEG)


def _window_bias_tiles(rel_bias):
    cols = _pad_group_cols(rel_bias.astype(F32), 1, unit=1)
    table = jnp.concatenate(cols, axis=1)
    n = table.shape[1]
    out = pl.pallas_call(
        _window_bias_kernel,
        grid=(n,),
        in_specs=[pl.BlockSpec(memory_space=pltpu.SMEM)],
        out_specs=pl.BlockSpec((None, 3, LANES, A_WIN), lambda i: (i, 0, 0, 0)),
        out_shape=jax.ShapeDtypeStruct((n, 3, LANES, A_WIN), F32),
        compiler_params=_cparams(("parallel",)),
        name="window_bias",
    )(table)
    return out.reshape(len(A_GROUPS), A_PAD_HEADS // 2, 2, 3, LANES, A_WIN)


def _rope_tables(seq):
    n_rows = seq // GRID_W
    row = jnp.repeat(jnp.arange(n_rows, dtype=jnp.int32), GRID_W).astype(F32)
    col = jnp.tile(jnp.arange(GRID_W, dtype=jnp.int32), n_rows).astype(F32)
    half = HEAD_DIM // 4
    inv = ROPE_THETA ** (-jnp.arange(half, dtype=F32) / half)
    ang = jnp.concatenate([row[:, None] * inv, col[:, None] * inv], axis=-1)
    lane = jnp.arange(LANES, dtype=jnp.int32) % HEAD_DIM
    idx = (lane // 32) * half + lane % half
    sign = jnp.where((lane % 32) < half, -1.0, 1.0).astype(F32)
    return jnp.cos(ang)[:, idx], jnp.sin(ang)[:, idx] * sign


def _pad_group_cols(w, axis, unit=HEAD_DIM):
    parts = []
    h0 = 0
    for (_, _, nh) in A_GROUPS:
        sl = [slice(None)] * w.ndim
        sl[axis] = slice(h0 * unit, (h0 + nh) * unit)
        part = w[tuple(sl)]
        if nh < A_PAD_HEADS:
            pad = [(0, 0)] * w.ndim
            pad[axis] = (0, (A_PAD_HEADS - nh) * unit)
            part = jnp.pad(part, pad)
        parts.append(part)
        h0 += nh
    return parts


def _dilated_layer(h, g_mix, w_qkv, w_o, bias_tiles, batch, seq):
    d_attn = w_qkv.shape[1] // 3
    wq, wk, wv = (w_qkv[:, i * d_attn:(i + 1) * d_attn] for i in range(3))
    qs, ks, vs = _pad_group_cols(wq, 1), _pad_group_cols(wk, 1), _pad_group_cols(wv, 1)
    w = jnp.concatenate([jnp.concatenate([qs[g], ks[g], vs[g]], axis=1) for g in range(3)],
                        axis=1).astype(BF16)
    slab_cols = 3 * A_COLS
    scale = jnp.tile(jnp.concatenate([jnp.full((A_COLS,), Q_SCALE, F32),
                                      jnp.ones((2 * A_COLS,), F32)]), 3)[None, :]
    slabs = _norm_proj_groups(h, g_mix, w, scale, batch=batch, seq=seq)
    outs, lses = [], []
    for g in range(len(A_GROUPS)):
        o, lse = _window_attn(slabs[g], bias_tiles[g])
        outs.append(o)
        lses.append(lse)
    w_o_pad = jnp.concatenate(_pad_group_cols(w_o, 0), axis=0).astype(BF16)
    return outs, lses, w_o_pad


def kernel(x, rel_bias, norm_mix_g, norm_mlp_g, norm_final_g, a_w_qkv, a_w_o, b_w_qkv,
           b_lambda, b_subln_g, b_w_o, c_w_qkv, c_q_norm_g, c_k_norm_g, c_w_o, mlp_w_in,
           mlp_w_out):
    batch, seq, d = x.shape
    t = batch * seq
    h = x.reshape(t, d)
    cos, sin = _rope_tables(seq)
    window_bias = _window_bias_tiles(rel_bias)
    for i in range(N_LAYERS):
        kind, j = i % 3, i // 3
        g_mix = norm_mix_g[i][None, :]
        lse = None
        if kind == 0:
            o, lse, w_o = _dilated_layer(h, g_mix, a_w_qkv[j], a_w_o[j], window_bias, batch, seq)
        elif kind == 1:
            scale = jnp.concatenate([jnp.full((d,), Q_SCALE_LOG2, F32),
                                     jnp.ones((2 * d,), F32)])[None, :]
            qkv = _norm_proj(h, g_mix, b_w_qkv[j].astype(BF16), scale, chunk=1024)
            n_pairs = d // LANES
            o = _pair_attn(qkv.reshape(batch, seq, 3 * d), n_pairs=n_pairs,
                           k_block=lambda p: n_pairs + p, v_block=lambda p: 2 * n_pairs + p,
                           mode="diff", gb=_diff_bias_blocks(rel_bias), lam=b_lambda[j],
                           subln=b_subln_g[j][None, :], lambda_init=_lambda_init(i))
            o = o.reshape(t, d)
            w_o = b_w_o[j].astype(BF16)
        else:
            n_q = d
            n_kv = C_KV_HEADS * HEAD_DIM
            qkv = _norm_proj_rope(h, g_mix, c_w_qkv[j].astype(BF16),
                                  jnp.tile(c_q_norm_g[j], 2)[None, :],
                                  jnp.tile(c_k_norm_g[j], 2)[None, :], cos, sin,
                                  n_q=n_q, n_k=n_kv, seq=seq, chunk=512)
            n_pairs = n_q // LANES
            kv_blocks = n_kv // LANES
            o = _pair_attn(qkv.reshape(batch, seq, n_q + 2 * n_kv), n_pairs=n_pairs,
                           k_block=lambda p: n_pairs + p // 4,
                           v_block=lambda p: n_pairs + kv_blocks + p // 4,
                           mode="gqa")
            o = o.reshape(t, d)
            w_o = c_w_o[j].astype(BF16)
        g_final = norm_final_g[None, :] if i == N_LAYERS - 1 else None
        h = _post(h, o, w_o, norm_mlp_g[i][None, :], mlp_w_in[i].astype(BF16),
                  mlp_w_out[i].astype(BF16), lse=lse, g_final=g_final, seq=seq)
    return h.reshape(batch, seq, d)
```

```python
import functools
import math

import jax
import jax.numpy as jnp
from jax import lax
from jax.experimental import pallas as pl
from jax.experimental.pallas import tpu as pltpu

F32 = jnp.float32
BF16 = jnp.bfloat16

D_MODEL = 1024
HEAD_DIM = 64
LANES = 128
EPS = 1e-6
NEG = -1e30
N_LAYERS = 4
D_FF = 4 * D_MODEL
A_GROUPS = ((128, 1, 6), (512, 4, 5), (2048, 16, 5))
A_PAD_HEADS = 6
A_COLS = A_PAD_HEADS * HEAD_DIM
A_WIN = 256
A_SIDE = 64
A_UNROLL = 8
NUM_BUCKETS = 32
REL_MAX_DISTANCE = 1024
B_EBLOCKS = 13
B_ECLIP = 6
C_KV_HEADS = 4
GRID_W = 64
ROPE_THETA = 10000.0

LOG2E = math.log2(math.e)
Q_SCALE = HEAD_DIM ** -0.5
Q_SCALE_LOG2 = Q_SCALE * LOG2E

ONES_ROWS = 16

VMEM_LIMIT = 56 * 1024 * 1024


def _cparams(sem):
    return pltpu.CompilerParams(dimension_semantics=sem, vmem_limit_bytes=VMEM_LIMIT)


def _lambda_init(layer_idx):
    return 0.8 - 0.6 * math.exp(-0.3 * layer_idx)


def _rms(x, g):
    ms = jnp.mean(x * x, axis=-1, keepdims=True)
    return x * lax.rsqrt(ms + EPS) * g


def _norm_proj_kernel(x_ref, g_ref, w_ref, cs_ref, o_ref, *, chunk):
    xn = _rms(x_ref[...], g_ref[...]).astype(BF16)
    n = o_ref.shape[-1]
    for c in range(0, n, chunk):
        y = jnp.dot(xn, w_ref[:, c:c + chunk], preferred_element_type=F32)
        o_ref[:, c:c + chunk] = (y * cs_ref[:, c:c + chunk]).astype(o_ref.dtype)


def _norm_proj(x, g, w, col_scale, *, tm=512, chunk=None):
    t, d = x.shape
    n = w.shape[1]
    chunk = chunk or n
    return pl.pallas_call(
        functools.partial(_norm_proj_kernel, chunk=chunk),
        grid=(t // tm,),
        in_specs=[
            pl.BlockSpec((tm, d), lambda i: (i, 0)),
            pl.BlockSpec((1, d), lambda i: (0, 0)),
            pl.BlockSpec((d, n), lambda i: (0, 0)),
            pl.BlockSpec((1, n), lambda i: (0, 0)),
        ],
        out_specs=pl.BlockSpec((tm, n), lambda i: (i, 0)),
        out_shape=jax.ShapeDtypeStruct((t, n), BF16),
        compiler_params=_cparams(("parallel",)),
        name="norm_proj",
    )(x, g, w, col_scale)


def _norm_proj_groups_kernel(x_ref, g_ref, w_ref, cs_ref, o0_ref, o1_ref, o2_ref, y_scr):
    xn = _rms(x_ref[...], g_ref[...]).astype(BF16)
    tm = x_ref.shape[0]
    n = 3 * A_COLS
    for gi, o_ref in enumerate((o0_ref, o1_ref, o2_ref)):
        dil = A_GROUPS[gi][1]
        cols = slice(gi * n, (gi + 1) * n)
        y = jnp.dot(xn, w_ref[:, cols], preferred_element_type=F32) * cs_ref[:, cols]
        if dil == 1:
            o_ref[0] = y.astype(o_ref.dtype)
        else:
            for cb in range(n // LANES):
                y_scr[cb] = y[:, cb * LANES:(cb + 1) * LANES]
            for r in range(dil):
                for cb in range(n // LANES):
                    o_ref[r, :, cb * LANES:(cb + 1) * LANES] = (
                        y_scr[cb, pl.ds(r, tm // dil, stride=dil), :].astype(o_ref.dtype))


def _norm_proj_groups(x, g, w, col_scale, *, batch, seq, tm=512):
    t, d = x.shape
    n = 3 * A_COLS
    spb = seq // tm
    const = lambda i: (0, 0)
    out_specs, out_shape = [], []
    for (_, dil, _) in A_GROUPS:
        out_specs.append(pl.BlockSpec((None, dil, tm // dil, n),
                                      lambda i: (i // spb, 0, i % spb, 0)))
        out_shape.append(jax.ShapeDtypeStruct((batch, dil, seq // dil, n), BF16))
    return pl.pallas_call(
        _norm_proj_groups_kernel,
        grid=(t // tm,),
        in_specs=[
            pl.BlockSpec((tm, d), lambda i: (i, 0)),
            pl.BlockSpec((1, d), const),
            pl.BlockSpec((d, 3 * n), const),
            pl.BlockSpec((1, 3 * n), const),
        ],
        out_specs=out_specs,
        out_shape=out_shape,
        scratch_shapes=[pltpu.VMEM((n // LANES, tm, LANES), F32)],
        compiler_params=_cparams(("parallel",)),
        name="norm_proj_groups",
    )(x, g, w, col_scale)


def _head_norm_rope(y, gain, cos, sin_signed, lane):
    lo = lane < HEAD_DIM
    y2 = y * y
    s0 = jnp.sum(jnp.where(lo, y2, 0.0), axis=-1, keepdims=True)
    s1 = jnp.sum(jnp.where(lo, 0.0, y2), axis=-1, keepdims=True)
    inv = jnp.where(lo, lax.rsqrt(s0 / HEAD_DIM + EPS), lax.rsqrt(s1 / HEAD_DIM + EPS))
    yn = y * inv * gain
    first = (lane & 31) < 16
    partner = jnp.where(first, pltpu.roll(yn, LANES - 16, 1), pltpu.roll(yn, 16, 1))
    return yn * cos + partner * sin_signed


def _norm_proj_rope_kernel(x_ref, g_ref, w_ref, qg_ref, kg_ref, cos_ref, sin_ref, o_ref,
                           *, n_q, n_k, chunk):
    xn = _rms(x_ref[...], g_ref[...]).astype(BF16)
    n = o_ref.shape[-1]
    tm = x_ref.shape[0]
    lane = lax.broadcasted_iota(jnp.int32, (tm, LANES), 1)
    cos = cos_ref[...]
    sin = sin_ref[...]
    for c in range(0, n, chunk):
        y = jnp.dot(xn, w_ref[:, c:c + chunk], preferred_element_type=F32)
        for b in range(0, chunk, LANES):
            col = c + b
            blk = y[:, b:b + LANES]
            if col < n_q:
                blk = _head_norm_rope(blk, qg_ref[...], cos, sin, lane) * Q_SCALE_LOG2
            elif col < n_q + n_k:
                blk = _head_norm_rope(blk, kg_ref[...], cos, sin, lane)
            o_ref[:, col:col + LANES] = blk.astype(o_ref.dtype)


def _norm_proj_rope(x, g, w, q_gain, k_gain, cos, sin, *, n_q, n_k, seq, tm=512, chunk=512):
    t, d = x.shape
    n = w.shape[1]
    sblk = seq // tm
    return pl.pallas_call(
        functools.partial(_norm_proj_rope_kernel, n_q=n_q, n_k=n_k, chunk=chunk),
        grid=(t // tm,),
        in_specs=[
            pl.BlockSpec((tm, d), lambda i: (i, 0)),
            pl.BlockSpec((1, d), lambda i: (0, 0)),
            pl.BlockSpec((d, n), lambda i: (0, 0)),
            pl.BlockSpec((1, LANES), lambda i: (0, 0)),
            pl.BlockSpec((1, LANES), lambda i: (0, 0)),
            pl.BlockSpec((tm, LANES), lambda i: (i % sblk, 0)),
            pl.BlockSpec((tm, LANES), lambda i: (i % sblk, 0)),
        ],
        out_specs=pl.BlockSpec((tm, n), lambda i: (i, 0)),
        out_shape=jax.ShapeDtypeStruct((t, n), BF16),
        compiler_params=_cparams(("parallel",)),
        name="norm_proj_rope",
    )(x, g, w, q_gain, k_gain, cos, sin)


def _pair_attn_kernel(*refs, tq, ts, tk, n_kt, mode, lambda_init):
    vt_ref = refs[-1]
    if mode == "diff":
        q_ref, k_ref, v_ref, gb_ref, lam_ref, sg_ref, o_ref = refs[:-1]
    else:
        q_ref, k_ref, v_ref, o_ref = refs[:-1]
    qi = pl.program_id(2)
    seq = k_ref.shape[0]

    n_v = LANES
    kv_half = (pl.program_id(1) >> 1) & 1

    @pl.when(qi == 0)
    def _():
        for c in range(0, seq, tk):
            vt_ref[:n_v, c:c + tk] = v_ref[c:c + tk, :].astype(F32).T.astype(BF16)
        vt_ref[n_v:, :] = jnp.ones((ONES_ROWS, seq), BF16)

    n_st = tq // ts
    lane = lax.broadcasted_iota(jnp.int32, (ts, LANES), 1)
    lo = lane < HEAD_DIM
    qs_st = []
    for st in range(n_st):
        q = q_ref[st * ts:(st + 1) * ts, :]
        zero = jnp.zeros_like(q)
        if mode == "gqa":
            q_sw = pltpu.roll(q.astype(F32), HEAD_DIM, 1).astype(q.dtype)
            in_half = lo == (kv_half == 0)
            h0 = jnp.where(kv_half == 0, q, q_sw)
            h1 = jnp.where(kv_half == 0, q_sw, q)
            qs = jnp.concatenate([jnp.where(in_half, h0, zero), jnp.where(in_half, h1, zero)],
                                 axis=0)
        else:
            qs = jnp.concatenate([jnp.where(lo, q, zero), jnp.where(lo, zero, q)], axis=0)
        qs_st.append(qs)
    rb_n = ts // LANES
    cb_n = tk // LANES

    def logits(kt, st):
        k = k_ref[kt * tk:(kt + 1) * tk, :]
        s = lax.dot_general(k, qs_st[st], (((1,), (1,)), ((), ())),
                            preferred_element_type=F32)
        if mode == "diff":
            base = kt * cb_n - (qi * n_st + st) * rb_n
            rows = []
            for cb in range(cb_n):
                blocks = []
                for j in range(2):
                    for rb in range(rb_n):
                        e = jnp.clip(base + (cb - rb), -B_ECLIP, B_ECLIP) + B_ECLIP
                        blocks.append(gb_ref[j, e])
                rows.append(jnp.concatenate(blocks, axis=1))
            s = s + jnp.concatenate(rows, axis=0)
        return s

    m = [jnp.full((1, 2 * ts), NEG, F32)] * n_st
    acc = [jnp.zeros((n_v + ONES_ROWS, 2 * ts), F32)] * n_st
    s_tile = {}

    def qk(kt, st):
        if kt < n_kt:
            s_tile[kt, st] = logits(kt, st)

    def softmax_pv(kt, st):
        if kt >= n_kt:
            return
        s = s_tile.pop((kt, st))
        m_new = jnp.maximum(m[st], jnp.max(s, axis=0, keepdims=True))
        alpha = jnp.exp2(m[st] - m_new)
        p = jnp.exp2(s - m_new).astype(BF16)
        vt = vt_ref[:, kt * tk:(kt + 1) * tk]
        acc[st] = alpha * acc[st] + jnp.dot(vt, p, preferred_element_type=F32)
        m[st] = m_new

    for st in range(n_st):
        qk(0, st)
    softmax_pv(0, 0)
    for kt in range(n_kt):
        for st in range(n_st):
            qk(kt + 1, st)
            if st + 1 < n_st:
                softmax_pv(kt, st + 1)
            else:
                softmax_pv(kt + 1, 0)
    if mode == "diff":
        lam = lam_ref[...]
        lam_full = (jnp.exp(jnp.sum(lam[0:1] * lam[1:2], axis=-1, keepdims=True))
                    - jnp.exp(jnp.sum(lam[2:3] * lam[3:4], axis=-1, keepdims=True))
                    + lambda_init)
    for st in range(n_st):
        o = acc[st][:n_v] / acc[st][n_v:n_v + 1]
        if mode == "diff":
            a = o[:, :ts].T - lam_full * o[:, ts:].T
            y = _rms(a, sg_ref[...]) * (1.0 - lambda_init)
        else:
            o = jnp.where(kv_half == 1, o[HEAD_DIM:], o[:HEAD_DIM])
            y = jnp.concatenate([o[:, :ts], o[:, ts:]], axis=0).T
        o_ref[st * ts:(st + 1) * ts, :] = y.astype(o_ref.dtype)


def _pair_attn(qkv, *, n_pairs, k_block, v_block, mode, gb=None, lam=None, subln=None,
               lambda_init=0.0, tq=1024, ts=256, tk=512):
    b, s, _ = qkv.shape
    in_specs = [
        pl.BlockSpec((None, tq, LANES), lambda bi, p, qi: (bi, qi, p)),
        pl.BlockSpec((None, s, LANES), lambda bi, p, qi: (bi, 0, k_block(p))),
        pl.BlockSpec((None, s, LANES), lambda bi, p, qi: (bi, 0, v_block(p))),
    ]
    args = [qkv, qkv, qkv]
    if mode == "diff":
        in_specs += [
            pl.BlockSpec((None, 2, B_EBLOCKS, LANES, LANES), lambda bi, p, qi: (p, 0, 0, 0, 0)),
            pl.BlockSpec((4, HEAD_DIM), lambda bi, p, qi: (0, 0)),
            pl.BlockSpec((1, LANES), lambda bi, p, qi: (0, 0)),
        ]
        args += [gb, lam, subln]
    return pl.pallas_call(
        functools.partial(_pair_attn_kernel, tq=tq, ts=ts, tk=tk, n_kt=s // tk, mode=mode,
                          lambda_init=lambda_init),
        grid=(b, n_pairs, s // tq),
        in_specs=in_specs,
        out_specs=pl.BlockSpec((None, tq, LANES), lambda bi, p, qi: (bi, qi, p)),
        out_shape=jax.ShapeDtypeStruct((b, s, n_pairs * LANES), BF16),
        scratch_shapes=[pltpu.VMEM((LANES + ONES_ROWS, s), BF16)],
        compiler_params=_cparams(("parallel", "parallel", "arbitrary")),
        name="pair_attn_" + mode,
    )(*args)


def _window_attn_kernel(slab_ref, ga_ref, o_ref, lse_ref, *, sub_len):
    lane = lax.broadcasted_iota(jnp.int32, (LANES, LANES), 1)
    lo = lane < HEAD_DIM
    n_p = A_COLS // LANES

    def sub_block(res, sb):
        static = isinstance(sb, int)
        q0 = sb * LANES
        if static:
            start = min(max(q0 - A_SIDE, 0), sub_len - A_WIN)
            variant = 1 if q0 == 0 else (2 if q0 == sub_len - LANES else 0)
            qrows, wrows = slice(q0, q0 + LANES), slice(start, start + A_WIN)
        else:
            q0 = pl.multiple_of(q0, LANES)
            start = pl.multiple_of(jnp.clip(q0 - A_SIDE, 0, sub_len - A_WIN), A_SIDE)
            variant = jnp.where(q0 == 0, 1, jnp.where(q0 == sub_len - LANES, 2, 0))
            qrows, wrows = pl.ds(q0, LANES), pl.ds(start, A_WIN)
        lse_all = jnp.full((LANES, LANES), NEG, F32)
        for p in range(n_p):
            cols = slice(p * LANES, (p + 1) * LANES)
            q = slab_ref[res, qrows, cols]
            kw = slab_ref[res, wrows, A_COLS + p * LANES:A_COLS + (p + 1) * LANES]
            vw = slab_ref[res, wrows, 2 * A_COLS + p * LANES:2 * A_COLS + (p + 1) * LANES]
            zero = jnp.zeros_like(q)
            qs = jnp.concatenate([jnp.where(lo, q, zero), jnp.where(lo, zero, q)], axis=0)
            s = lax.dot_general(qs, kw, (((1,), (1,)), ((), ())), preferred_element_type=F32)
            s = s + jnp.concatenate([ga_ref[p, 0, variant], ga_ref[p, 1, variant]], axis=0)
            m = jnp.max(s, axis=-1, keepdims=True)
            e = jnp.exp(s - m)
            l = jnp.sum(e, axis=-1, keepdims=True)
            acc = jnp.dot(e.astype(BF16), vw, preferred_element_type=F32)
            o = acc / l
            lse = m + jnp.log(l)
            o_ref[res, qrows, cols] = jnp.where(lo, o[:LANES], o[LANES:]).astype(o_ref.dtype)
            lse_all = jnp.where(lane == 2 * p, lse[:LANES],
                                jnp.where(lane == 2 * p + 1, lse[LANES:], lse_all))
        lse_ref[res, qrows, :] = lse_all

    n_res = slab_ref.shape[0]
    n_sb = sub_len // LANES
    if n_res * n_sb <= A_UNROLL:
        for res in range(n_res):
            for sb in range(n_sb):
                sub_block(res, sb)
    else:
        def body(i, carry):
            for u in range(A_UNROLL):
                sub_block(0, A_UNROLL * i + u)
            return carry
        lax.fori_loop(0, n_sb // A_UNROLL, body, 0)


def _window_attn(slab, ga):
    b, dil, sub_len, n = slab.shape
    n_sb = sub_len // LANES
    n_res = max(1, min(dil, A_UNROLL // n_sb))
    assert n_res == 1 or n_res * n_sb <= A_UNROLL
    assert n_res > 1 or n_sb <= A_UNROLL or n_sb % A_UNROLL == 0
    out_spec = pl.BlockSpec((None, n_res, sub_len, A_COLS), lambda bi, r: (bi, r, 0, 0))
    lse_spec = pl.BlockSpec((None, n_res, sub_len, LANES), lambda bi, r: (bi, r, 0, 0))
    return pl.pallas_call(
        functools.partial(_window_attn_kernel, sub_len=sub_len),
        grid=(b, dil // n_res),
        in_specs=[
            pl.BlockSpec((None, n_res, sub_len, n), lambda bi, r: (bi, r, 0, 0)),
            pl.BlockSpec(ga.shape, lambda bi, r: (0,) * ga.ndim),
        ],
        out_specs=[out_spec, lse_spec],
        out_shape=[jax.ShapeDtypeStruct((b, dil, sub_len, A_COLS), BF16),
                   jax.ShapeDtypeStruct((b, dil, sub_len, LANES), F32)],
        compiler_params=_cparams(("parallel", "parallel")),
        name="window_attn",
    )(slab, ga)


def _post_kernel(*refs, mix_groups, final_norm, ff_chunk):
    refs = list(refs)
    h_ref = refs.pop(0)
    if mix_groups:
        og_refs = [refs.pop(0) for _ in A_GROUPS]
        lg_refs = [refs.pop(0) for _ in A_GROUPS]
    else:
        o_ref = refs.pop(0)
    wo_ref, g_ref, win_ref, wout_ref = refs[:4]
    refs = refs[4:]
    gf_ref = refs.pop(0) if final_norm else None
    out_ref = refs.pop(0)

    if mix_groups:
        o_scr, lse_scr = refs
        n_cb, tm, _ = o_scr.shape
        cb_per_group = A_COLS // LANES
        for gi, (_, dil, _) in enumerate(A_GROUPS):
            for r in range(dil):
                rows = pl.ds(r, tm // dil, stride=dil) if dil > 1 else slice(None)
                for cb in range(cb_per_group):
                    cols = slice(cb * LANES, (cb + 1) * LANES)
                    o_scr[gi * cb_per_group + cb, rows, :] = og_refs[gi][r, :, cols].astype(F32)
                lse_scr[gi, rows, :] = lg_refs[gi][r]
        lane = lax.broadcasted_iota(jnp.int32, (tm, LANES), 1)
        real = [lane < nh for (_, _, nh) in A_GROUPS]
        lse = [jnp.where(real[gi], lse_scr[gi], NEG) for gi in range(len(A_GROUPS))]
        mx = jnp.max(jnp.maximum(jnp.maximum(lse[0], lse[1]), lse[2]), axis=-1, keepdims=True)
        s_g = [jnp.sum(jnp.where(real[gi], jnp.exp(lse[gi] - mx), 0.0), axis=-1, keepdims=True)
               / A_GROUPS[gi][2] for gi in range(len(A_GROUPS))]
        tot = s_g[0] + s_g[1] + s_g[2]
        parts = []
        for gi in range(len(A_GROUPS)):
            scale = len(A_GROUPS) * (s_g[gi] / tot)
            parts += [(o_scr[gi * cb_per_group + cb] * scale).astype(BF16)
                      for cb in range(cb_per_group)]
        o = jnp.concatenate(parts, axis=1)
    else:
        o = o_ref[...]
    h1 = h_ref[...] + jnp.dot(o, wo_ref[...], preferred_element_type=F32)
    xn = _rms(h1, g_ref[...]).astype(BF16)
    acc = h1
    for c in range(0, D_FF, ff_chunk):
        u = jnp.dot(xn, win_ref[:, c:c + ff_chunk], preferred_element_type=F32)
        u = jnp.maximum(u, 0.0)
        u = (u * u).astype(BF16)
        acc = acc + jnp.dot(u, wout_ref[c:c + ff_chunk, :], preferred_element_type=F32)
    if final_norm:
        acc = _rms(acc, gf_ref[...])
    out_ref[...] = acc


def _post(h, o, w_o, g_mlp, w_in, w_out, *, lse=None, g_final=None, seq=None, tm=512,
          ff_chunk=1024):
    t, d = h.shape
    const = lambda i: (0, 0)
    single = dict(pipeline_mode=pl.Buffered(1))
    in_specs = [pl.BlockSpec((tm, d), lambda i: (i, 0))]
    args = [h]
    scratch = []
    if lse is not None:
        n_o = len(A_GROUPS) * A_COLS
        spb = seq // tm
        for arrs in (o, lse):
            for arr, (_, dil, _) in zip(arrs, A_GROUPS):
                in_specs.append(pl.BlockSpec((None, dil, tm // dil, arr.shape[-1]),
                                             lambda i: (i // spb, 0, i % spb, 0)))
                args.append(arr)
        scratch = [pltpu.VMEM((n_o // LANES, tm, LANES), F32),
                   pltpu.VMEM((len(A_GROUPS), tm, LANES), F32)]
    else:
        n_o = o.shape[1]
        in_specs.append(pl.BlockSpec((tm, n_o), lambda i: (i, 0)))
        args.append(o)
    in_specs += [pl.BlockSpec((n_o, d), const, **single),
                 pl.BlockSpec((1, d), const),
                 pl.BlockSpec((d, D_FF), const, **single),
                 pl.BlockSpec((D_FF, d), const, **single)]
    args += [w_o, g_mlp, w_in, w_out]
    if g_final is not None:
        in_specs.append(pl.BlockSpec((1, d), const))
        args.append(g_final)
    return pl.pallas_call(
        functools.partial(_post_kernel, mix_groups=lse is not None,
                          final_norm=g_final is not None, ff_chunk=ff_chunk),
        grid=(t // tm,),
        in_specs=in_specs,
        out_specs=pl.BlockSpec((tm, d), lambda i: (i, 0)),
        out_shape=jax.ShapeDtypeStruct((t, d), F32),
        scratch_shapes=scratch,
        compiler_params=_cparams(("parallel",)),
        name="post",
    )(*args)


def _t5_bucket(rel):
    nb = NUM_BUCKETS // 2
    max_exact = nb // 2
    side = jnp.where(rel > 0, nb, 0)
    n = jnp.abs(rel)
    nf = jnp.maximum(n, 1).astype(F32)
    large = max_exact + (jnp.log(nf / max_exact) / math.log(REL_MAX_DISTANCE / max_exact)
                         * (nb - max_exact)).astype(jnp.int32)
    large = jnp.minimum(large, nb - 1)
    return side + jnp.where(n < max_exact, n, large)


def _table_lookup(bucket, tab_ref, col):
    vals = [tab_ref[b, col] for b in range(NUM_BUCKETS)]
    bit = 1
    while len(vals) > 1:
        odd = (bucket & bit) != 0
        vals = [jnp.where(odd, vals[i + 1], vals[i]) for i in range(0, len(vals), 2)]
        bit *= 2
    return vals[0]


def _diff_bias_kernel(tab_ref, o_ref):
    hj = pl.program_id(0)
    r = lax.broadcasted_iota(jnp.int32, (LANES, LANES), 0)
    c = lax.broadcasted_iota(jnp.int32, (LANES, LANES), 1)
    for e in range(B_EBLOCKS):
        rel = LANES * (e - B_ECLIP) + r - c
        o_ref[e] = _table_lookup(_t5_bucket(rel), tab_ref, hj) * LOG2E


def _diff_bias_blocks(rel_bias):
    n = rel_bias.shape[1]
    out = pl.pallas_call(
        _diff_bias_kernel,
        grid=(n,),
        in_specs=[pl.BlockSpec(memory_space=pltpu.SMEM)],
        out_specs=pl.BlockSpec((None, B_EBLOCKS, LANES, LANES), lambda i: (i, 0, 0, 0)),
        out_shape=jax.ShapeDtypeStruct((n, B_EBLOCKS, LANES, LANES), F32),
        compiler_params=_cparams(("parallel",)),
        name="diff_bias",
    )(rel_bias.astype(F32))
    return out.reshape(n // 2, 2, B_EBLOCKS, LANES, LANES)


A_WINDOW_OFFSETS = (-A_SIDE, 0, -LANES)


def _window_bias_kernel(tab_ref, o_ref):
    hp = pl.program_id(0)
    dil = jnp.where(hp < A_PAD_HEADS, A_GROUPS[0][1],
                    jnp.where(hp < 2 * A_PAD_HEADS, A_GROUPS[1][1], A_GROUPS[2][1]))
    r = lax.broadcasted_iota(jnp.int32, (LANES, A_WIN), 0)
    c = lax.broadcasted_iota(jnp.int32, (LANES, A_WIN), 1)
    for v, off in enumerate(A_WINDOW_OFFSETS):
        rel = c + off - r
        bias = _table_lookup(_t5_bucket(rel * dil), tab_ref, hp)
        o_ref[v] = jnp.where(jnp.abs(rel) <= A_SIDE, bias, NEG)


def _window_bias_tiles(rel_bias):
    cols = _pad_group_cols(rel_bias.astype(F32), 1, unit=1)
    table = jnp.concatenate(cols, axis=1)
    n = table.shape[1]
    out = pl.pallas_call(
        _window_bias_kernel,
        grid=(n,),
        in_specs=[pl.BlockSpec(memory_space=pltpu.SMEM)],
        out_specs=pl.BlockSpec((None, 3, LANES, A_WIN), lambda i: (i, 0, 0, 0)),
        out_shape=jax.ShapeDtypeStruct((n, 3, LANES, A_WIN), F32),
        compiler_params=_cparams(("parallel",)),
        name="window_bias",
    )(table)
    return out.reshape(len(A_GROUPS), A_PAD_HEADS // 2, 2, 3, LANES, A_WIN)


def _rope_tables(seq):
    n_rows = seq // GRID_W
    row = jnp.repeat(jnp.arange(n_rows, dtype=jnp.int32), GRID_W).astype(F32)
    col = jnp.tile(jnp.arange(GRID_W, dtype=jnp.int32), n_rows).astype(F32)
    half = HEAD_DIM // 4
    inv = ROPE_THETA ** (-jnp.arange(half, dtype=F32) / half)
    ang = jnp.concatenate([row[:, None] * inv, col[:, None] * inv], axis=-1)
    lane = jnp.arange(LANES, dtype=jnp.int32) % HEAD_DIM
    idx = (lane // 32) * half + lane % half
    sign = jnp.where((lane % 32) < half, -1.0, 1.0).astype(F32)
    return jnp.cos(ang)[:, idx], jnp.sin(ang)[:, idx] * sign


def _pad_group_cols(w, axis, unit=HEAD_DIM):
    parts = []
    h0 = 0
    for (_, _, nh) in A_GROUPS:
        sl = [slice(None)] * w.ndim
        sl[axis] = slice(h0 * unit, (h0 + nh) * unit)
        part = w[tuple(sl)]
        if nh < A_PAD_HEADS:
            pad = [(0, 0)] * w.ndim
            pad[axis] = (0, (A_PAD_HEADS - nh) * unit)
            part = jnp.pad(part, pad)
        parts.append(part)
        h0 += nh
    return parts


def _dilated_layer(h, g_mix, w_qkv, w_o, bias_tiles, batch, seq):
    d_attn = w_qkv.shape[1] // 3
    wq, wk, wv = (w_qkv[:, i * d_attn:(i + 1) * d_attn] for i in range(3))
    qs, ks, vs = _pad_group_cols(wq, 1), _pad_group_cols(wk, 1), _pad_group_cols(wv, 1)
    w = jnp.concatenate([jnp.concatenate([qs[g], ks[g], vs[g]], axis=1) for g in range(3)],
                        axis=1).astype(BF16)
    slab_cols = 3 * A_COLS
    scale = jnp.tile(jnp.concatenate([jnp.full((A_COLS,), Q_SCALE, F32),
                                      jnp.ones((2 * A_COLS,), F32)]), 3)[None, :]
    slabs = _norm_proj_groups(h, g_mix, w, scale, batch=batch, seq=seq)
    outs, lses = [], []
    for g in range(len(A_GROUPS)):
        o, lse = _window_attn(slabs[g], bias_tiles[g])
        outs.append(o)
        lses.append(lse)
    w_o_pad = jnp.concatenate(_pad_group_cols(w_o, 0), axis=0).astype(BF16)
    return outs, lses, w_o_pad


def kernel(x, rel_bias, norm_mix_g, norm_mlp_g, norm_final_g, a_w_qkv, a_w_o, b_w_qkv,
           b_lambda, b_subln_g, b_w_o, c_w_qkv, c_q_norm_g, c_k_norm_g, c_w_o, mlp_w_in,
           mlp_w_out):
    batch, seq, d = x.shape
    t = batch * seq
    h = x.reshape(t, d)
    cos, sin = _rope_tables(seq)
    window_bias = _window_bias_tiles(rel_bias)
    for i in range(N_LAYERS):
        kind, j = i % 3, i // 3
        g_mix = norm_mix_g[i][None, :]
        lse = None
        if kind == 0:
            o, lse, w_o = _dilated_layer(h, g_mix, a_w_qkv[j], a_w_o[j], window_bias, batch, seq)
        elif kind == 1:
            scale = jnp.concatenate([jnp.full((d,), Q_SCALE_LOG2, F32),
                                     jnp.ones((2 * d,), F32)])[None, :]
            qkv = _norm_proj(h, g_mix, b_w_qkv[j].astype(BF16), scale, chunk=1024)
            n_pairs = d // LANES
            o = _pair_attn(qkv.reshape(batch, seq, 3 * d), n_pairs=n_pairs,
                           k_block=lambda p: n_pairs + p, v_block=lambda p: 2 * n_pairs + p,
                           mode="diff", gb=_diff_bias_blocks(rel_bias), lam=b_lambda[j],
                           subln=b_subln_g[j][None, :], lambda_init=_lambda_init(i))
            o = o.reshape(t, d)
            w_o = b_w_o[j].astype(BF16)
        else:
            n_q = d
            n_kv = C_KV_HEADS * HEAD_DIM
            qkv = _norm_proj_rope(h, g_mix, c_w_qkv[j].astype(BF16),
                                  jnp.tile(c_q_norm_g[j], 2)[None, :],
                                  jnp.tile(c_k_norm_g[j], 2)[None, :], cos, sin,
                                  n_q=n_q, n_k=n_kv, seq=seq, chunk=512)
            n_pairs = n_q // LANES
            kv_blocks = n_kv // LANES
            o = _pair_attn(qkv.reshape(batch, seq, n_q + 2 * n_kv), n_pairs=n_pairs,
                           k_block=lambda p: n_pairs + p // 4,
                           v_block=lambda p: n_pairs + kv_blocks + p // 4,
                           mode="gqa", ts=512)
            o = o.reshape(t, d)
            w_o = c_w_o[j].astype(BF16)
        g_final = norm_final_g[None, :] if i == N_LAYERS - 1 else None
        h = _post(h, o, w_o, norm_mlp_g[i][None, :], mlp_w_in[i].astype(BF16),
                  mlp_w_out[i].astype(BF16), lse=lse, g_final=g_final, seq=seq)
    return h.reshape(batch, seq, d)
```

```python
import functools
import math

import jax
import jax.numpy as jnp
from jax import lax
from jax.experimental import pallas as pl
from jax.experimental.pallas import tpu as pltpu

F32 = jnp.float32
BF16 = jnp.bfloat16

D_MODEL = 1024
HEAD_DIM = 64
LANES = 128
EPS = 1e-6
NEG = -1e30
N_LAYERS = 4
D_FF = 4 * D_MODEL
A_GROUPS = ((128, 1, 6), (512, 4, 5), (2048, 16, 5))
A_PAD_HEADS = 6
A_COLS = A_PAD_HEADS * HEAD_DIM
A_WIN = 256
A_SIDE = 64
A_UNROLL = 8
NUM_BUCKETS = 32
REL_MAX_DISTANCE = 1024
B_EBLOCKS = 13
B_ECLIP = 6
C_KV_HEADS = 4
GRID_W = 64
ROPE_THETA = 10000.0

LOG2E = math.log2(math.e)
Q_SCALE = HEAD_DIM ** -0.5
Q_SCALE_LOG2 = Q_SCALE * LOG2E

ONES_ROWS = 16

VMEM_LIMIT = 56 * 1024 * 1024


def _cparams(sem):
    return pltpu.CompilerParams(dimension_semantics=sem, vmem_limit_bytes=VMEM_LIMIT)


def _lambda_init(layer_idx):
    return 0.8 - 0.6 * math.exp(-0.3 * layer_idx)


def _rms(x, g):
    ms = jnp.mean(x * x, axis=-1, keepdims=True)
    return x * lax.rsqrt(ms + EPS) * g


def _norm_proj_kernel(x_ref, g_ref, w_ref, cs_ref, o_ref, *, chunk):
    xn = _rms(x_ref[...], g_ref[...]).astype(BF16)
    n = o_ref.shape[-1]
    for c in range(0, n, chunk):
        y = jnp.dot(xn, w_ref[:, c:c + chunk], preferred_element_type=F32)
        o_ref[:, c:c + chunk] = (y * cs_ref[:, c:c + chunk]).astype(o_ref.dtype)


def _norm_proj(x, g, w, col_scale, *, tm=512, chunk=None):
    t, d = x.shape
    n = w.shape[1]
    chunk = chunk or n
    return pl.pallas_call(
        functools.partial(_norm_proj_kernel, chunk=chunk),
        grid=(t // tm,),
        in_specs=[
            pl.BlockSpec((tm, d), lambda i: (i, 0)),
            pl.BlockSpec((1, d), lambda i: (0, 0)),
            pl.BlockSpec((d, n), lambda i: (0, 0)),
            pl.BlockSpec((1, n), lambda i: (0, 0)),
        ],
        out_specs=pl.BlockSpec((tm, n), lambda i: (i, 0)),
        out_shape=jax.ShapeDtypeStruct((t, n), BF16),
        compiler_params=_cparams(("parallel",)),
        name="norm_proj",
    )(x, g, w, col_scale)


def _norm_proj_groups_kernel(x_ref, g_ref, w_ref, cs_ref, o0_ref, o1_ref, o2_ref, y_scr):
    xn = _rms(x_ref[...], g_ref[...]).astype(BF16)
    tm = x_ref.shape[0]
    n = 3 * A_COLS
    for gi, o_ref in enumerate((o0_ref, o1_ref, o2_ref)):
        dil = A_GROUPS[gi][1]
        cols = slice(gi * n, (gi + 1) * n)
        y = jnp.dot(xn, w_ref[:, cols], preferred_element_type=F32) * cs_ref[:, cols]
        if dil == 1:
            o_ref[0] = y.astype(o_ref.dtype)
        else:
            for cb in range(n // LANES):
                y_scr[cb] = y[:, cb * LANES:(cb + 1) * LANES]
            for r in range(dil):
                for cb in range(n // LANES):
                    o_ref[r, :, cb * LANES:(cb + 1) * LANES] = (
                        y_scr[cb, pl.ds(r, tm // dil, stride=dil), :].astype(o_ref.dtype))


def _norm_proj_groups(x, g, w, col_scale, *, batch, seq, tm=512):
    t, d = x.shape
    n = 3 * A_COLS
    spb = seq // tm
    const = lambda i: (0, 0)
    out_specs, out_shape = [], []
    for (_, dil, _) in A_GROUPS:
        out_specs.append(pl.BlockSpec((None, dil, tm // dil, n),
                                      lambda i: (i // spb, 0, i % spb, 0)))
        out_shape.append(jax.ShapeDtypeStruct((batch, dil, seq // dil, n), BF16))
    return pl.pallas_call(
        _norm_proj_groups_kernel,
        grid=(t // tm,),
        in_specs=[
            pl.BlockSpec((tm, d), lambda i: (i, 0)),
            pl.BlockSpec((1, d), const),
            pl.BlockSpec((d, 3 * n), const),
            pl.BlockSpec((1, 3 * n), const),
        ],
        out_specs=out_specs,
        out_shape=out_shape,
        scratch_shapes=[pltpu.VMEM((n // LANES, tm, LANES), F32)],
        compiler_params=_cparams(("parallel",)),
        name="norm_proj_groups",
    )(x, g, w, col_scale)


def _head_norm_rope(y, gain, cos, sin_signed, lane):
    lo = lane < HEAD_DIM
    y2 = y * y
    s0 = jnp.sum(jnp.where(lo, y2, 0.0), axis=-1, keepdims=True)
    s1 = jnp.sum(jnp.where(lo, 0.0, y2), axis=-1, keepdims=True)
    inv = jnp.where(lo, lax.rsqrt(s0 / HEAD_DIM + EPS), lax.rsqrt(s1 / HEAD_DIM + EPS))
    yn = y * inv * gain
    first = (lane & 31) < 16
    partner = jnp.where(first, pltpu.roll(yn, LANES - 16, 1), pltpu.roll(yn, 16, 1))
    return yn * cos + partner * sin_signed


def _norm_proj_rope_kernel(x_ref, g_ref, w_ref, qg_ref, kg_ref, cos_ref, sin_ref, o_ref,
                           *, n_q, n_k, chunk):
    xn = _rms(x_ref[...], g_ref[...]).astype(BF16)
    n = o_ref.shape[-1]
    tm = x_ref.shape[0]
    lane = lax.broadcasted_iota(jnp.int32, (tm, LANES), 1)
    cos = cos_ref[...]
    sin = sin_ref[...]
    for c in range(0, n, chunk):
        y = jnp.dot(xn, w_ref[:, c:c + chunk], preferred_element_type=F32)
        for b in range(0, chunk, LANES):
            col = c + b
            blk = y[:, b:b + LANES]
            if col < n_q:
                blk = _head_norm_rope(blk, qg_ref[...], cos, sin, lane) * Q_SCALE_LOG2
            elif col < n_q + n_k:
                blk = _head_norm_rope(blk, kg_ref[...], cos, sin, lane)
            o_ref[:, col:col + LANES] = blk.astype(o_ref.dtype)


def _norm_proj_rope(x, g, w, q_gain, k_gain, cos, sin, *, n_q, n_k, seq, tm=512, chunk=512):
    t, d = x.shape
    n = w.shape[1]
    sblk = seq // tm
    return pl.pallas_call(
        functools.partial(_norm_proj_rope_kernel, n_q=n_q, n_k=n_k, chunk=chunk),
        grid=(t // tm,),
        in_specs=[
            pl.BlockSpec((tm, d), lambda i: (i, 0)),
            pl.BlockSpec((1, d), lambda i: (0, 0)),
            pl.BlockSpec((d, n), lambda i: (0, 0)),
            pl.BlockSpec((1, LANES), lambda i: (0, 0)),
            pl.BlockSpec((1, LANES), lambda i: (0, 0)),
            pl.BlockSpec((tm, LANES), lambda i: (i % sblk, 0)),
            pl.BlockSpec((tm, LANES), lambda i: (i % sblk, 0)),
        ],
        out_specs=pl.BlockSpec((tm, n), lambda i: (i, 0)),
        out_shape=jax.ShapeDtypeStruct((t, n), BF16),
        compiler_params=_cparams(("parallel",)),
        name="norm_proj_rope",
    )(x, g, w, q_gain, k_gain, cos, sin)


def _pair_attn_kernel(*refs, tq, ts, tk, n_kt, mode, lambda_init):
    vt_ref = refs[-1]
    if mode == "diff":
        q_ref, k_ref, v_ref, gb_ref, lam_ref, sg_ref, o_ref = refs[:-1]
    else:
        q_ref, k_ref, v_ref, o_ref = refs[:-1]
    qi = pl.program_id(2)
    seq = k_ref.shape[0]

    n_v = HEAD_DIM if mode == "gqa" else LANES
    kv_half = (pl.program_id(1) >> 1) & 1

    @pl.when(qi == 0)
    def _():
        for c in range(0, seq, tk):
            vt = v_ref[c:c + tk, :].astype(F32).T
            if mode == "gqa":
                vt = jnp.where(kv_half == 1, vt[HEAD_DIM:], vt[:HEAD_DIM])
            vt_ref[:n_v, c:c + tk] = vt.astype(BF16)
        vt_ref[n_v:, :] = jnp.ones((ONES_ROWS, seq), BF16)

    n_st = tq // ts
    lane = lax.broadcasted_iota(jnp.int32, (ts, LANES), 1)
    lo = lane < HEAD_DIM
    qs_st = []
    for st in range(n_st):
        q = q_ref[st * ts:(st + 1) * ts, :]
        zero = jnp.zeros_like(q)
        if mode == "gqa":
            q_sw = pltpu.roll(q.astype(F32), HEAD_DIM, 1).astype(q.dtype)
            in_half = lo == (kv_half == 0)
            h0 = jnp.where(kv_half == 0, q, q_sw)
            h1 = jnp.where(kv_half == 0, q_sw, q)
            qs = jnp.concatenate([jnp.where(in_half, h0, zero), jnp.where(in_half, h1, zero)],
                                 axis=0)
        else:
            qs = jnp.concatenate([jnp.where(lo, q, zero), jnp.where(lo, zero, q)], axis=0)
        qs_st.append(qs)
    rb_n = ts // LANES
    cb_n = tk // LANES

    def logits(kt, st):
        k = k_ref[kt * tk:(kt + 1) * tk, :]
        s = lax.dot_general(k, qs_st[st], (((1,), (1,)), ((), ())),
                            preferred_element_type=F32)
        if mode == "diff":
            base = kt * cb_n - (qi * n_st + st) * rb_n
            rows = []
            for cb in range(cb_n):
                blocks = []
                for j in range(2):
                    for rb in range(rb_n):
                        e = jnp.clip(base + (cb - rb), -B_ECLIP, B_ECLIP) + B_ECLIP
                        blocks.append(gb_ref[j, e])
                rows.append(jnp.concatenate(blocks, axis=1))
            s = s + jnp.concatenate(rows, axis=0)
        return s

    m = [jnp.full((1, 2 * ts), NEG, F32)] * n_st
    acc = [jnp.zeros((n_v + ONES_ROWS, 2 * ts), F32)] * n_st
    s_tile = {}

    def qk(kt, st):
        if kt < n_kt:
            s_tile[kt, st] = logits(kt, st)

    def softmax_pv(kt, st):
        if kt >= n_kt:
            return
        s = s_tile.pop((kt, st))
        m_new = jnp.maximum(m[st], jnp.max(s, axis=0, keepdims=True))
        alpha = jnp.exp2(m[st] - m_new)
        p = jnp.exp2(s - m_new).astype(BF16)
        vt = vt_ref[:, kt * tk:(kt + 1) * tk]
        acc[st] = alpha * acc[st] + jnp.dot(vt, p, preferred_element_type=F32)
        m[st] = m_new

    for st in range(n_st):
        qk(0, st)
    softmax_pv(0, 0)
    for kt in range(n_kt):
        for st in range(n_st):
            qk(kt + 1, st)
            if st + 1 < n_st:
                softmax_pv(kt, st + 1)
            else:
                softmax_pv(kt + 1, 0)
    if mode == "diff":
        lam = lam_ref[...]
        lam_full = (jnp.exp(jnp.sum(lam[0:1] * lam[1:2], axis=-1, keepdims=True))
                    - jnp.exp(jnp.sum(lam[2:3] * lam[3:4], axis=-1, keepdims=True))
                    + lambda_init)
    for st in range(n_st):
        o = acc[st][:n_v] / acc[st][n_v:n_v + 1]
        if mode == "diff":
            a = o[:, :ts].T - lam_full * o[:, ts:].T
            y = _rms(a, sg_ref[...]) * (1.0 - lambda_init)
        else:
            y = jnp.concatenate([o[:, :ts], o[:, ts:]], axis=0).T
        o_ref[st * ts:(st + 1) * ts, :] = y.astype(o_ref.dtype)


def _pair_attn(qkv, *, n_pairs, k_block, v_block, mode, gb=None, lam=None, subln=None,
               lambda_init=0.0, tq=1024, ts=256, tk=512):
    b, s, _ = qkv.shape
    in_specs = [
        pl.BlockSpec((None, tq, LANES), lambda bi, p, qi: (bi, qi, p)),
        pl.BlockSpec((None, s, LANES), lambda bi, p, qi: (bi, 0, k_block(p))),
        pl.BlockSpec((None, s, LANES), lambda bi, p, qi: (bi, 0, v_block(p))),
    ]
    args = [qkv, qkv, qkv]
    if mode == "diff":
        in_specs += [
            pl.BlockSpec((None, 2, B_EBLOCKS, LANES, LANES), lambda bi, p, qi: (p, 0, 0, 0, 0)),
            pl.BlockSpec((4, HEAD_DIM), lambda bi, p, qi: (0, 0)),
            pl.BlockSpec((1, LANES), lambda bi, p, qi: (0, 0)),
        ]
        args += [gb, lam, subln]
    return pl.pallas_call(
        functools.partial(_pair_attn_kernel, tq=tq, ts=ts, tk=tk, n_kt=s // tk, mode=mode,
                          lambda_init=lambda_init),
        grid=(b, n_pairs, s // tq),
        in_specs=in_specs,
        out_specs=pl.BlockSpec((None, tq, LANES), lambda bi, p, qi: (bi, qi, p)),
        out_shape=jax.ShapeDtypeStruct((b, s, n_pairs * LANES), BF16),
        scratch_shapes=[pltpu.VMEM(((HEAD_DIM if mode == "gqa" else LANES) + ONES_ROWS, s),
                                   BF16)],
        compiler_params=_cparams(("parallel", "parallel", "arbitrary")),
        name="pair_attn_" + mode,
    )(*args)


def _window_attn_kernel(slab_ref, ga_ref, o_ref, lse_ref, *, sub_len):
    lane = lax.broadcasted_iota(jnp.int32, (LANES, LANES), 1)
    lo = lane < HEAD_DIM
    n_p = A_COLS // LANES

    def sub_block(res, sb):
        static = isinstance(sb, int)
        q0 = sb * LANES
        if static:
            start = min(max(q0 - A_SIDE, 0), sub_len - A_WIN)
            variant = 1 if q0 == 0 else (2 if q0 == sub_len - LANES else 0)
            qrows, wrows = slice(q0, q0 + LANES), slice(start, start + A_WIN)
        else:
            q0 = pl.multiple_of(q0, LANES)
            start = pl.multiple_of(jnp.clip(q0 - A_SIDE, 0, sub_len - A_WIN), A_SIDE)
            variant = jnp.where(q0 == 0, 1, jnp.where(q0 == sub_len - LANES, 2, 0))
            qrows, wrows = pl.ds(q0, LANES), pl.ds(start, A_WIN)
        lse_all = jnp.full((LANES, LANES), NEG, F32)
        for p in range(n_p):
            cols = slice(p * LANES, (p + 1) * LANES)
            q = slab_ref[res, qrows, cols]
            kw = slab_ref[res, wrows, A_COLS + p * LANES:A_COLS + (p + 1) * LANES]
            vw = slab_ref[res, wrows, 2 * A_COLS + p * LANES:2 * A_COLS + (p + 1) * LANES]
            zero = jnp.zeros_like(q)
            qs = jnp.concatenate([jnp.where(lo, q, zero), jnp.where(lo, zero, q)], axis=0)
            s = lax.dot_general(qs, kw, (((1,), (1,)), ((), ())), preferred_element_type=F32)
            s = s + jnp.concatenate([ga_ref[p, 0, variant], ga_ref[p, 1, variant]], axis=0)
            m = jnp.max(s, axis=-1, keepdims=True)
            e = jnp.exp(s - m)
            l = jnp.sum(e, axis=-1, keepdims=True)
            acc = jnp.dot(e.astype(BF16), vw, preferred_element_type=F32)
            o = acc / l
            lse = m + jnp.log(l)
            o_ref[res, qrows, cols] = jnp.where(lo, o[:LANES], o[LANES:]).astype(o_ref.dtype)
            lse_all = jnp.where(lane == 2 * p, lse[:LANES],
                                jnp.where(lane == 2 * p + 1, lse[LANES:], lse_all))
        lse_ref[res, qrows, :] = lse_all

    n_res = slab_ref.shape[0]
    n_sb = sub_len // LANES
    if n_res * n_sb <= A_UNROLL:
        for res in range(n_res):
            for sb in range(n_sb):
                sub_block(res, sb)
    else:
        def body(i, carry):
            for u in range(A_UNROLL):
                sub_block(0, A_UNROLL * i + u)
            return carry
        lax.fori_loop(0, n_sb // A_UNROLL, body, 0)


def _window_attn(slab, ga):
    b, dil, sub_len, n = slab.shape
    n_sb = sub_len // LANES
    n_res = max(1, min(dil, A_UNROLL // n_sb))
    assert n_res == 1 or n_res * n_sb <= A_UNROLL
    assert n_res > 1 or n_sb <= A_UNROLL or n_sb % A_UNROLL == 0
    out_spec = pl.BlockSpec((None, n_res, sub_len, A_COLS), lambda bi, r: (bi, r, 0, 0))
    lse_spec = pl.BlockSpec((None, n_res, sub_len, LANES), lambda bi, r: (bi, r, 0, 0))
    return pl.pallas_call(
        functools.partial(_window_attn_kernel, sub_len=sub_len),
        grid=(b, dil // n_res),
        in_specs=[
            pl.BlockSpec((None, n_res, sub_len, n), lambda bi, r: (bi, r, 0, 0)),
            pl.BlockSpec(ga.shape, lambda bi, r: (0,) * ga.ndim),
        ],
        out_specs=[out_spec, lse_spec],
        out_shape=[jax.ShapeDtypeStruct((b, dil, sub_len, A_COLS), BF16),
                   jax.ShapeDtypeStruct((b, dil, sub_len, LANES), F32)],
        compiler_params=_cparams(("parallel", "parallel")),
        name="window_attn",
    )(slab, ga)


def _post_kernel(*refs, mix_groups, final_norm, ff_chunk):
    refs = list(refs)
    h_ref = refs.pop(0)
    if mix_groups:
        og_refs = [refs.pop(0) for _ in A_GROUPS]
        lg_refs = [refs.pop(0) for _ in A_GROUPS]
    else:
        o_ref = refs.pop(0)
    wo_ref, g_ref, win_ref, wout_ref = refs[:4]
    refs = refs[4:]
    gf_ref = refs.pop(0) if final_norm else None
    out_ref = refs.pop(0)

    if mix_groups:
        o_scr, lse_scr = refs
        n_cb, tm, _ = o_scr.shape
        cb_per_group = A_COLS // LANES
        for gi, (_, dil, _) in enumerate(A_GROUPS):
            for r in range(dil):
                rows = pl.ds(r, tm // dil, stride=dil) if dil > 1 else slice(None)
                for cb in range(cb_per_group):
                    cols = slice(cb * LANES, (cb + 1) * LANES)
                    o_scr[gi * cb_per_group + cb, rows, :] = og_refs[gi][r, :, cols].astype(F32)
                lse_scr[gi, rows, :] = lg_refs[gi][r]
        lane = lax.broadcasted_iota(jnp.int32, (tm, LANES), 1)
        real = [lane < nh for (_, _, nh) in A_GROUPS]
        lse = [jnp.where(real[gi], lse_scr[gi], NEG) for gi in range(len(A_GROUPS))]
        mx = jnp.max(jnp.maximum(jnp.maximum(lse[0], lse[1]), lse[2]), axis=-1, keepdims=True)
        s_g = [jnp.sum(jnp.where(real[gi], jnp.exp(lse[gi] - mx), 0.0), axis=-1, keepdims=True)
               / A_GROUPS[gi][2] for gi in range(len(A_GROUPS))]
        tot = s_g[0] + s_g[1] + s_g[2]
        parts = []
        for gi in range(len(A_GROUPS)):
            scale = len(A_GROUPS) * (s_g[gi] / tot)
            parts += [(o_scr[gi * cb_per_group + cb] * scale).astype(BF16)
                      for cb in range(cb_per_group)]
        o = jnp.concatenate(parts, axis=1)
    else:
        o = o_ref[...]
    h1 = h_ref[...] + jnp.dot(o, wo_ref[...], preferred_element_type=F32)
    xn = _rms(h1, g_ref[...]).astype(BF16)
    acc = h1
    for c in range(0, D_FF, ff_chunk):
        u = jnp.dot(xn, win_ref[:, c:c + ff_chunk], preferred_element_type=F32)
        u = jnp.maximum(u, 0.0)
        u = (u * u).astype(BF16)
        acc = acc + jnp.dot(u, wout_ref[c:c + ff_chunk, :], preferred_element_type=F32)
    if final_norm:
        acc = _rms(acc, gf_ref[...])
    out_ref[...] = acc


def _post(h, o, w_o, g_mlp, w_in, w_out, *, lse=None, g_final=None, seq=None, tm=512,
          ff_chunk=1024):
    t, d = h.shape
    const = lambda i: (0, 0)
    single = dict(pipeline_mode=pl.Buffered(1))
    in_specs = [pl.BlockSpec((tm, d), lambda i: (i, 0))]
    args = [h]
    scratch = []
    if lse is not None:
        n_o = len(A_GROUPS) * A_COLS
        spb = seq // tm
        for arrs in (o, lse):
            for arr, (_, dil, _) in zip(arrs, A_GROUPS):
                in_specs.append(pl.BlockSpec((None, dil, tm // dil, arr.shape[-1]),
                                             lambda i: (i // spb, 0, i % spb, 0)))
                args.append(arr)
        scratch = [pltpu.VMEM((n_o // LANES, tm, LANES), F32),
                   pltpu.VMEM((len(A_GROUPS), tm, LANES), F32)]
    else:
        n_o = o.shape[1]
        in_specs.append(pl.BlockSpec((tm, n_o), lambda i: (i, 0)))
        args.append(o)
    in_specs += [pl.BlockSpec((n_o, d), const, **single),
                 pl.BlockSpec((1, d), const),
                 pl.BlockSpec((d, D_FF), const, **single),
                 pl.BlockSpec((D_FF, d), const, **single)]
    args += [w_o, g_mlp, w_in, w_out]
    if g_final is not None:
        in_specs.append(pl.BlockSpec((1, d), const))
        args.append(g_final)
    return pl.pallas_call(
        functools.partial(_post_kernel, mix_groups=lse is not None,
                          final_norm=g_final is not None, ff_chunk=ff_chunk),
        grid=(t // tm,),
        in_specs=in_specs,
        out_specs=pl.BlockSpec((tm, d), lambda i: (i, 0)),
        out_shape=jax.ShapeDtypeStruct((t, d), F32),
        scratch_shapes=scratch,
        compiler_params=_cparams(("parallel",)),
        name="post",
    )(*args)


def _t5_bucket(rel):
    nb = NUM_BUCKETS // 2
    max_exact = nb // 2
    side = jnp.where(rel > 0, nb, 0)
    n = jnp.abs(rel)
    nf = jnp.maximum(n, 1).astype(F32)
    large = max_exact + (jnp.log(nf / max_exact) / math.log(REL_MAX_DISTANCE / max_exact)
                         * (nb - max_exact)).astype(jnp.int32)
    large = jnp.minimum(large, nb - 1)
    return side + jnp.where(n < max_exact, n, large)


def _table_lookup(bucket, tab_ref, col):
    vals = [tab_ref[b, col] for b in range(NUM_BUCKETS)]
    bit = 1
    while len(vals) > 1:
        odd = (bucket & bit) != 0
        vals = [jnp.where(odd, vals[i + 1], vals[i]) for i in range(0, len(vals), 2)]
        bit *= 2
    return vals[0]


def _diff_bias_kernel(tab_ref, o_ref):
    hj = pl.program_id(0)
    r = lax.broadcasted_iota(jnp.int32, (LANES, LANES), 0)
    c = lax.broadcasted_iota(jnp.int32, (LANES, LANES), 1)
    for e in range(B_EBLOCKS):
        rel = LANES * (e - B_ECLIP) + r - c
        o_ref[e] = _table_lookup(_t5_bucket(rel), tab_ref, hj) * LOG2E


def _diff_bias_blocks(rel_bias):
    n = rel_bias.shape[1]
    out = pl.pallas_call(
        _diff_bias_kernel,
        grid=(n,),
        in_specs=[pl.BlockSpec(memory_space=pltpu.SMEM)],
        out_specs=pl.BlockSpec((None, B_EBLOCKS, LANES, LANES), lambda i: (i, 0, 0, 0)),
        out_shape=jax.ShapeDtypeStruct((n, B_EBLOCKS, LANES, LANES), F32),
        compiler_params=_cparams(("parallel",)),
        name="diff_bias",
    )(rel_bias.astype(F32))
    return out.reshape(n // 2, 2, B_EBLOCKS, LANES, LANES)


A_WINDOW_OFFSETS = (-A_SIDE, 0, -LANES)


def _window_bias_kernel(tab_ref, o_ref):
    hp = pl.program_id(0)
    dil = jnp.where(hp < A_PAD_HEADS, A_GROUPS[0][1],
                    jnp.where(hp < 2 * A_PAD_HEADS, A_GROUPS[1][1], A_GROUPS[2][1]))
    r = lax.broadcasted_iota(jnp.int32, (LANES, A_WIN), 0)
    c = lax.broadcasted_iota(jnp.int32, (LANES, A_WIN), 1)
    for v, off in enumerate(A_WINDOW_OFFSETS):
        rel = c + off - r
        bias = _table_lookup(_t5_bucket(rel * dil), tab_ref, hp)
        o_ref[v] = jnp.where(jnp.abs(rel) <= A_SIDE, bias, NEG)


def _window_bias_tiles(rel_bias):
    cols = _pad_group_cols(rel_bias.astype(F32), 1, unit=1)
    table = jnp.concatenate(cols, axis=1)
    n = table.shape[1]
    out = pl.pallas_call(
        _window_bias_kernel,
        grid=(n,),
        in_specs=[pl.BlockSpec(memory_space=pltpu.SMEM)],
        out_specs=pl.BlockSpec((None, 3, LANES, A_WIN), lambda i: (i, 0, 0, 0)),
        out_shape=jax.ShapeDtypeStruct((n, 3, LANES, A_WIN), F32),
        compiler_params=_cparams(("parallel",)),
        name="window_bias",
    )(table)
    return out.reshape(len(A_GROUPS), A_PAD_HEADS // 2, 2, 3, LANES, A_WIN)


def _rope_tables(seq):
    n_rows = seq // GRID_W
    row = jnp.repeat(jnp.arange(n_rows, dtype=jnp.int32), GRID_W).astype(F32)
    col = jnp.tile(jnp.arange(GRID_W, dtype=jnp.int32), n_rows).astype(F32)
    half = HEAD_DIM // 4
    inv = ROPE_THETA ** (-jnp.arange(half, dtype=F32) / half)
    ang = jnp.concatenate([row[:, None] * inv, col[:, None] * inv], axis=-1)
    lane = jnp.arange(LANES, dtype=jnp.int32) % HEAD_DIM
    idx = (lane // 32) * half + lane % half
    sign = jnp.where((lane % 32) < half, -1.0, 1.0).astype(F32)
    return jnp.cos(ang)[:, idx], jnp.sin(ang)[:, idx] * sign


def _pad_group_cols(w, axis, unit=HEAD_DIM):
    parts = []
    h0 = 0
    for (_, _, nh) in A_GROUPS:
        sl = [slice(None)] * w.ndim
        sl[axis] = slice(h0 * unit, (h0 + nh) * unit)
        part = w[tuple(sl)]
        if nh < A_PAD_HEADS:
            pad = [(0, 0)] * w.ndim
            pad[axis] = (0, (A_PAD_HEADS - nh) * unit)
            part = jnp.pad(part, pad)
        parts.append(part)
        h0 += nh
    return parts


def _dilated_layer(h, g_mix, w_qkv, w_o, bias_tiles, batch, seq):
    d_attn = w_qkv.shape[1] // 3
    wq, wk, wv = (w_qkv[:, i * d_attn:(i + 1) * d_attn] for i in range(3))
    qs, ks, vs = _pad_group_cols(wq, 1), _pad_group_cols(wk, 1), _pad_group_cols(wv, 1)
    w = jnp.concatenate([jnp.concatenate([qs[g], ks[g], vs[g]], axis=1) for g in range(3)],
                        axis=1).astype(BF16)
    scale = jnp.tile(jnp.concatenate([jnp.full((A_COLS,), Q_SCALE, F32),
                                      jnp.ones((2 * A_COLS,), F32)]), 3)[None, :]
    slabs = _norm_proj_groups(h, g_mix, w, scale, batch=batch, seq=seq)
    outs, lses = [], []
    for g in range(len(A_GROUPS)):
        o, lse = _window_attn(slabs[g], bias_tiles[g])
        outs.append(o)
        lses.append(lse)
    w_o_pad = jnp.concatenate(_pad_group_cols(w_o, 0), axis=0).astype(BF16)
    return outs, lses, w_o_pad


def kernel(x, rel_bias, norm_mix_g, norm_mlp_g, norm_final_g, a_w_qkv, a_w_o, b_w_qkv,
           b_lambda, b_subln_g, b_w_o, c_w_qkv, c_q_norm_g, c_k_norm_g, c_w_o, mlp_w_in,
           mlp_w_out):
    batch, seq, d = x.shape
    t = batch * seq
    h = x.reshape(t, d)
    cos, sin = _rope_tables(seq)
    window_bias = _window_bias_tiles(rel_bias)
    for i in range(N_LAYERS):
        kind, j = i % 3, i // 3
        g_mix = norm_mix_g[i][None, :]
        lse = None
        if kind == 0:
            o, lse, w_o = _dilated_layer(h, g_mix, a_w_qkv[j], a_w_o[j], window_bias, batch, seq)
        elif kind == 1:
            scale = jnp.concatenate([jnp.full((d,), Q_SCALE_LOG2, F32),
                                     jnp.ones((2 * d,), F32)])[None, :]
            qkv = _norm_proj(h, g_mix, b_w_qkv[j].astype(BF16), scale, chunk=1024)
            n_pairs = d // LANES
            o = _pair_attn(qkv.reshape(batch, seq, 3 * d), n_pairs=n_pairs,
                           k_block=lambda p: n_pairs + p, v_block=lambda p: 2 * n_pairs + p,
                           mode="diff", gb=_diff_bias_blocks(rel_bias), lam=b_lambda[j],
                           subln=b_subln_g[j][None, :], lambda_init=_lambda_init(i))
            o = o.reshape(t, d)
            w_o = b_w_o[j].astype(BF16)
        else:
            n_q = d
            n_kv = C_KV_HEADS * HEAD_DIM
            qkv = _norm_proj_rope(h, g_mix, c_w_qkv[j].astype(BF16),
                                  jnp.tile(c_q_norm_g[j], 2)[None, :],
                                  jnp.tile(c_k_norm_g[j], 2)[None, :], cos, sin,
                                  n_q=n_q, n_k=n_kv, seq=seq, chunk=512)
            n_pairs = n_q // LANES
            kv_blocks = n_kv // LANES
            o = _pair_attn(qkv.reshape(batch, seq, n_q + 2 * n_kv), n_pairs=n_pairs,
                           k_block=lambda p: n_pairs + p // 4,
                           v_block=lambda p: n_pairs + kv_blocks + p // 4,
                           mode="gqa", tq=2048, ts=512)
            o = o.reshape(t, d)
            w_o = c_w_o[j].astype(BF16)
        g_final = norm_final_g[None, :] if i == N_LAYERS - 1 else None
        h = _post(h, o, w_o, norm_mlp_g[i][None, :], mlp_w_in[i].astype(BF16),
                  mlp_w_out[i].astype(BF16), lse=lse, g_final=g_final, seq=seq)
    return h.reshape(batch, seq, d)
```

```python
import functools
import math

import jax
import jax.numpy as jnp
from jax import lax
from jax.experimental import pallas as pl
from jax.experimental.pallas import tpu as pltpu

F32 = jnp.float32
BF16 = jnp.bfloat16

D_MODEL = 1024
HEAD_DIM = 64
LANES = 128
EPS = 1e-6
NEG = -1e30
N_LAYERS = 4
D_FF = 4 * D_MODEL
A_GROUPS = ((128, 1, 6), (512, 4, 5), (2048, 16, 5))
A_PAD_HEADS = 6
A_COLS = A_PAD_HEADS * HEAD_DIM
A_WIN = 256
A_SIDE = 64
A_UNROLL = 8
NUM_BUCKETS = 32
REL_MAX_DISTANCE = 1024
B_EBLOCKS = 13
B_ECLIP = 6
C_KV_HEADS = 4
GRID_W = 64
ROPE_THETA = 10000.0
ROPE_AXIS_DIM = HEAD_DIM // 2

LOG2E = math.log2(math.e)
Q_SCALE = HEAD_DIM ** -0.5
Q_SCALE_LOG2 = Q_SCALE * LOG2E

ONES_ROWS = 16

VMEM_LIMIT = 56 * 1024 * 1024


def _cparams(sem):
    return pltpu.CompilerParams(dimension_semantics=sem, vmem_limit_bytes=VMEM_LIMIT)


def _lambda_init(layer_idx):
    return 0.8 - 0.6 * math.exp(-0.3 * layer_idx)


def _rms(x, g):
    ms = jnp.mean(x * x, axis=-1, keepdims=True)
    return x * lax.rsqrt(ms + EPS) * g


def _norm_proj_kernel(x_ref, g_ref, w_ref, cs_ref, o_ref, *, chunk):
    xn = _rms(x_ref[...], g_ref[...]).astype(BF16)
    n = o_ref.shape[-1]
    for c in range(0, n, chunk):
        y = jnp.dot(xn, w_ref[:, c:c + chunk], preferred_element_type=F32)
        o_ref[:, c:c + chunk] = (y * cs_ref[:, c:c + chunk]).astype(o_ref.dtype)


def _norm_proj(x, g, w, col_scale, *, tm=512, chunk=None):
    t, d = x.shape
    n = w.shape[1]
    chunk = chunk or n
    assert t % tm == 0 and n % chunk == 0 and chunk % LANES == 0
    return pl.pallas_call(
        functools.partial(_norm_proj_kernel, chunk=chunk),
        grid=(t // tm,),
        in_specs=[
            pl.BlockSpec((tm, d), lambda i: (i, 0)),
            pl.BlockSpec((1, d), lambda i: (0, 0)),
            pl.BlockSpec((d, n), lambda i: (0, 0)),
            pl.BlockSpec((1, n), lambda i: (0, 0)),
        ],
        out_specs=pl.BlockSpec((tm, n), lambda i: (i, 0)),
        out_shape=jax.ShapeDtypeStruct((t, n), BF16),
        compiler_params=_cparams(("parallel",)),
        name="norm_proj",
    )(x, g, w, col_scale)


def _norm_proj_groups_kernel(x_ref, g_ref, w_ref, cs_ref, o0_ref, o1_ref, o2_ref, y_scr):
    xn = _rms(x_ref[...], g_ref[...]).astype(BF16)
    tm = x_ref.shape[0]
    n = 3 * A_COLS
    for gi, o_ref in enumerate((o0_ref, o1_ref, o2_ref)):
        dil = A_GROUPS[gi][1]
        cols = slice(gi * n, (gi + 1) * n)
        y = jnp.dot(xn, w_ref[:, cols], preferred_element_type=F32) * cs_ref[:, cols]
        if dil == 1:
            o_ref[0] = y.astype(o_ref.dtype)
        else:
            for cb in range(n // LANES):
                y_scr[cb] = y[:, cb * LANES:(cb + 1) * LANES]
            for r in range(dil):
                for cb in range(n // LANES):
                    o_ref[r, :, cb * LANES:(cb + 1) * LANES] = (
                        y_scr[cb, pl.ds(r, tm // dil, stride=dil), :].astype(o_ref.dtype))


def _norm_proj_groups(x, g, w, col_scale, *, batch, seq, tm=512):
    t, d = x.shape
    n = 3 * A_COLS
    spb = seq // tm
    assert seq % tm == 0 and all(tm % (16 * dil) == 0 for (_, dil, _) in A_GROUPS)
    const = lambda i: (0, 0)
    out_specs, out_shape = [], []
    for (_, dil, _) in A_GROUPS:
        out_specs.append(pl.BlockSpec((None, dil, tm // dil, n),
                                      lambda i: (i // spb, 0, i % spb, 0)))
        out_shape.append(jax.ShapeDtypeStruct((batch, dil, seq // dil, n), BF16))
    return pl.pallas_call(
        _norm_proj_groups_kernel,
        grid=(t // tm,),
        in_specs=[
            pl.BlockSpec((tm, d), lambda i: (i, 0)),
            pl.BlockSpec((1, d), const),
            pl.BlockSpec((d, 3 * n), const),
            pl.BlockSpec((1, 3 * n), const),
        ],
        out_specs=out_specs,
        out_shape=out_shape,
        scratch_shapes=[pltpu.VMEM((n // LANES, tm, LANES), F32)],
        compiler_params=_cparams(("parallel",)),
        name="norm_proj_groups",
    )(x, g, w, col_scale)


def _head_norm_rope(y, gain, cos, sin_signed, lane, head_ones):
    y2 = y * y
    hi = y2.astype(BF16)
    lo = (y2 - hi.astype(F32)).astype(BF16)
    ss = (jnp.dot(hi, head_ones, preferred_element_type=F32)
          + jnp.dot(lo, head_ones, preferred_element_type=F32))
    yn = y * lax.rsqrt(ss / HEAD_DIM + EPS) * gain
    first = (lane & (ROPE_AXIS_DIM - 1)) < ROPE_AXIS_DIM // 2
    partner = jnp.where(first, pltpu.roll(yn, LANES - ROPE_AXIS_DIM // 2, 1),
                        pltpu.roll(yn, ROPE_AXIS_DIM // 2, 1))
    return yn * cos + partner * sin_signed


def _norm_proj_rope_kernel(x_ref, g_ref, w_ref, qg_ref, kg_ref, cos_ref, sin_ref, o_ref,
                           *, n_q, n_k, chunk):
    xn = _rms(x_ref[...], g_ref[...]).astype(BF16)
    n = o_ref.shape[-1]
    tm = x_ref.shape[0]
    lane = lax.broadcasted_iota(jnp.int32, (tm, LANES), 1)
    row_head = lax.broadcasted_iota(jnp.int32, (LANES, LANES), 0) // HEAD_DIM
    col_head = lax.broadcasted_iota(jnp.int32, (LANES, LANES), 1) // HEAD_DIM
    head_ones = (row_head == col_head).astype(BF16)
    cos = cos_ref[...]
    sin = sin_ref[...]
    for c in range(0, n, chunk):
        y = jnp.dot(xn, w_ref[:, c:c + chunk], preferred_element_type=F32)
        for b in range(0, chunk, LANES):
            col = c + b
            blk = y[:, b:b + LANES]
            if col < n_q:
                blk = _head_norm_rope(blk, qg_ref[...], cos, sin, lane, head_ones) * Q_SCALE_LOG2
            elif col < n_q + n_k:
                blk = _head_norm_rope(blk, kg_ref[...], cos, sin, lane, head_ones)
            o_ref[:, col:col + LANES] = blk.astype(o_ref.dtype)


def _norm_proj_rope(x, g, w, q_gain, k_gain, cos, sin, *, n_q, n_k, seq, tm=512, chunk=512):
    t, d = x.shape
    n = w.shape[1]
    sblk = seq // tm
    assert seq % tm == 0 and n % chunk == 0 and n_q % LANES == 0 and n_k % LANES == 0
    return pl.pallas_call(
        functools.partial(_norm_proj_rope_kernel, n_q=n_q, n_k=n_k, chunk=chunk),
        grid=(t // tm,),
        in_specs=[
            pl.BlockSpec((tm, d), lambda i: (i, 0)),
            pl.BlockSpec((1, d), lambda i: (0, 0)),
            pl.BlockSpec((d, n), lambda i: (0, 0)),
            pl.BlockSpec((1, LANES), lambda i: (0, 0)),
            pl.BlockSpec((1, LANES), lambda i: (0, 0)),
            pl.BlockSpec((tm, LANES), lambda i: (i % sblk, 0)),
            pl.BlockSpec((tm, LANES), lambda i: (i % sblk, 0)),
        ],
        out_specs=pl.BlockSpec((tm, n), lambda i: (i, 0)),
        out_shape=jax.ShapeDtypeStruct((t, n), BF16),
        compiler_params=_cparams(("parallel",)),
        name="norm_proj_rope",
    )(x, g, w, q_gain, k_gain, cos, sin)


def _pair_attn_kernel(*refs, tq, ts, tk, n_kt, mode, lambda_init):
    vt_ref = refs[-1]
    if mode == "diff":
        q_ref, k_ref, v_ref, gb_ref, lam_ref, sg_ref, o_ref = refs[:-1]
    else:
        q_ref, k_ref, v_ref, o_ref = refs[:-1]
    qi = pl.program_id(2)
    seq = k_ref.shape[0]

    n_v = HEAD_DIM if mode == "gqa" else LANES
    kv_half = (pl.program_id(1) >> 1) & 1

    @pl.when(qi == 0)
    def _():
        for c in range(0, seq, tk):
            vt = v_ref[c:c + tk, :].astype(F32).T
            if mode == "gqa":
                vt = jnp.where(kv_half == 1, vt[HEAD_DIM:], vt[:HEAD_DIM])
            vt_ref[:n_v, c:c + tk] = vt.astype(BF16)
        vt_ref[n_v:, :] = jnp.ones((ONES_ROWS, seq), BF16)

    n_st = tq // ts
    lane = lax.broadcasted_iota(jnp.int32, (ts, LANES), 1)
    lo = lane < HEAD_DIM
    qs_st = []
    for st in range(n_st):
        q = q_ref[st * ts:(st + 1) * ts, :]
        zero = jnp.zeros_like(q)
        if mode == "gqa":
            q_sw = pltpu.roll(q.astype(F32), HEAD_DIM, 1).astype(q.dtype)
            in_half = lo == (kv_half == 0)
            h0 = jnp.where(kv_half == 0, q, q_sw)
            h1 = jnp.where(kv_half == 0, q_sw, q)
            qs = jnp.concatenate([jnp.where(in_half, h0, zero), jnp.where(in_half, h1, zero)],
                                 axis=0)
        else:
            qs = jnp.concatenate([jnp.where(lo, q, zero), jnp.where(lo, zero, q)], axis=0)
        qs_st.append(qs)
    rb_n = ts // LANES
    cb_n = tk // LANES

    def logits(kt, st):
        k = k_ref[kt * tk:(kt + 1) * tk, :]
        s = lax.dot_general(k, qs_st[st], (((1,), (1,)), ((), ())),
                            preferred_element_type=F32)
        if mode == "diff":
            base = kt * cb_n - (qi * n_st + st) * rb_n
            rows = []
            for cb in range(cb_n):
                blocks = []
                for j in range(2):
                    for rb in range(rb_n):
                        e = jnp.clip(base + (cb - rb), -B_ECLIP, B_ECLIP) + B_ECLIP
                        blocks.append(gb_ref[j, e])
                rows.append(jnp.concatenate(blocks, axis=1))
            s = s + jnp.concatenate(rows, axis=0)
        return s

    m = [jnp.full((1, 2 * ts), NEG, F32)] * n_st
    acc = [jnp.zeros((n_v + ONES_ROWS, 2 * ts), F32)] * n_st
    s_tile = {}

    def qk(kt, st):
        if kt < n_kt:
            s_tile[kt, st] = logits(kt, st)

    def softmax_pv(kt, st):
        if kt >= n_kt:
            return
        s = s_tile.pop((kt, st))
        m_new = jnp.maximum(m[st], jnp.max(s, axis=0, keepdims=True))
        alpha = jnp.exp2(m[st] - m_new)
        p = jnp.exp2(s - m_new).astype(BF16)
        vt = vt_ref[:, kt * tk:(kt + 1) * tk]
        acc[st] = alpha * acc[st] + jnp.dot(vt, p, preferred_element_type=F32)
        m[st] = m_new

    for st in range(n_st):
        qk(0, st)
    softmax_pv(0, 0)
    for kt in range(n_kt):
        for st in range(n_st):
            qk(kt + 1, st)
            if st + 1 < n_st:
                softmax_pv(kt, st + 1)
            else:
                softmax_pv(kt + 1, 0)
    if mode == "diff":
        lam = lam_ref[...]
        lam_full = (jnp.exp(jnp.sum(lam[0:1] * lam[1:2], axis=-1, keepdims=True))
                    - jnp.exp(jnp.sum(lam[2:3] * lam[3:4], axis=-1, keepdims=True))
                    + lambda_init)
    for st in range(n_st):
        o = acc[st][:n_v] / acc[st][n_v:n_v + 1]
        if mode == "diff":
            a = o[:, :ts].T - lam_full * o[:, ts:].T
            y = _rms(a, sg_ref[...]) * (1.0 - lambda_init)
        else:
            y = jnp.concatenate([o[:, :ts], o[:, ts:]], axis=0).T
        o_ref[st * ts:(st + 1) * ts, :] = y.astype(o_ref.dtype)


def _pair_attn(qkv, *, n_pairs, k_block, v_block, mode, gb=None, lam=None, subln=None,
               lambda_init=0.0, tq=1024, ts=256, tk=512):
    b, s, _ = qkv.shape
    assert s % tq == 0 and tq % ts == 0 and s % tk == 0 and ts % LANES == 0 and tk % LANES == 0
    in_specs = [
        pl.BlockSpec((None, tq, LANES), lambda bi, p, qi: (bi, qi, p)),
        pl.BlockSpec((None, s, LANES), lambda bi, p, qi: (bi, 0, k_block(p))),
        pl.BlockSpec((None, s, LANES), lambda bi, p, qi: (bi, 0, v_block(p))),
    ]
    args = [qkv, qkv, qkv]
    if mode == "diff":
        in_specs += [
            pl.BlockSpec((None, 2, B_EBLOCKS, LANES, LANES), lambda bi, p, qi: (p, 0, 0, 0, 0)),
            pl.BlockSpec((4, HEAD_DIM), lambda bi, p, qi: (0, 0)),
            pl.BlockSpec((1, LANES), lambda bi, p, qi: (0, 0)),
        ]
        args += [gb, lam, subln]
    return pl.pallas_call(
        functools.partial(_pair_attn_kernel, tq=tq, ts=ts, tk=tk, n_kt=s // tk, mode=mode,
                          lambda_init=lambda_init),
        grid=(b, n_pairs, s // tq),
        in_specs=in_specs,
        out_specs=pl.BlockSpec((None, tq, LANES), lambda bi, p, qi: (bi, qi, p)),
        out_shape=jax.ShapeDtypeStruct((b, s, n_pairs * LANES), BF16),
        scratch_shapes=[pltpu.VMEM(((HEAD_DIM if mode == "gqa" else LANES) + ONES_ROWS, s),
                                   BF16)],
        compiler_params=_cparams(("parallel", "parallel", "arbitrary")),
        name="pair_attn_" + mode,
    )(*args)


def _window_attn_kernel(slab_ref, ga_ref, o_ref, lse_ref, *, sub_len):
    lane = lax.broadcasted_iota(jnp.int32, (LANES, LANES), 1)
    lo = lane < HEAD_DIM
    n_p = A_COLS // LANES

    def sub_block(res, sb):
        static = isinstance(sb, int)
        q0 = sb * LANES
        if static:
            start = min(max(q0 - A_SIDE, 0), sub_len - A_WIN)
            variant = 1 if q0 == 0 else (2 if q0 == sub_len - LANES else 0)
            qrows, wrows = slice(q0, q0 + LANES), slice(start, start + A_WIN)
        else:
            q0 = pl.multiple_of(q0, LANES)
            start = pl.multiple_of(jnp.clip(q0 - A_SIDE, 0, sub_len - A_WIN), A_SIDE)
            variant = jnp.where(q0 == 0, 1, jnp.where(q0 == sub_len - LANES, 2, 0))
            qrows, wrows = pl.ds(q0, LANES), pl.ds(start, A_WIN)
        lse_all = jnp.full((LANES, LANES), NEG, F32)
        for p in range(n_p):
            cols = slice(p * LANES, (p + 1) * LANES)
            q = slab_ref[res, qrows, cols]
            kw = slab_ref[res, wrows, A_COLS + p * LANES:A_COLS + (p + 1) * LANES]
            vw = slab_ref[res, wrows, 2 * A_COLS + p * LANES:2 * A_COLS + (p + 1) * LANES]
            zero = jnp.zeros_like(q)
            qs = jnp.concatenate([jnp.where(lo, q, zero), jnp.where(lo, zero, q)], axis=0)
            s = lax.dot_general(qs, kw, (((1,), (1,)), ((), ())), preferred_element_type=F32)
            s = s + jnp.concatenate([ga_ref[p, 0, variant], ga_ref[p, 1, variant]], axis=0)
            m = jnp.max(s, axis=-1, keepdims=True)
            e = jnp.exp(s - m)
            l = jnp.sum(e, axis=-1, keepdims=True)
            acc = jnp.dot(e.astype(BF16), vw, preferred_element_type=F32)
            o = acc / l
            lse = m + jnp.log(l)
            o_ref[res, qrows, cols] = jnp.where(lo, o[:LANES], o[LANES:]).astype(o_ref.dtype)
            lse_all = jnp.where(lane == 2 * p, lse[:LANES],
                                jnp.where(lane == 2 * p + 1, lse[LANES:], lse_all))
        lse_ref[res, qrows, :] = lse_all

    n_res = slab_ref.shape[0]
    n_sb = sub_len // LANES
    if n_res * n_sb <= A_UNROLL:
        for res in range(n_res):
            for sb in range(n_sb):
                sub_block(res, sb)
    else:
        def body(i, carry):
            for u in range(A_UNROLL):
                sub_block(0, A_UNROLL * i + u)
            return carry
        lax.fori_loop(0, n_sb // A_UNROLL, body, 0)


def _window_attn(slab, ga):
    b, dil, sub_len, n = slab.shape
    n_sb = sub_len // LANES
    n_res = max(1, min(dil, A_UNROLL // n_sb))
    assert n_res == 1 or n_res * n_sb <= A_UNROLL
    assert n_res > 1 or n_sb <= A_UNROLL or n_sb % A_UNROLL == 0
    out_spec = pl.BlockSpec((None, n_res, sub_len, A_COLS), lambda bi, r: (bi, r, 0, 0))
    lse_spec = pl.BlockSpec((None, n_res, sub_len, LANES), lambda bi, r: (bi, r, 0, 0))
    return pl.pallas_call(
        functools.partial(_window_attn_kernel, sub_len=sub_len),
        grid=(b, dil // n_res),
        in_specs=[
            pl.BlockSpec((None, n_res, sub_len, n), lambda bi, r: (bi, r, 0, 0)),
            pl.BlockSpec(ga.shape, lambda bi, r: (0,) * ga.ndim),
        ],
        out_specs=[out_spec, lse_spec],
        out_shape=[jax.ShapeDtypeStruct((b, dil, sub_len, A_COLS), BF16),
                   jax.ShapeDtypeStruct((b, dil, sub_len, LANES), F32)],
        compiler_params=_cparams(("parallel", "parallel")),
        name="window_attn",
    )(slab, ga)


def _post_kernel(*refs, mix_groups, final_norm, ff_chunk):
    refs = list(refs)
    h_ref = refs.pop(0)
    if mix_groups:
        og_refs = [refs.pop(0) for _ in A_GROUPS]
        lg_refs = [refs.pop(0) for _ in A_GROUPS]
    else:
        o_ref = refs.pop(0)
    wo_ref, g_ref, win_ref, wout_ref = refs[:4]
    refs = refs[4:]
    gf_ref = refs.pop(0) if final_norm else None
    out_ref = refs.pop(0)

    if mix_groups:
        o_scr, lse_scr = refs
        n_cb, tm, _ = o_scr.shape
        cb_per_group = A_COLS // LANES
        for gi, (_, dil, _) in enumerate(A_GROUPS):
            for r in range(dil):
                rows = pl.ds(r, tm // dil, stride=dil) if dil > 1 else slice(None)
                for cb in range(cb_per_group):
                    cols = slice(cb * LANES, (cb + 1) * LANES)
                    o_scr[gi * cb_per_group + cb, rows, :] = og_refs[gi][r, :, cols].astype(F32)
                lse_scr[gi, rows, :] = lg_refs[gi][r]
        lane = lax.broadcasted_iota(jnp.int32, (tm, LANES), 1)
        real = [lane < nh for (_, _, nh) in A_GROUPS]
        lse = [jnp.where(real[gi], lse_scr[gi], NEG) for gi in range(len(A_GROUPS))]
        mx = jnp.max(jnp.maximum(jnp.maximum(lse[0], lse[1]), lse[2]), axis=-1, keepdims=True)
        s_g = [jnp.sum(jnp.where(real[gi], jnp.exp(lse[gi] - mx), 0.0), axis=-1, keepdims=True)
               / A_GROUPS[gi][2] for gi in range(len(A_GROUPS))]
        tot = s_g[0] + s_g[1] + s_g[2]
        parts = []
        for gi in range(len(A_GROUPS)):
            scale = len(A_GROUPS) * (s_g[gi] / tot)
            parts += [(o_scr[gi * cb_per_group + cb] * scale).astype(BF16)
                      for cb in range(cb_per_group)]
        o = jnp.concatenate(parts, axis=1)
    else:
        o = o_ref[...]
    h1 = h_ref[...] + jnp.dot(o, wo_ref[...], preferred_element_type=F32)
    xn = _rms(h1, g_ref[...]).astype(BF16)
    acc = h1
    for c in range(0, D_FF, ff_chunk):
        u = jnp.dot(xn, win_ref[:, c:c + ff_chunk], preferred_element_type=F32)
        u = jnp.maximum(u, 0.0)
        u = (u * u).astype(BF16)
        acc = acc + jnp.dot(u, wout_ref[c:c + ff_chunk, :], preferred_element_type=F32)
    if final_norm:
        acc = _rms(acc, gf_ref[...])
    out_ref[...] = acc


def _post(h, o, w_o, g_mlp, w_in, w_out, *, lse=None, g_final=None, seq=None, tm=512,
          ff_chunk=1024):
    t, d = h.shape
    assert t % tm == 0 and D_FF % ff_chunk == 0 and (lse is None or seq % tm == 0)
    const = lambda i: (0, 0)
    single = dict(pipeline_mode=pl.Buffered(1))
    in_specs = [pl.BlockSpec((tm, d), lambda i: (i, 0))]
    args = [h]
    scratch = []
    if lse is not None:
        n_o = len(A_GROUPS) * A_COLS
        spb = seq // tm
        for arrs in (o, lse):
            for arr, (_, dil, _) in zip(arrs, A_GROUPS):
                in_specs.append(pl.BlockSpec((None, dil, tm // dil, arr.shape[-1]),
                                             lambda i: (i // spb, 0, i % spb, 0)))
                args.append(arr)
        scratch = [pltpu.VMEM((n_o // LANES, tm, LANES), F32),
                   pltpu.VMEM((len(A_GROUPS), tm, LANES), F32)]
    else:
        n_o = o.shape[1]
        in_specs.append(pl.BlockSpec((tm, n_o), lambda i: (i, 0)))
        args.append(o)
    in_specs += [pl.BlockSpec((n_o, d), const, **single),
                 pl.BlockSpec((1, d), const),
                 pl.BlockSpec((d, D_FF), const, **single),
                 pl.BlockSpec((D_FF, d), const, **single)]
    args += [w_o, g_mlp, w_in, w_out]
    if g_final is not None:
        in_specs.append(pl.BlockSpec((1, d), const))
        args.append(g_final)
    return pl.pallas_call(
        functools.partial(_post_kernel, mix_groups=lse is not None,
                          final_norm=g_final is not None, ff_chunk=ff_chunk),
        grid=(t // tm,),
        in_specs=in_specs,
        out_specs=pl.BlockSpec((tm, d), lambda i: (i, 0)),
        out_shape=jax.ShapeDtypeStruct((t, d), F32),
        scratch_shapes=scratch,
        compiler_params=_cparams(("parallel",)),
        name="post",
    )(*args)


def _t5_bucket(rel):
    nb = NUM_BUCKETS // 2
    max_exact = nb // 2
    side = jnp.where(rel > 0, nb, 0)
    n = jnp.abs(rel)
    nf = jnp.maximum(n, 1).astype(F32)
    large = max_exact + (jnp.log(nf / max_exact) / math.log(REL_MAX_DISTANCE / max_exact)
                         * (nb - max_exact)).astype(jnp.int32)
    large = jnp.minimum(large, nb - 1)
    return side + jnp.where(n < max_exact, n, large)


def _table_lookup(bucket, tab_ref, col):
    vals = [tab_ref[b, col] for b in range(NUM_BUCKETS)]
    bit = 1
    while len(vals) > 1:
        odd = (bucket & bit) != 0
        vals = [jnp.where(odd, vals[i + 1], vals[i]) for i in range(0, len(vals), 2)]
        bit *= 2
    return vals[0]


def _diff_bias_kernel(tab_ref, o_ref):
    hj = pl.program_id(0)
    r = lax.broadcasted_iota(jnp.int32, (LANES, LANES), 0)
    c = lax.broadcasted_iota(jnp.int32, (LANES, LANES), 1)
    for e in range(B_EBLOCKS):
        rel = LANES * (e - B_ECLIP) + r - c
        o_ref[e] = _table_lookup(_t5_bucket(rel), tab_ref, hj) * LOG2E


def _diff_bias_blocks(rel_bias):
    n = rel_bias.shape[1]
    out = pl.pallas_call(
        _diff_bias_kernel,
        grid=(n,),
        in_specs=[pl.BlockSpec(memory_space=pltpu.SMEM)],
        out_specs=pl.BlockSpec((None, B_EBLOCKS, LANES, LANES), lambda i: (i, 0, 0, 0)),
        out_shape=jax.ShapeDtypeStruct((n, B_EBLOCKS, LANES, LANES), F32),
        compiler_params=_cparams(("parallel",)),
        name="diff_bias",
    )(rel_bias.astype(F32))
    return out.reshape(n // 2, 2, B_EBLOCKS, LANES, LANES)


A_WINDOW_OFFSETS = (-A_SIDE, 0, -LANES)


def _window_bias_kernel(tab_ref, o_ref):
    hp = pl.program_id(0)
    dil = jnp.where(hp < A_PAD_HEADS, A_GROUPS[0][1],
                    jnp.where(hp < 2 * A_PAD_HEADS, A_GROUPS[1][1], A_GROUPS[2][1]))
    r = lax.broadcasted_iota(jnp.int32, (LANES, A_WIN), 0)
    c = lax.broadcasted_iota(jnp.int32, (LANES, A_WIN), 1)
    for v, off in enumerate(A_WINDOW_OFFSETS):
        rel = c + off - r
        bias = _table_lookup(_t5_bucket(rel * dil), tab_ref, hp)
        o_ref[v] = jnp.where(jnp.abs(rel) <= A_SIDE, bias, NEG)


def _window_bias_tiles(rel_bias):
    cols = _pad_group_cols(rel_bias.astype(F32), 1, unit=1)
    table = jnp.concatenate(cols, axis=1)
    n = table.shape[1]
    out = pl.pallas_call(
        _window_bias_kernel,
        grid=(n,),
        in_specs=[pl.BlockSpec(memory_space=pltpu.SMEM)],
        out_specs=pl.BlockSpec((None, 3, LANES, A_WIN), lambda i: (i, 0, 0, 0)),
        out_shape=jax.ShapeDtypeStruct((n, 3, LANES, A_WIN), F32),
        compiler_params=_cparams(("parallel",)),
        name="window_bias",
    )(table)
    return out.reshape(len(A_GROUPS), A_PAD_HEADS // 2, 2, 3, LANES, A_WIN)


def _rope_tables(seq):
    n_rows = seq // GRID_W
    row = jnp.repeat(jnp.arange(n_rows, dtype=jnp.int32), GRID_W).astype(F32)
    col = jnp.tile(jnp.arange(GRID_W, dtype=jnp.int32), n_rows).astype(F32)
    half = ROPE_AXIS_DIM // 2
    inv = ROPE_THETA ** (-jnp.arange(half, dtype=F32) / half)
    ang = jnp.concatenate([row[:, None] * inv, col[:, None] * inv], axis=-1)
    lane = jnp.arange(LANES, dtype=jnp.int32) % HEAD_DIM
    idx = (lane // ROPE_AXIS_DIM) * half + lane % half
    sign = jnp.where((lane % ROPE_AXIS_DIM) < half, -1.0, 1.0).astype(F32)
    return jnp.cos(ang)[:, idx], jnp.sin(ang)[:, idx] * sign


def _pad_group_cols(w, axis, unit=HEAD_DIM):
    parts = []
    h0 = 0
    for (_, _, nh) in A_GROUPS:
        sl = [slice(None)] * w.ndim
        sl[axis] = slice(h0 * unit, (h0 + nh) * unit)
        part = w[tuple(sl)]
        if nh < A_PAD_HEADS:
            pad = [(0, 0)] * w.ndim
            pad[axis] = (0, (A_PAD_HEADS - nh) * unit)
            part = jnp.pad(part, pad)
        parts.append(part)
        h0 += nh
    return parts


def _dilated_layer(h, g_mix, w_qkv, w_o, bias_tiles, batch, seq):
    d_attn = w_qkv.shape[1] // 3
    wq, wk, wv = (w_qkv[:, i * d_attn:(i + 1) * d_attn] for i in range(3))
    qs, ks, vs = _pad_group_cols(wq, 1), _pad_group_cols(wk, 1), _pad_group_cols(wv, 1)
    w = jnp.concatenate([jnp.concatenate([qs[g], ks[g], vs[g]], axis=1) for g in range(3)],
                        axis=1).astype(BF16)
    scale = jnp.tile(jnp.concatenate([jnp.full((A_COLS,), Q_SCALE, F32),
                                      jnp.ones((2 * A_COLS,), F32)]), 3)[None, :]
    slabs = _norm_proj_groups(h, g_mix, w, scale, batch=batch, seq=seq)
    outs, lses = [], []
    for g in range(len(A_GROUPS)):
        o, lse = _window_attn(slabs[g], bias_tiles[g])
        outs.append(o)
        lses.append(lse)
    w_o_pad = jnp.concatenate(_pad_group_cols(w_o, 0), axis=0).astype(BF16)
    return outs, lses, w_o_pad


def kernel(x, rel_bias, norm_mix_g, norm_mlp_g, norm_final_g, a_w_qkv, a_w_o, b_w_qkv,
           b_lambda, b_subln_g, b_w_o, c_w_qkv, c_q_norm_g, c_k_norm_g, c_w_o, mlp_w_in,
           mlp_w_out):
    batch, seq, d = x.shape
    t = batch * seq
    h = x.reshape(t, d)
    cos, sin = _rope_tables(seq)
    window_bias = _window_bias_tiles(rel_bias)
    for i in range(N_LAYERS):
        kind, j = i % 3, i // 3
        g_mix = norm_mix_g[i][None, :]
        lse = None
        if kind == 0:
            o, lse, w_o = _dilated_layer(h, g_mix, a_w_qkv[j], a_w_o[j], window_bias, batch, seq)
        elif kind == 1:
            scale = jnp.concatenate([jnp.full((d,), Q_SCALE_LOG2, F32),
                                     jnp.ones((2 * d,), F32)])[None, :]
            qkv = _norm_proj(h, g_mix, b_w_qkv[j].astype(BF16), scale, chunk=1024)
            n_pairs = d // LANES
            o = _pair_attn(qkv.reshape(batch, seq, 3 * d), n_pairs=n_pairs,
                           k_block=lambda p: n_pairs + p, v_block=lambda p: 2 * n_pairs + p,
                           mode="diff", gb=_diff_bias_blocks(rel_bias), lam=b_lambda[j],
                           subln=b_subln_g[j][None, :], lambda_init=_lambda_init(i))
            o = o.reshape(t, d)
            w_o = b_w_o[j].astype(BF16)
        else:
            n_q = d
            n_kv = C_KV_HEADS * HEAD_DIM
            qkv = _norm_proj_rope(h, g_mix, c_w_qkv[j].astype(BF16),
                                  jnp.tile(c_q_norm_g[j], 2)[None, :],
                                  jnp.tile(c_k_norm_g[j], 2)[None, :], cos, sin,
                                  n_q=n_q, n_k=n_kv, seq=seq, chunk=512)
            n_pairs = n_q // LANES
            kv_blocks = n_kv // LANES
            o = _pair_attn(qkv.reshape(batch, seq, n_q + 2 * n_kv), n_pairs=n_pairs,
                           k_block=lambda p: n_pairs + p // 4,
                           v_block=lambda p: n_pairs + kv_blocks + p // 4,
                           mode="gqa", tq=2048, ts=512)
            o = o.reshape(t, d)
            w_o = c_w_o[j].astype(BF16)
        g_final = norm_final_g[None, :] if i == N_LAYERS - 1 else None
        h = _post(h, o, w_o, norm_mlp_g[i][None, :], mlp_w_in[i].astype(BF16),
                  mlp_w_out[i].astype(BF16), lse=lse, g_final=g_final, seq=seq)
    return h.reshape(batch, seq, d)
```

```python
import functools
import math

import jax
import jax.numpy as jnp
from jax import lax
from jax.experimental import pallas as pl
from jax.experimental.pallas import tpu as pltpu

F32 = jnp.float32
BF16 = jnp.bfloat16

D_MODEL = 1024
HEAD_DIM = 64
LANES = 128
EPS = 1e-6
NEG = -1e30
N_LAYERS = 4
D_FF = 4 * D_MODEL
A_GROUPS = ((128, 1, 6), (512, 4, 5), (2048, 16, 5))
A_PAD_HEADS = 6
A_COLS = A_PAD_HEADS * HEAD_DIM
A_WIN = 256
A_SIDE = 64
A_UNROLL = 8
NUM_BUCKETS = 32
REL_MAX_DISTANCE = 1024
B_EBLOCKS = 13
B_ECLIP = 6
C_KV_HEADS = 4
GRID_W = 64
ROPE_THETA = 10000.0
ROPE_AXIS_DIM = HEAD_DIM // 2

LOG2E = math.log2(math.e)
Q_SCALE = HEAD_DIM ** -0.5
Q_SCALE_LOG2 = Q_SCALE * LOG2E

ONES_ROWS = 16

VMEM_LIMIT = 56 * 1024 * 1024


def _cparams(sem):
    return pltpu.CompilerParams(dimension_semantics=sem, vmem_limit_bytes=VMEM_LIMIT)


def _lambda_init(layer_idx):
    return 0.8 - 0.6 * math.exp(-0.3 * layer_idx)


def _rms(x, g):
    ms = jnp.mean(x * x, axis=-1, keepdims=True)
    return x * lax.rsqrt(ms + EPS) * g


def _norm_proj_kernel(x_ref, g_ref, w_ref, cs_ref, o_ref, *, chunk):
    xn = _rms(x_ref[...], g_ref[...]).astype(BF16)
    n = o_ref.shape[-1]
    for c in range(0, n, chunk):
        y = jnp.dot(xn, w_ref[:, c:c + chunk], preferred_element_type=F32)
        o_ref[:, c:c + chunk] = (y * cs_ref[:, c:c + chunk]).astype(o_ref.dtype)


def _norm_proj(x, g, w, col_scale, *, tm=512, chunk=None):
    t, d = x.shape
    n = w.shape[1]
    chunk = chunk or n
    assert t % tm == 0 and n % chunk == 0 and chunk % LANES == 0
    return pl.pallas_call(
        functools.partial(_norm_proj_kernel, chunk=chunk),
        grid=(t // tm,),
        in_specs=[
            pl.BlockSpec((tm, d), lambda i: (i, 0)),
            pl.BlockSpec((1, d), lambda i: (0, 0)),
            pl.BlockSpec((d, n), lambda i: (0, 0)),
            pl.BlockSpec((1, n), lambda i: (0, 0)),
        ],
        out_specs=pl.BlockSpec((tm, n), lambda i: (i, 0)),
        out_shape=jax.ShapeDtypeStruct((t, n), BF16),
        compiler_params=_cparams(("parallel",)),
        name="norm_proj",
    )(x, g, w, col_scale)


def _norm_proj_groups_kernel(x_ref, g_ref, w_ref, cs_ref, o0_ref, o1_ref, o2_ref, y_scr):
    xn = _rms(x_ref[...], g_ref[...]).astype(BF16)
    tm = x_ref.shape[0]
    n = 3 * A_COLS
    y_all = jnp.dot(xn, w_ref[...], preferred_element_type=F32) * cs_ref[...]
    for gi, o_ref in enumerate((o0_ref, o1_ref, o2_ref)):
        dil = A_GROUPS[gi][1]
        y = y_all[:, gi * n:(gi + 1) * n]
        if dil == 1:
            o_ref[0] = y.astype(o_ref.dtype)
        else:
            for cb in range(n // LANES):
                y_scr[cb] = y[:, cb * LANES:(cb + 1) * LANES]
            for r in range(dil):
                for cb in range(n // LANES):
                    o_ref[r, :, cb * LANES:(cb + 1) * LANES] = (
                        y_scr[cb, pl.ds(r, tm // dil, stride=dil), :].astype(o_ref.dtype))


def _norm_proj_groups(x, g, w, col_scale, *, batch, seq, tm=512):
    t, d = x.shape
    n = 3 * A_COLS
    spb = seq // tm
    assert seq % tm == 0 and all(tm % (16 * dil) == 0 for (_, dil, _) in A_GROUPS)
    const = lambda i: (0, 0)
    out_specs, out_shape = [], []
    for (_, dil, _) in A_GROUPS:
        out_specs.append(pl.BlockSpec((None, dil, tm // dil, n),
                                      lambda i: (i // spb, 0, i % spb, 0)))
        out_shape.append(jax.ShapeDtypeStruct((batch, dil, seq // dil, n), BF16))
    return pl.pallas_call(
        _norm_proj_groups_kernel,
        grid=(t // tm,),
        in_specs=[
            pl.BlockSpec((tm, d), lambda i: (i, 0)),
            pl.BlockSpec((1, d), const),
            pl.BlockSpec((d, 3 * n), const),
            pl.BlockSpec((1, 3 * n), const),
        ],
        out_specs=out_specs,
        out_shape=out_shape,
        scratch_shapes=[pltpu.VMEM((n // LANES, tm, LANES), F32)],
        compiler_params=_cparams(("parallel",)),
        name="norm_proj_groups",
    )(x, g, w, col_scale)


def _head_norm_rope(y, gain, cos, sin_signed, lane, head_ones):
    y2 = y * y
    hi = y2.astype(BF16)
    lo = (y2 - hi.astype(F32)).astype(BF16)
    ss = (jnp.dot(hi, head_ones, preferred_element_type=F32)
          + jnp.dot(lo, head_ones, preferred_element_type=F32))
    yn = y * lax.rsqrt(ss / HEAD_DIM + EPS) * gain
    first = (lane & (ROPE_AXIS_DIM - 1)) < ROPE_AXIS_DIM // 2
    partner = jnp.where(first, pltpu.roll(yn, LANES - ROPE_AXIS_DIM // 2, 1),
                        pltpu.roll(yn, ROPE_AXIS_DIM // 2, 1))
    return yn * cos + partner * sin_signed


def _norm_proj_rope_kernel(x_ref, g_ref, w_ref, qg_ref, kg_ref, cos_ref, sin_ref, o_ref,
                           *, n_q, n_k, chunk):
    xn = _rms(x_ref[...], g_ref[...]).astype(BF16)
    n = o_ref.shape[-1]
    tm = x_ref.shape[0]
    lane = lax.broadcasted_iota(jnp.int32, (tm, LANES), 1)
    row_head = lax.broadcasted_iota(jnp.int32, (LANES, LANES), 0) // HEAD_DIM
    col_head = lax.broadcasted_iota(jnp.int32, (LANES, LANES), 1) // HEAD_DIM
    head_ones = (row_head == col_head).astype(BF16)
    cos = cos_ref[...]
    sin = sin_ref[...]
    for c in range(0, n, chunk):
        y = jnp.dot(xn, w_ref[:, c:c + chunk], preferred_element_type=F32)
        for b in range(0, chunk, LANES):
            col = c + b
            blk = y[:, b:b + LANES]
            if col < n_q:
                blk = _head_norm_rope(blk, qg_ref[...], cos, sin, lane, head_ones) * Q_SCALE_LOG2
            elif col < n_q + n_k:
                blk = _head_norm_rope(blk, kg_ref[...], cos, sin, lane, head_ones)
            o_ref[:, col:col + LANES] = blk.astype(o_ref.dtype)


def _norm_proj_rope(x, g, w, q_gain, k_gain, cos, sin, *, n_q, n_k, seq, tm=512, chunk=512):
    t, d = x.shape
    n = w.shape[1]
    sblk = seq // tm
    assert seq % tm == 0 and n % chunk == 0 and n_q % LANES == 0 and n_k % LANES == 0
    return pl.pallas_call(
        functools.partial(_norm_proj_rope_kernel, n_q=n_q, n_k=n_k, chunk=chunk),
        grid=(t // tm,),
        in_specs=[
            pl.BlockSpec((tm, d), lambda i: (i, 0)),
            pl.BlockSpec((1, d), lambda i: (0, 0)),
            pl.BlockSpec((d, n), lambda i: (0, 0)),
            pl.BlockSpec((1, LANES), lambda i: (0, 0)),
            pl.BlockSpec((1, LANES), lambda i: (0, 0)),
            pl.BlockSpec((tm, LANES), lambda i: (i % sblk, 0)),
            pl.BlockSpec((tm, LANES), lambda i: (i % sblk, 0)),
        ],
        out_specs=pl.BlockSpec((tm, n), lambda i: (i, 0)),
        out_shape=jax.ShapeDtypeStruct((t, n), BF16),
        compiler_params=_cparams(("parallel",)),
        name="norm_proj_rope",
    )(x, g, w, q_gain, k_gain, cos, sin)


def _pair_attn_kernel(*refs, tq, ts, tk, n_kt, mode, lambda_init):
    vt_ref = refs[-1]
    if mode == "diff":
        q_ref, k_ref, v_ref, gb_ref, lam_ref, sg_ref, o_ref = refs[:-1]
    else:
        q_ref, k_ref, v_ref, o_ref = refs[:-1]
    qi = pl.program_id(2)
    seq = k_ref.shape[0]

    n_v = HEAD_DIM if mode == "gqa" else LANES
    kv_half = (pl.program_id(1) >> 1) & 1

    @pl.when(qi == 0)
    def _():
        for c in range(0, seq, tk):
            vt = v_ref[c:c + tk, :].astype(F32).T
            if mode == "gqa":
                vt = jnp.where(kv_half == 1, vt[HEAD_DIM:], vt[:HEAD_DIM])
            vt_ref[:n_v, c:c + tk] = vt.astype(BF16)
        vt_ref[n_v:, :] = jnp.ones((ONES_ROWS, seq), BF16)

    n_st = tq // ts
    lane = lax.broadcasted_iota(jnp.int32, (ts, LANES), 1)
    lo = lane < HEAD_DIM
    qs_st = []
    for st in range(n_st):
        q = q_ref[st * ts:(st + 1) * ts, :]
        zero = jnp.zeros_like(q)
        if mode == "gqa":
            q_sw = pltpu.roll(q.astype(F32), HEAD_DIM, 1).astype(q.dtype)
            in_half = lo == (kv_half == 0)
            h0 = jnp.where(kv_half == 0, q, q_sw)
            h1 = jnp.where(kv_half == 0, q_sw, q)
            qs = jnp.concatenate([jnp.where(in_half, h0, zero), jnp.where(in_half, h1, zero)],
                                 axis=0)
        else:
            qs = jnp.concatenate([jnp.where(lo, q, zero), jnp.where(lo, zero, q)], axis=0)
        qs_st.append(qs)
    rb_n = ts // LANES
    cb_n = tk // LANES

    def logits(kt, st):
        k = k_ref[kt * tk:(kt + 1) * tk, :]
        s = lax.dot_general(k, qs_st[st], (((1,), (1,)), ((), ())),
                            preferred_element_type=F32)
        if mode == "diff":
            base = kt * cb_n - (qi * n_st + st) * rb_n
            rows = []
            for cb in range(cb_n):
                blocks = []
                for j in range(2):
                    for rb in range(rb_n):
                        e = jnp.clip(base + (cb - rb), -B_ECLIP, B_ECLIP) + B_ECLIP
                        blocks.append(gb_ref[j, e])
                rows.append(jnp.concatenate(blocks, axis=1))
            s = s + jnp.concatenate(rows, axis=0)
        return s

    m = [jnp.full((1, 2 * ts), NEG, F32)] * n_st
    acc = [jnp.zeros((n_v + ONES_ROWS, 2 * ts), F32)] * n_st
    s_tile = {}

    def qk(kt, st):
        if kt < n_kt:
            s_tile[kt, st] = logits(kt, st)

    def softmax_pv(kt, st):
        if kt >= n_kt:
            return
        s = s_tile.pop((kt, st))
        m_new = jnp.maximum(m[st], jnp.max(s, axis=0, keepdims=True))
        alpha = jnp.exp2(m[st] - m_new)
        p = jnp.exp2(s - m_new).astype(BF16)
        vt = vt_ref[:, kt * tk:(kt + 1) * tk]
        acc[st] = alpha * acc[st] + jnp.dot(vt, p, preferred_element_type=F32)
        m[st] = m_new

    for st in range(n_st):
        qk(0, st)
    softmax_pv(0, 0)
    for kt in range(n_kt):
        for st in range(n_st):
            qk(kt + 1, st)
            if st + 1 < n_st:
                softmax_pv(kt, st + 1)
            else:
                softmax_pv(kt + 1, 0)
    if mode == "diff":
        lam = lam_ref[...]
        lam_full = (jnp.exp(jnp.sum(lam[0:1] * lam[1:2], axis=-1, keepdims=True))
                    - jnp.exp(jnp.sum(lam[2:3] * lam[3:4], axis=-1, keepdims=True))
                    + lambda_init)
    for st in range(n_st):
        o = acc[st][:n_v] / acc[st][n_v:n_v + 1]
        if mode == "diff":
            a = o[:, :ts].T - lam_full * o[:, ts:].T
            y = _rms(a, sg_ref[...]) * (1.0 - lambda_init)
        else:
            y = jnp.concatenate([o[:, :ts], o[:, ts:]], axis=0).T
        o_ref[st * ts:(st + 1) * ts, :] = y.astype(o_ref.dtype)


def _pair_attn(qkv, *, n_pairs, k_block, v_block, mode, gb=None, lam=None, subln=None,
               lambda_init=0.0, tq=1024, ts=256, tk=512):
    b, s, _ = qkv.shape
    assert s % tq == 0 and tq % ts == 0 and s % tk == 0 and ts % LANES == 0 and tk % LANES == 0
    in_specs = [
        pl.BlockSpec((None, tq, LANES), lambda bi, p, qi: (bi, qi, p)),
        pl.BlockSpec((None, s, LANES), lambda bi, p, qi: (bi, 0, k_block(p))),
        pl.BlockSpec((None, s, LANES), lambda bi, p, qi: (bi, 0, v_block(p))),
    ]
    args = [qkv, qkv, qkv]
    if mode == "diff":
        in_specs += [
            pl.BlockSpec((None, 2, B_EBLOCKS, LANES, LANES), lambda bi, p, qi: (p, 0, 0, 0, 0)),
            pl.BlockSpec((4, HEAD_DIM), lambda bi, p, qi: (0, 0)),
            pl.BlockSpec((1, LANES), lambda bi, p, qi: (0, 0)),
        ]
        args += [gb, lam, subln]
    return pl.pallas_call(
        functools.partial(_pair_attn_kernel, tq=tq, ts=ts, tk=tk, n_kt=s // tk, mode=mode,
                          lambda_init=lambda_init),
        grid=(b, n_pairs, s // tq),
        in_specs=in_specs,
        out_specs=pl.BlockSpec((None, tq, LANES), lambda bi, p, qi: (bi, qi, p)),
        out_shape=jax.ShapeDtypeStruct((b, s, n_pairs * LANES), BF16),
        scratch_shapes=[pltpu.VMEM(((HEAD_DIM if mode == "gqa" else LANES) + ONES_ROWS, s),
                                   BF16)],
        compiler_params=_cparams(("parallel", "parallel", "arbitrary")),
        name="pair_attn_" + mode,
    )(*args)


def _window_attn_kernel(slab_ref, ga_ref, o_ref, lse_ref, *, sub_len):
    lane = lax.broadcasted_iota(jnp.int32, (LANES, LANES), 1)
    lo = lane < HEAD_DIM
    n_p = A_COLS // LANES

    def sub_block(res, sb):
        static = isinstance(sb, int)
        q0 = sb * LANES
        if static:
            start = min(max(q0 - A_SIDE, 0), sub_len - A_WIN)
            variant = 1 if q0 == 0 else (2 if q0 == sub_len - LANES else 0)
            qrows, wrows = slice(q0, q0 + LANES), slice(start, start + A_WIN)
        else:
            q0 = pl.multiple_of(q0, LANES)
            start = pl.multiple_of(jnp.clip(q0 - A_SIDE, 0, sub_len - A_WIN), A_SIDE)
            variant = jnp.where(q0 == 0, 1, jnp.where(q0 == sub_len - LANES, 2, 0))
            qrows, wrows = pl.ds(q0, LANES), pl.ds(start, A_WIN)
        lse_all = jnp.full((LANES, LANES), NEG, F32)
        for p in range(n_p):
            cols = slice(p * LANES, (p + 1) * LANES)
            q = slab_ref[res, qrows, cols]
            kw = slab_ref[res, wrows, A_COLS + p * LANES:A_COLS + (p + 1) * LANES]
            vw = slab_ref[res, wrows, 2 * A_COLS + p * LANES:2 * A_COLS + (p + 1) * LANES]
            zero = jnp.zeros_like(q)
            qs = jnp.concatenate([jnp.where(lo, q, zero), jnp.where(lo, zero, q)], axis=0)
            s = lax.dot_general(qs, kw, (((1,), (1,)), ((), ())), preferred_element_type=F32)
            s = s + jnp.concatenate([ga_ref[p, 0, variant], ga_ref[p, 1, variant]], axis=0)
            m = jnp.max(s, axis=-1, keepdims=True)
            e = jnp.exp(s - m)
            l = jnp.sum(e, axis=-1, keepdims=True)
            acc = jnp.dot(e.astype(BF16), vw, preferred_element_type=F32)
            o = acc / l
            lse = m + jnp.log(l)
            o_ref[res, qrows, cols] = jnp.where(lo, o[:LANES], o[LANES:]).astype(o_ref.dtype)
            lse_all = jnp.where(lane == 2 * p, lse[:LANES],
                                jnp.where(lane == 2 * p + 1, lse[LANES:], lse_all))
        lse_ref[res, qrows, :] = lse_all

    n_res = slab_ref.shape[0]
    n_sb = sub_len // LANES
    if n_res * n_sb <= A_UNROLL:
        for res in range(n_res):
            for sb in range(n_sb):
                sub_block(res, sb)
    else:
        def body(i, carry):
            for u in range(A_UNROLL):
                sub_block(0, A_UNROLL * i + u)
            return carry
        lax.fori_loop(0, n_sb // A_UNROLL, body, 0)


def _window_attn(slab, ga):
    b, dil, sub_len, n = slab.shape
    n_sb = sub_len // LANES
    n_res = max(1, min(dil, A_UNROLL // n_sb))
    assert n_res == 1 or n_res * n_sb <= A_UNROLL
    assert n_res > 1 or n_sb <= A_UNROLL or n_sb % A_UNROLL == 0
    out_spec = pl.BlockSpec((None, n_res, sub_len, A_COLS), lambda bi, r: (bi, r, 0, 0))
    lse_spec = pl.BlockSpec((None, n_res, sub_len, LANES), lambda bi, r: (bi, r, 0, 0))
    return pl.pallas_call(
        functools.partial(_window_attn_kernel, sub_len=sub_len),
        grid=(b, dil // n_res),
        in_specs=[
            pl.BlockSpec((None, n_res, sub_len, n), lambda bi, r: (bi, r, 0, 0)),
            pl.BlockSpec(ga.shape, lambda bi, r: (0,) * ga.ndim),
        ],
        out_specs=[out_spec, lse_spec],
        out_shape=[jax.ShapeDtypeStruct((b, dil, sub_len, A_COLS), BF16),
                   jax.ShapeDtypeStruct((b, dil, sub_len, LANES), F32)],
        compiler_params=_cparams(("parallel", "parallel")),
        name="window_attn",
    )(slab, ga)


def _post_kernel(*refs, mix_groups, final_norm, ff_chunk):
    refs = list(refs)
    h_ref = refs.pop(0)
    if mix_groups:
        og_refs = [refs.pop(0) for _ in A_GROUPS]
        lg_refs = [refs.pop(0) for _ in A_GROUPS]
    else:
        o_ref = refs.pop(0)
    wo_ref, g_ref, win_ref, wout_ref = refs[:4]
    refs = refs[4:]
    gf_ref = refs.pop(0) if final_norm else None
    out_ref = refs.pop(0)

    if mix_groups:
        o_scr, lse_scr = refs
        n_cb, tm, _ = o_scr.shape
        cb_per_group = A_COLS // LANES
        for gi, (_, dil, _) in enumerate(A_GROUPS):
            for r in range(dil):
                rows = pl.ds(r, tm // dil, stride=dil) if dil > 1 else slice(None)
                for cb in range(cb_per_group):
                    cols = slice(cb * LANES, (cb + 1) * LANES)
                    o_scr[gi * cb_per_group + cb, rows, :] = og_refs[gi][r, :, cols].astype(F32)
                lse_scr[gi, rows, :] = lg_refs[gi][r]
        lane = lax.broadcasted_iota(jnp.int32, (tm, LANES), 1)
        real = [lane < nh for (_, _, nh) in A_GROUPS]
        lse = [jnp.where(real[gi], lse_scr[gi], NEG) for gi in range(len(A_GROUPS))]
        mx = jnp.max(jnp.maximum(jnp.maximum(lse[0], lse[1]), lse[2]), axis=-1, keepdims=True)
        s_g = [jnp.sum(jnp.where(real[gi], jnp.exp(lse[gi] - mx), 0.0), axis=-1, keepdims=True)
               / A_GROUPS[gi][2] for gi in range(len(A_GROUPS))]
        tot = s_g[0] + s_g[1] + s_g[2]
        parts = []
        for gi in range(len(A_GROUPS)):
            scale = len(A_GROUPS) * (s_g[gi] / tot)
            parts += [(o_scr[gi * cb_per_group + cb] * scale).astype(BF16)
                      for cb in range(cb_per_group)]
        o = jnp.concatenate(parts, axis=1)
    else:
        o = o_ref[...]
    h1 = h_ref[...] + jnp.dot(o, wo_ref[...], preferred_element_type=F32)
    xn = _rms(h1, g_ref[...]).astype(BF16)
    acc = h1
    for c in range(0, D_FF, ff_chunk):
        u = jnp.dot(xn, win_ref[:, c:c + ff_chunk], preferred_element_type=F32)
        u = jnp.maximum(u, 0.0)
        u = (u * u).astype(BF16)
        acc = acc + jnp.dot(u, wout_ref[c:c + ff_chunk, :], preferred_element_type=F32)
    if final_norm:
        acc = _rms(acc, gf_ref[...])
    out_ref[...] = acc


def _post(h, o, w_o, g_mlp, w_in, w_out, *, lse=None, g_final=None, seq=None, tm=512,
          ff_chunk=1024):
    t, d = h.shape
    assert t % tm == 0 and D_FF % ff_chunk == 0 and (lse is None or seq % tm == 0)
    const = lambda i: (0, 0)
    single = dict(pipeline_mode=pl.Buffered(1))
    in_specs = [pl.BlockSpec((tm, d), lambda i: (i, 0))]
    args = [h]
    scratch = []
    if lse is not None:
        n_o = len(A_GROUPS) * A_COLS
        spb = seq // tm
        for arrs in (o, lse):
            for arr, (_, dil, _) in zip(arrs, A_GROUPS):
                in_specs.append(pl.BlockSpec((None, dil, tm // dil, arr.shape[-1]),
                                             lambda i: (i // spb, 0, i % spb, 0)))
                args.append(arr)
        scratch = [pltpu.VMEM((n_o // LANES, tm, LANES), F32),
                   pltpu.VMEM((len(A_GROUPS), tm, LANES), F32)]
    else:
        n_o = o.shape[1]
        in_specs.append(pl.BlockSpec((tm, n_o), lambda i: (i, 0)))
        args.append(o)
    in_specs += [pl.BlockSpec((n_o, d), const, **single),
                 pl.BlockSpec((1, d), const),
                 pl.BlockSpec((d, D_FF), const, **single),
                 pl.BlockSpec((D_FF, d), const, **single)]
    args += [w_o, g_mlp, w_in, w_out]
    if g_final is not None:
        in_specs.append(pl.BlockSpec((1, d), const))
        args.append(g_final)
    return pl.pallas_call(
        functools.partial(_post_kernel, mix_groups=lse is not None,
                          final_norm=g_final is not None, ff_chunk=ff_chunk),
        grid=(t // tm,),
        in_specs=in_specs,
        out_specs=pl.BlockSpec((tm, d), lambda i: (i, 0)),
        out_shape=jax.ShapeDtypeStruct((t, d), F32),
        scratch_shapes=scratch,
        compiler_params=_cparams(("parallel",)),
        name="post",
    )(*args)


def _t5_bucket(rel):
    nb = NUM_BUCKETS // 2
    max_exact = nb // 2
    side = jnp.where(rel > 0, nb, 0)
    n = jnp.abs(rel)
    nf = jnp.maximum(n, 1).astype(F32)
    large = max_exact + (jnp.log(nf / max_exact) / math.log(REL_MAX_DISTANCE / max_exact)
                         * (nb - max_exact)).astype(jnp.int32)
    large = jnp.minimum(large, nb - 1)
    return side + jnp.where(n < max_exact, n, large)


def _table_lookup(bucket, tab_ref, col):
    vals = [tab_ref[b, col] for b in range(NUM_BUCKETS)]
    bit = 1
    while len(vals) > 1:
        odd = (bucket & bit) != 0
        vals = [jnp.where(odd, vals[i + 1], vals[i]) for i in range(0, len(vals), 2)]
        bit *= 2
    return vals[0]


def _diff_bias_kernel(tab_ref, o_ref):
    hj = pl.program_id(0)
    r = lax.broadcasted_iota(jnp.int32, (LANES, LANES), 0)
    c = lax.broadcasted_iota(jnp.int32, (LANES, LANES), 1)
    for e in range(B_EBLOCKS):
        rel = LANES * (e - B_ECLIP) + r - c
        o_ref[e] = _table_lookup(_t5_bucket(rel), tab_ref, hj) * LOG2E


def _diff_bias_blocks(rel_bias):
    n = rel_bias.shape[1]
    out = pl.pallas_call(
        _diff_bias_kernel,
        grid=(n,),
        in_specs=[pl.BlockSpec(memory_space=pltpu.SMEM)],
        out_specs=pl.BlockSpec((None, B_EBLOCKS, LANES, LANES), lambda i: (i, 0, 0, 0)),
        out_shape=jax.ShapeDtypeStruct((n, B_EBLOCKS, LANES, LANES), F32),
        compiler_params=_cparams(("parallel",)),
        name="diff_bias",
    )(rel_bias.astype(F32))
    return out.reshape(n // 2, 2, B_EBLOCKS, LANES, LANES)


A_WINDOW_OFFSETS = (-A_SIDE, 0, -LANES)


def _window_bias_kernel(tab_ref, o_ref):
    hp = pl.program_id(0)
    dil = jnp.where(hp < A_PAD_HEADS, A_GROUPS[0][1],
                    jnp.where(hp < 2 * A_PAD_HEADS, A_GROUPS[1][1], A_GROUPS[2][1]))
    r = lax.broadcasted_iota(jnp.int32, (LANES, A_WIN), 0)
    c = lax.broadcasted_iota(jnp.int32, (LANES, A_WIN), 1)
    for v, off in enumerate(A_WINDOW_OFFSETS):
        rel = c + off - r
        bias = _table_lookup(_t5_bucket(rel * dil), tab_ref, hp)
        o_ref[v] = jnp.where(jnp.abs(rel) <= A_SIDE, bias, NEG)


def _window_bias_tiles(rel_bias):
    cols = _pad_group_cols(rel_bias.astype(F32), 1, unit=1)
    table = jnp.concatenate(cols, axis=1)
    n = table.shape[1]
    out = pl.pallas_call(
        _window_bias_kernel,
        grid=(n,),
        in_specs=[pl.BlockSpec(memory_space=pltpu.SMEM)],
        out_specs=pl.BlockSpec((None, 3, LANES, A_WIN), lambda i: (i, 0, 0, 0)),
        out_shape=jax.ShapeDtypeStruct((n, 3, LANES, A_WIN), F32),
        compiler_params=_cparams(("parallel",)),
        name="window_bias",
    )(table)
    return out.reshape(len(A_GROUPS), A_PAD_HEADS // 2, 2, 3, LANES, A_WIN)


def _rope_tables(seq):
    n_rows = seq // GRID_W
    row = jnp.repeat(jnp.arange(n_rows, dtype=jnp.int32), GRID_W).astype(F32)
    col = jnp.tile(jnp.arange(GRID_W, dtype=jnp.int32), n_rows).astype(F32)
    half = ROPE_AXIS_DIM // 2
    inv = ROPE_THETA ** (-jnp.arange(half, dtype=F32) / half)
    ang = jnp.concatenate([row[:, None] * inv, col[:, None] * inv], axis=-1)
    lane = jnp.arange(LANES, dtype=jnp.int32) % HEAD_DIM
    idx = (lane // ROPE_AXIS_DIM) * half + lane % half
    sign = jnp.where((lane % ROPE_AXIS_DIM) < half, -1.0, 1.0).astype(F32)
    return jnp.cos(ang)[:, idx], jnp.sin(ang)[:, idx] * sign


def _pad_group_cols(w, axis, unit=HEAD_DIM):
    parts = []
    h0 = 0
    for (_, _, nh) in A_GROUPS:
        sl = [slice(None)] * w.ndim
        sl[axis] = slice(h0 * unit, (h0 + nh) * unit)
        part = w[tuple(sl)]
        if nh < A_PAD_HEADS:
            pad = [(0, 0)] * w.ndim
            pad[axis] = (0, (A_PAD_HEADS - nh) * unit)
            part = jnp.pad(part, pad)
        parts.append(part)
        h0 += nh
    return parts


def _dilated_layer(h, g_mix, w_qkv, w_o, bias_tiles, batch, seq):
    d_attn = w_qkv.shape[1] // 3
    w_qkv = w_qkv.astype(BF16)
    wq, wk, wv = (w_qkv[:, i * d_attn:(i + 1) * d_attn] for i in range(3))
    qs, ks, vs = _pad_group_cols(wq, 1), _pad_group_cols(wk, 1), _pad_group_cols(wv, 1)
    w = jnp.concatenate([jnp.concatenate([qs[g], ks[g], vs[g]], axis=1) for g in range(3)],
                        axis=1)
    scale = jnp.tile(jnp.concatenate([jnp.full((A_COLS,), Q_SCALE, F32),
                                      jnp.ones((2 * A_COLS,), F32)]), 3)[None, :]
    slabs = _norm_proj_groups(h, g_mix, w, scale, batch=batch, seq=seq)
    outs, lses = [], []
    for g in range(len(A_GROUPS)):
        o, lse = _window_attn(slabs[g], bias_tiles[g])
        outs.append(o)
        lses.append(lse)
    w_o_pad = jnp.concatenate(_pad_group_cols(w_o.astype(BF16), 0), axis=0)
    return outs, lses, w_o_pad


def kernel(x, rel_bias, norm_mix_g, norm_mlp_g, norm_final_g, a_w_qkv, a_w_o, b_w_qkv,
           b_lambda, b_subln_g, b_w_o, c_w_qkv, c_q_norm_g, c_k_norm_g, c_w_o, mlp_w_in,
           mlp_w_out):
    batch, seq, d = x.shape
    t = batch * seq
    h = x.reshape(t, d)
    cos, sin = _rope_tables(seq)
    window_bias = _window_bias_tiles(rel_bias)
    for i in range(N_LAYERS):
        kind, j = i % 3, i // 3
        g_mix = norm_mix_g[i][None, :]
        lse = None
        if kind == 0:
            o, lse, w_o = _dilated_layer(h, g_mix, a_w_qkv[j], a_w_o[j], window_bias, batch, seq)
        elif kind == 1:
            scale = jnp.concatenate([jnp.full((d,), Q_SCALE_LOG2, F32),
                                     jnp.ones((2 * d,), F32)])[None, :]
            qkv = _norm_proj(h, g_mix, b_w_qkv[j].astype(BF16), scale, chunk=1024)
            n_pairs = d // LANES
            o = _pair_attn(qkv.reshape(batch, seq, 3 * d), n_pairs=n_pairs,
                           k_block=lambda p: n_pairs + p, v_block=lambda p: 2 * n_pairs + p,
                           mode="diff", gb=_diff_bias_blocks(rel_bias), lam=b_lambda[j],
                           subln=b_subln_g[j][None, :], lambda_init=_lambda_init(i))
            o = o.reshape(t, d)
            w_o = b_w_o[j].astype(BF16)
        else:
            n_q = d
            n_kv = C_KV_HEADS * HEAD_DIM
            qkv = _norm_proj_rope(h, g_mix, c_w_qkv[j].astype(BF16),
                                  jnp.tile(c_q_norm_g[j], 2)[None, :],
                                  jnp.tile(c_k_norm_g[j], 2)[None, :], cos, sin,
                                  n_q=n_q, n_k=n_kv, seq=seq, chunk=512)
            n_pairs = n_q // LANES
            kv_blocks = n_kv // LANES
            o = _pair_attn(qkv.reshape(batch, seq, n_q + 2 * n_kv), n_pairs=n_pairs,
                           k_block=lambda p: n_pairs + p // 4,
                           v_block=lambda p: n_pairs + kv_blocks + p // 4,
                           mode="gqa", tq=2048, ts=512)
            o = o.reshape(t, d)
            w_o = c_w_o[j].astype(BF16)
        g_final = norm_final_g[None, :] if i == N_LAYERS - 1 else None
        h = _post(h, o, w_o, norm_mlp_g[i][None, :], mlp_w_in[i].astype(BF16),
                  mlp_w_out[i].astype(BF16), lse=lse, g_final=g_final, seq=seq)
    return h.reshape(batch, seq, d)
```

```python
import functools
import math

import jax
import jax.numpy as jnp
from jax import lax
from jax.experimental import pallas as pl
from jax.experimental.pallas import tpu as pltpu

F32 = jnp.float32
BF16 = jnp.bfloat16

D_MODEL = 1024
HEAD_DIM = 64
LANES = 128
EPS = 1e-6
NEG = -1e30
N_LAYERS = 4
D_FF = 4 * D_MODEL
A_GROUPS = ((128, 1, 6), (512, 4, 5), (2048, 16, 5))
A_PAD_HEADS = 6
A_COLS = A_PAD_HEADS * HEAD_DIM
A_WIN = 256
A_SIDE = 64
A_UNROLL = 8
NUM_BUCKETS = 32
REL_MAX_DISTANCE = 1024
B_EBLOCKS = 13
B_ECLIP = 6
C_KV_HEADS = 4
GRID_W = 64
ROPE_THETA = 10000.0
ROPE_AXIS_DIM = HEAD_DIM // 2

LOG2E = math.log2(math.e)
Q_SCALE = HEAD_DIM ** -0.5
Q_SCALE_LOG2 = Q_SCALE * LOG2E

ONES_ROWS = 16

VMEM_LIMIT = 56 * 1024 * 1024


def _cparams(sem, fuse_inputs=None):
    return pltpu.CompilerParams(dimension_semantics=sem, vmem_limit_bytes=VMEM_LIMIT,
                                allow_input_fusion=fuse_inputs)


def _lambda_init(layer_idx):
    return 0.8 - 0.6 * math.exp(-0.3 * layer_idx)


def _rms(x, g):
    ms = jnp.mean(x * x, axis=-1, keepdims=True)
    return x * lax.rsqrt(ms + EPS) * g


def _norm_proj_kernel(x_ref, g_ref, w_ref, cs_ref, o_ref, *, chunk):
    xn = _rms(x_ref[...], g_ref[...]).astype(BF16)
    n = o_ref.shape[-1]
    for c in range(0, n, chunk):
        y = jnp.dot(xn, w_ref[:, c:c + chunk], preferred_element_type=F32)
        o_ref[:, c:c + chunk] = (y * cs_ref[:, c:c + chunk]).astype(o_ref.dtype)


def _norm_proj(x, g, w, col_scale, *, tm=512, chunk=None):
    t, d = x.shape
    n = w.shape[1]
    chunk = chunk or n
    assert t % tm == 0 and n % chunk == 0 and chunk % LANES == 0
    return pl.pallas_call(
        functools.partial(_norm_proj_kernel, chunk=chunk),
        grid=(t // tm,),
        in_specs=[
            pl.BlockSpec((tm, d), lambda i: (i, 0)),
            pl.BlockSpec((1, d), lambda i: (0, 0)),
            pl.BlockSpec((d, n), lambda i: (0, 0)),
            pl.BlockSpec((1, n), lambda i: (0, 0)),
        ],
        out_specs=pl.BlockSpec((tm, n), lambda i: (i, 0)),
        out_shape=jax.ShapeDtypeStruct((t, n), BF16),
        compiler_params=_cparams(("parallel",)),
        name="norm_proj",
    )(x, g, w, col_scale)


def _norm_proj_groups_kernel(x_ref, g_ref, w_ref, cs_ref, o0_ref, o1_ref, o2_ref, y_scr):
    xn = _rms(x_ref[...], g_ref[...]).astype(BF16)
    tm = x_ref.shape[0]
    n = 3 * A_COLS
    y_all = jnp.dot(xn, w_ref[...], preferred_element_type=F32) * cs_ref[...]
    for gi, o_ref in enumerate((o0_ref, o1_ref, o2_ref)):
        dil = A_GROUPS[gi][1]
        y = y_all[:, gi * n:(gi + 1) * n]
        if dil == 1:
            o_ref[0] = y.astype(o_ref.dtype)
        else:
            for cb in range(n // LANES):
                y_scr[cb] = y[:, cb * LANES:(cb + 1) * LANES]
            for r in range(dil):
                for cb in range(n // LANES):
                    o_ref[r, :, cb * LANES:(cb + 1) * LANES] = (
                        y_scr[cb, pl.ds(r, tm // dil, stride=dil), :].astype(o_ref.dtype))


def _norm_proj_groups(x, g, w, col_scale, *, batch, seq, tm=512):
    t, d = x.shape
    n = 3 * A_COLS
    spb = seq // tm
    assert seq % tm == 0 and all(tm % (16 * dil) == 0 for (_, dil, _) in A_GROUPS)
    const = lambda i: (0, 0)
    out_specs, out_shape = [], []
    for (_, dil, _) in A_GROUPS:
        out_specs.append(pl.BlockSpec((None, dil, tm // dil, n),
                                      lambda i: (i // spb, 0, i % spb, 0)))
        out_shape.append(jax.ShapeDtypeStruct((batch, dil, seq // dil, n), BF16))
    return pl.pallas_call(
        _norm_proj_groups_kernel,
        grid=(t // tm,),
        in_specs=[
            pl.BlockSpec((tm, d), lambda i: (i, 0)),
            pl.BlockSpec((1, d), const),
            pl.BlockSpec((d, 3 * n), const),
            pl.BlockSpec((1, 3 * n), const),
        ],
        out_specs=out_specs,
        out_shape=out_shape,
        scratch_shapes=[pltpu.VMEM((n // LANES, tm, LANES), F32)],
        compiler_params=_cparams(("parallel",)),
        name="norm_proj_groups",
    )(x, g, w, col_scale)


def _head_norm_rope(y, gain, cos, sin_signed, lane, head_ones):
    y2 = y * y
    hi = y2.astype(BF16)
    lo = (y2 - hi.astype(F32)).astype(BF16)
    ss = (jnp.dot(hi, head_ones, preferred_element_type=F32)
          + jnp.dot(lo, head_ones, preferred_element_type=F32))
    yn = y * lax.rsqrt(ss / HEAD_DIM + EPS) * gain
    first = (lane & (ROPE_AXIS_DIM - 1)) < ROPE_AXIS_DIM // 2
    partner = jnp.where(first, pltpu.roll(yn, LANES - ROPE_AXIS_DIM // 2, 1),
                        pltpu.roll(yn, ROPE_AXIS_DIM // 2, 1))
    return yn * cos + partner * sin_signed


def _norm_proj_rope_kernel(x_ref, g_ref, w_ref, qg_ref, kg_ref, cos_ref, sin_ref, o_ref,
                           *, n_q, n_k, chunk):
    xn = _rms(x_ref[...], g_ref[...]).astype(BF16)
    n = o_ref.shape[-1]
    tm = x_ref.shape[0]
    lane = lax.broadcasted_iota(jnp.int32, (tm, LANES), 1)
    row_head = lax.broadcasted_iota(jnp.int32, (LANES, LANES), 0) // HEAD_DIM
    col_head = lax.broadcasted_iota(jnp.int32, (LANES, LANES), 1) // HEAD_DIM
    head_ones = (row_head == col_head).astype(BF16)
    cos = cos_ref[...]
    sin = sin_ref[...]
    for c in range(0, n, chunk):
        y = jnp.dot(xn, w_ref[:, c:c + chunk], preferred_element_type=F32)
        for b in range(0, chunk, LANES):
            col = c + b
            blk = y[:, b:b + LANES]
            if col < n_q:
                blk = _head_norm_rope(blk, qg_ref[...], cos, sin, lane, head_ones) * Q_SCALE_LOG2
            elif col < n_q + n_k:
                blk = _head_norm_rope(blk, kg_ref[...], cos, sin, lane, head_ones)
            o_ref[:, col:col + LANES] = blk.astype(o_ref.dtype)


def _norm_proj_rope(x, g, w, q_gain, k_gain, cos, sin, *, n_q, n_k, seq, tm=512, chunk=512):
    t, d = x.shape
    n = w.shape[1]
    sblk = seq // tm
    assert seq % tm == 0 and n % chunk == 0 and n_q % LANES == 0 and n_k % LANES == 0
    return pl.pallas_call(
        functools.partial(_norm_proj_rope_kernel, n_q=n_q, n_k=n_k, chunk=chunk),
        grid=(t // tm,),
        in_specs=[
            pl.BlockSpec((tm, d), lambda i: (i, 0)),
            pl.BlockSpec((1, d), lambda i: (0, 0)),
            pl.BlockSpec((d, n), lambda i: (0, 0)),
            pl.BlockSpec((1, LANES), lambda i: (0, 0)),
            pl.BlockSpec((1, LANES), lambda i: (0, 0)),
            pl.BlockSpec((tm, LANES), lambda i: (i % sblk, 0)),
            pl.BlockSpec((tm, LANES), lambda i: (i % sblk, 0)),
        ],
        out_specs=pl.BlockSpec((tm, n), lambda i: (i, 0)),
        out_shape=jax.ShapeDtypeStruct((t, n), BF16),
        compiler_params=_cparams(("parallel",)),
        name="norm_proj_rope",
    )(x, g, w, q_gain, k_gain, cos, sin)


def _pair_attn_kernel(*refs, tq, ts, tk, n_kt, mode, lambda_init):
    vt_ref = refs[-1]
    if mode == "diff":
        q_ref, k_ref, v_ref, gb_ref, lam_ref, sg_ref, o_ref = refs[:-1]
    else:
        q_ref, k_ref, v_ref, o_ref = refs[:-1]
    qi = pl.program_id(2)
    seq = k_ref.shape[0]

    n_v = HEAD_DIM if mode == "gqa" else LANES
    kv_half = (pl.program_id(1) >> 1) & 1

    @pl.when(qi == 0)
    def _():
        for c in range(0, seq, tk):
            vt = v_ref[c:c + tk, :].astype(F32).T
            if mode == "gqa":
                vt = jnp.where(kv_half == 1, vt[HEAD_DIM:], vt[:HEAD_DIM])
            vt_ref[:n_v, c:c + tk] = vt.astype(BF16)
        vt_ref[n_v:, :] = jnp.ones((ONES_ROWS, seq), BF16)

    n_st = tq // ts
    lane = lax.broadcasted_iota(jnp.int32, (ts, LANES), 1)
    lo = lane < HEAD_DIM
    qs_st = []
    for st in range(n_st):
        q = q_ref[st * ts:(st + 1) * ts, :]
        zero = jnp.zeros_like(q)
        if mode == "gqa":
            q_sw = pltpu.roll(q.astype(F32), HEAD_DIM, 1).astype(q.dtype)
            in_half = lo == (kv_half == 0)
            h0 = jnp.where(kv_half == 0, q, q_sw)
            h1 = jnp.where(kv_half == 0, q_sw, q)
            qs = jnp.concatenate([jnp.where(in_half, h0, zero), jnp.where(in_half, h1, zero)],
                                 axis=0)
        else:
            qs = jnp.concatenate([jnp.where(lo, q, zero), jnp.where(lo, zero, q)], axis=0)
        qs_st.append(qs)
    rb_n = ts // LANES
    cb_n = tk // LANES

    def logits(kt, st):
        k = k_ref[kt * tk:(kt + 1) * tk, :]
        s = lax.dot_general(k, qs_st[st], (((1,), (1,)), ((), ())),
                            preferred_element_type=F32)
        if mode == "diff":
            base = kt * cb_n - (qi * n_st + st) * rb_n
            rows = []
            for cb in range(cb_n):
                blocks = []
                for j in range(2):
                    for rb in range(rb_n):
                        e = jnp.clip(base + (cb - rb), -B_ECLIP, B_ECLIP) + B_ECLIP
                        blocks.append(gb_ref[j, e])
                rows.append(jnp.concatenate(blocks, axis=1))
            s = s + jnp.concatenate(rows, axis=0)
        return s

    m = [jnp.full((1, 2 * ts), NEG, F32)] * n_st
    acc = [jnp.zeros((n_v + ONES_ROWS, 2 * ts), F32)] * n_st
    s_tile = {}

    def qk(kt, st):
        if kt < n_kt:
            s_tile[kt, st] = logits(kt, st)

    def softmax_pv(kt, st):
        if kt >= n_kt:
            return
        s = s_tile.pop((kt, st))
        m_new = jnp.maximum(m[st], jnp.max(s, axis=0, keepdims=True))
        alpha = jnp.exp2(m[st] - m_new)
        p = jnp.exp2(s - m_new).astype(BF16)
        vt = vt_ref[:, kt * tk:(kt + 1) * tk]
        acc[st] = alpha * acc[st] + jnp.dot(vt, p, preferred_element_type=F32)
        m[st] = m_new

    for st in range(n_st):
        qk(0, st)
    softmax_pv(0, 0)
    for kt in range(n_kt):
        for st in range(n_st):
            qk(kt + 1, st)
            if st + 1 < n_st:
                softmax_pv(kt, st + 1)
            else:
                softmax_pv(kt + 1, 0)
    if mode == "diff":
        lam = lam_ref[...]
        lam_full = (jnp.exp(jnp.sum(lam[0:1] * lam[1:2], axis=-1, keepdims=True))
                    - jnp.exp(jnp.sum(lam[2:3] * lam[3:4], axis=-1, keepdims=True))
                    + lambda_init)
    for st in range(n_st):
        o = acc[st][:n_v] / acc[st][n_v:n_v + 1]
        if mode == "diff":
            a = o[:, :ts].T - lam_full * o[:, ts:].T
            y = _rms(a, sg_ref[...]) * (1.0 - lambda_init)
        else:
            y = jnp.concatenate([o[:, :ts], o[:, ts:]], axis=0).T
        o_ref[st * ts:(st + 1) * ts, :] = y.astype(o_ref.dtype)


def _pair_attn(qkv, *, n_pairs, k_block, v_block, mode, gb=None, lam=None, subln=None,
               lambda_init=0.0, tq=1024, ts=256, tk=512):
    b, s, _ = qkv.shape
    assert s % tq == 0 and tq % ts == 0 and s % tk == 0 and ts % LANES == 0 and tk % LANES == 0
    in_specs = [
        pl.BlockSpec((None, tq, LANES), lambda bi, p, qi: (bi, qi, p)),
        pl.BlockSpec((None, s, LANES), lambda bi, p, qi: (bi, 0, k_block(p))),
        pl.BlockSpec((None, s, LANES), lambda bi, p, qi: (bi, 0, v_block(p))),
    ]
    args = [qkv, qkv, qkv]
    if mode == "diff":
        in_specs += [
            pl.BlockSpec((None, 2, B_EBLOCKS, LANES, LANES), lambda bi, p, qi: (p, 0, 0, 0, 0)),
            pl.BlockSpec((4, HEAD_DIM), lambda bi, p, qi: (0, 0)),
            pl.BlockSpec((1, LANES), lambda bi, p, qi: (0, 0)),
        ]
        args += [gb, lam, subln]
    return pl.pallas_call(
        functools.partial(_pair_attn_kernel, tq=tq, ts=ts, tk=tk, n_kt=s // tk, mode=mode,
                          lambda_init=lambda_init),
        grid=(b, n_pairs, s // tq),
        in_specs=in_specs,
        out_specs=pl.BlockSpec((None, tq, LANES), lambda bi, p, qi: (bi, qi, p)),
        out_shape=jax.ShapeDtypeStruct((b, s, n_pairs * LANES), BF16),
        scratch_shapes=[pltpu.VMEM(((HEAD_DIM if mode == "gqa" else LANES) + ONES_ROWS, s),
                                   BF16)],
        compiler_params=_cparams(("parallel", "parallel", "arbitrary")),
        name="pair_attn_" + mode,
    )(*args)


def _window_attn_kernel(slab_ref, ga_ref, o_ref, lse_ref, *, sub_len):
    lane = lax.broadcasted_iota(jnp.int32, (LANES, LANES), 1)
    lo = lane < HEAD_DIM
    n_p = A_COLS // LANES

    def sub_block(res, sb):
        static = isinstance(sb, int)
        q0 = sb * LANES
        if static:
            start = min(max(q0 - A_SIDE, 0), sub_len - A_WIN)
            variant = 1 if q0 == 0 else (2 if q0 == sub_len - LANES else 0)
            qrows, wrows = slice(q0, q0 + LANES), slice(start, start + A_WIN)
        else:
            q0 = pl.multiple_of(q0, LANES)
            start = pl.multiple_of(jnp.clip(q0 - A_SIDE, 0, sub_len - A_WIN), A_SIDE)
            variant = jnp.where(q0 == 0, 1, jnp.where(q0 == sub_len - LANES, 2, 0))
            qrows, wrows = pl.ds(q0, LANES), pl.ds(start, A_WIN)
        lse_all = jnp.full((LANES, LANES), NEG, F32)
        for p in range(n_p):
            cols = slice(p * LANES, (p + 1) * LANES)
            q = slab_ref[res, qrows, cols]
            kw = slab_ref[res, wrows, A_COLS + p * LANES:A_COLS + (p + 1) * LANES]
            vw = slab_ref[res, wrows, 2 * A_COLS + p * LANES:2 * A_COLS + (p + 1) * LANES]
            zero = jnp.zeros_like(q)
            qs = jnp.concatenate([jnp.where(lo, q, zero), jnp.where(lo, zero, q)], axis=0)
            s = lax.dot_general(qs, kw, (((1,), (1,)), ((), ())), preferred_element_type=F32)
            s = s + jnp.concatenate([ga_ref[p, 0, variant], ga_ref[p, 1, variant]], axis=0)
            m = jnp.max(s, axis=-1, keepdims=True)
            e = jnp.exp(s - m)
            l = jnp.sum(e, axis=-1, keepdims=True)
            acc = jnp.dot(e.astype(BF16), vw, preferred_element_type=F32)
            o = acc / l
            lse = m + jnp.log(l)
            o_ref[res, qrows, cols] = jnp.where(lo, o[:LANES], o[LANES:]).astype(o_ref.dtype)
            lse_all = jnp.where(lane == 2 * p, lse[:LANES],
                                jnp.where(lane == 2 * p + 1, lse[LANES:], lse_all))
        lse_ref[res, qrows, :] = lse_all

    n_res = slab_ref.shape[0]
    n_sb = sub_len // LANES
    if n_res * n_sb <= A_UNROLL:
        for res in range(n_res):
            for sb in range(n_sb):
                sub_block(res, sb)
    else:
        def body(i, carry):
            for u in range(A_UNROLL):
                sub_block(0, A_UNROLL * i + u)
            return carry
        lax.fori_loop(0, n_sb // A_UNROLL, body, 0)


def _window_attn(slab, ga):
    b, dil, sub_len, n = slab.shape
    n_sb = sub_len // LANES
    n_res = max(1, min(dil, A_UNROLL // n_sb))
    assert n_res == 1 or n_res * n_sb <= A_UNROLL
    assert n_res > 1 or n_sb <= A_UNROLL or n_sb % A_UNROLL == 0
    out_spec = pl.BlockSpec((None, n_res, sub_len, A_COLS), lambda bi, r: (bi, r, 0, 0))
    lse_spec = pl.BlockSpec((None, n_res, sub_len, LANES), lambda bi, r: (bi, r, 0, 0))
    return pl.pallas_call(
        functools.partial(_window_attn_kernel, sub_len=sub_len),
        grid=(b, dil // n_res),
        in_specs=[
            pl.BlockSpec((None, n_res, sub_len, n), lambda bi, r: (bi, r, 0, 0)),
            pl.BlockSpec(ga.shape, lambda bi, r: (0,) * ga.ndim),
        ],
        out_specs=[out_spec, lse_spec],
        out_shape=[jax.ShapeDtypeStruct((b, dil, sub_len, A_COLS), BF16),
                   jax.ShapeDtypeStruct((b, dil, sub_len, LANES), F32)],
        compiler_params=_cparams(("parallel", "parallel")),
        name="window_attn",
    )(slab, ga)


def _post_kernel(*refs, mix_groups, final_norm, ff_chunk):
    refs = list(refs)
    h_ref = refs.pop(0)
    if mix_groups:
        og_refs = [refs.pop(0) for _ in A_GROUPS]
        lg_refs = [refs.pop(0) for _ in A_GROUPS]
    else:
        o_ref = refs.pop(0)
    wo_ref, g_ref, win_ref, wout_ref = refs[:4]
    refs = refs[4:]
    gf_ref = refs.pop(0) if final_norm else None
    out_ref = refs.pop(0)

    if mix_groups:
        o_scr, lse_scr = refs
        n_cb, tm, _ = o_scr.shape
        cb_per_group = A_COLS // LANES
        for gi, (_, dil, _) in enumerate(A_GROUPS):
            for r in range(dil):
                rows = pl.ds(r, tm // dil, stride=dil) if dil > 1 else slice(None)
                for cb in range(cb_per_group):
                    cols = slice(cb * LANES, (cb + 1) * LANES)
                    o_scr[gi * cb_per_group + cb, rows, :] = og_refs[gi][r, :, cols].astype(F32)
                lse_scr[gi, rows, :] = lg_refs[gi][r]
        lane = lax.broadcasted_iota(jnp.int32, (tm, LANES), 1)
        real = [lane < nh for (_, _, nh) in A_GROUPS]
        lse = [jnp.where(real[gi], lse_scr[gi], NEG) for gi in range(len(A_GROUPS))]
        mx = jnp.max(jnp.maximum(jnp.maximum(lse[0], lse[1]), lse[2]), axis=-1, keepdims=True)
        s_g = [jnp.sum(jnp.where(real[gi], jnp.exp(lse[gi] - mx), 0.0), axis=-1, keepdims=True)
               / A_GROUPS[gi][2] for gi in range(len(A_GROUPS))]
        tot = s_g[0] + s_g[1] + s_g[2]
        parts = []
        for gi in range(len(A_GROUPS)):
            scale = len(A_GROUPS) * (s_g[gi] / tot)
            parts += [(o_scr[gi * cb_per_group + cb] * scale).astype(BF16)
                      for cb in range(cb_per_group)]
        o = jnp.concatenate(parts, axis=1)
    else:
        o = o_ref[...]
    h1 = h_ref[...] + jnp.dot(o, wo_ref[...], preferred_element_type=F32)
    xn = _rms(h1, g_ref[...]).astype(BF16)
    acc = h1
    for c in range(0, D_FF, ff_chunk):
        u = jnp.dot(xn, win_ref[:, c:c + ff_chunk], preferred_element_type=F32)
        u = jnp.maximum(u, 0.0)
        u = (u * u).astype(BF16)
        acc = acc + jnp.dot(u, wout_ref[c:c + ff_chunk, :], preferred_element_type=F32)
    if final_norm:
        acc = _rms(acc, gf_ref[...])
    out_ref[...] = acc


def _post(h, o, w_o, g_mlp, w_in, w_out, *, lse=None, g_final=None, seq=None, tm=512,
          ff_chunk=1024):
    t, d = h.shape
    assert t % tm == 0 and D_FF % ff_chunk == 0 and (lse is None or seq % tm == 0)
    const = lambda i: (0, 0)
    single = dict(pipeline_mode=pl.Buffered(1))
    in_specs = [pl.BlockSpec((tm, d), lambda i: (i, 0))]
    args = [h]
    scratch = []
    if lse is not None:
        n_o = len(A_GROUPS) * A_COLS
        spb = seq // tm
        for arrs in (o, lse):
            for arr, (_, dil, _) in zip(arrs, A_GROUPS):
                in_specs.append(pl.BlockSpec((None, dil, tm // dil, arr.shape[-1]),
                                             lambda i: (i // spb, 0, i % spb, 0)))
                args.append(arr)
        scratch = [pltpu.VMEM((n_o // LANES, tm, LANES), F32),
                   pltpu.VMEM((len(A_GROUPS), tm, LANES), F32)]
    else:
        n_o = o.shape[1]
        in_specs.append(pl.BlockSpec((tm, n_o), lambda i: (i, 0)))
        args.append(o)
    in_specs += [pl.BlockSpec((n_o, d), const, **single),
                 pl.BlockSpec((1, d), const),
                 pl.BlockSpec((d, D_FF), const, **single),
                 pl.BlockSpec((D_FF, d), const, **single)]
    n_act = len(args)
    args += [w_o, g_mlp, w_in, w_out]
    if g_final is not None:
        in_specs.append(pl.BlockSpec((1, d), const))
        args.append(g_final)
    fuse = [n_act <= a < n_act + 4 and a != n_act + 1 for a in range(len(args))]
    return pl.pallas_call(
        functools.partial(_post_kernel, mix_groups=lse is not None,
                          final_norm=g_final is not None, ff_chunk=ff_chunk),
        grid=(t // tm,),
        in_specs=in_specs,
        out_specs=pl.BlockSpec((tm, d), lambda i: (i, 0)),
        out_shape=jax.ShapeDtypeStruct((t, d), F32),
        scratch_shapes=scratch,
        compiler_params=_cparams(("parallel",), fuse_inputs=fuse),
        name="post",
    )(*args)


def _t5_bucket(rel):
    nb = NUM_BUCKETS // 2
    max_exact = nb // 2
    side = jnp.where(rel > 0, nb, 0)
    n = jnp.abs(rel)
    nf = jnp.maximum(n, 1).astype(F32)
    large = max_exact + (jnp.log(nf / max_exact) / math.log(REL_MAX_DISTANCE / max_exact)
                         * (nb - max_exact)).astype(jnp.int32)
    large = jnp.minimum(large, nb - 1)
    return side + jnp.where(n < max_exact, n, large)


def _table_lookup(bucket, tab_ref, col):
    vals = [tab_ref[b, col] for b in range(NUM_BUCKETS)]
    bit = 1
    while len(vals) > 1:
        odd = (bucket & bit) != 0
        vals = [jnp.where(odd, vals[i + 1], vals[i]) for i in range(0, len(vals), 2)]
        bit *= 2
    return vals[0]


def _diff_bias_kernel(tab_ref, o_ref):
    hj = pl.program_id(0)
    r = lax.broadcasted_iota(jnp.int32, (LANES, LANES), 0)
    c = lax.broadcasted_iota(jnp.int32, (LANES, LANES), 1)
    for e in range(B_EBLOCKS):
        rel = LANES * (e - B_ECLIP) + r - c
        o_ref[e] = _table_lookup(_t5_bucket(rel), tab_ref, hj) * LOG2E


def _diff_bias_blocks(rel_bias):
    n = rel_bias.shape[1]
    out = pl.pallas_call(
        _diff_bias_kernel,
        grid=(n,),
        in_specs=[pl.BlockSpec(memory_space=pltpu.SMEM)],
        out_specs=pl.BlockSpec((None, B_EBLOCKS, LANES, LANES), lambda i: (i, 0, 0, 0)),
        out_shape=jax.ShapeDtypeStruct((n, B_EBLOCKS, LANES, LANES), F32),
        compiler_params=_cparams(("parallel",)),
        name="diff_bias",
    )(rel_bias.astype(F32))
    return out.reshape(n // 2, 2, B_EBLOCKS, LANES, LANES)


A_WINDOW_OFFSETS = (-A_SIDE, 0, -LANES)


def _window_bias_kernel(tab_ref, o_ref):
    hp = pl.program_id(0)
    dil = jnp.where(hp < A_PAD_HEADS, A_GROUPS[0][1],
                    jnp.where(hp < 2 * A_PAD_HEADS, A_GROUPS[1][1], A_GROUPS[2][1]))
    r = lax.broadcasted_iota(jnp.int32, (LANES, A_WIN), 0)
    c = lax.broadcasted_iota(jnp.int32, (LANES, A_WIN), 1)
    for v, off in enumerate(A_WINDOW_OFFSETS):
        rel = c + off - r
        bias = _table_lookup(_t5_bucket(rel * dil), tab_ref, hp)
        o_ref[v] = jnp.where(jnp.abs(rel) <= A_SIDE, bias, NEG)


def _window_bias_tiles(rel_bias):
    cols = _pad_group_cols(rel_bias.astype(F32), 1, unit=1)
    table = jnp.concatenate(cols, axis=1)
    n = table.shape[1]
    out = pl.pallas_call(
        _window_bias_kernel,
        grid=(n,),
        in_specs=[pl.BlockSpec(memory_space=pltpu.SMEM)],
        out_specs=pl.BlockSpec((None, 3, LANES, A_WIN), lambda i: (i, 0, 0, 0)),
        out_shape=jax.ShapeDtypeStruct((n, 3, LANES, A_WIN), F32),
        compiler_params=_cparams(("parallel",)),
        name="window_bias",
    )(table)
    return out.reshape(len(A_GROUPS), A_PAD_HEADS // 2, 2, 3, LANES, A_WIN)


def _rope_tables(seq):
    n_rows = seq // GRID_W
    row = jnp.repeat(jnp.arange(n_rows, dtype=jnp.int32), GRID_W).astype(F32)
    col = jnp.tile(jnp.arange(GRID_W, dtype=jnp.int32), n_rows).astype(F32)
    half = ROPE_AXIS_DIM // 2
    inv = ROPE_THETA ** (-jnp.arange(half, dtype=F32) / half)
    ang = jnp.concatenate([row[:, None] * inv, col[:, None] * inv], axis=-1)
    lane = jnp.arange(LANES, dtype=jnp.int32) % HEAD_DIM
    idx = (lane // ROPE_AXIS_DIM) * half + lane % half
    sign = jnp.where((lane % ROPE_AXIS_DIM) < half, -1.0, 1.0).astype(F32)
    return jnp.cos(ang)[:, idx], jnp.sin(ang)[:, idx] * sign


def _pad_group_cols(w, axis, unit=HEAD_DIM):
    parts = []
    h0 = 0
    for (_, _, nh) in A_GROUPS:
        sl = [slice(None)] * w.ndim
        sl[axis] = slice(h0 * unit, (h0 + nh) * unit)
        part = w[tuple(sl)]
        if nh < A_PAD_HEADS:
            pad = [(0, 0)] * w.ndim
            pad[axis] = (0, (A_PAD_HEADS - nh) * unit)
            part = jnp.pad(part, pad)
        parts.append(part)
        h0 += nh
    return parts


def _dilated_layer(h, g_mix, w_qkv, w_o, bias_tiles, batch, seq):
    d_attn = w_qkv.shape[1] // 3
    w_qkv = w_qkv.astype(BF16)
    wq, wk, wv = (w_qkv[:, i * d_attn:(i + 1) * d_attn] for i in range(3))
    qs, ks, vs = _pad_group_cols(wq, 1), _pad_group_cols(wk, 1), _pad_group_cols(wv, 1)
    w = jnp.concatenate([jnp.concatenate([qs[g], ks[g], vs[g]], axis=1) for g in range(3)],
                        axis=1)
    scale = jnp.tile(jnp.concatenate([jnp.full((A_COLS,), Q_SCALE, F32),
                                      jnp.ones((2 * A_COLS,), F32)]), 3)[None, :]
    slabs = _norm_proj_groups(h, g_mix, w, scale, batch=batch, seq=seq)
    outs, lses = [], []
    for g in range(len(A_GROUPS)):
        o, lse = _window_attn(slabs[g], bias_tiles[g])
        outs.append(o)
        lses.append(lse)
    w_o_pad = jnp.concatenate(_pad_group_cols(w_o.astype(BF16), 0), axis=0)
    return outs, lses, w_o_pad


def kernel(x, rel_bias, norm_mix_g, norm_mlp_g, norm_final_g, a_w_qkv, a_w_o, b_w_qkv,
           b_lambda, b_subln_g, b_w_o, c_w_qkv, c_q_norm_g, c_k_norm_g, c_w_o, mlp_w_in,
           mlp_w_out):
    batch, seq, d = x.shape
    t = batch * seq
    h = x.reshape(t, d)
    cos, sin = _rope_tables(seq)
    window_bias = _window_bias_tiles(rel_bias)
    for i in range(N_LAYERS):
        kind, j = i % 3, i // 3
        g_mix = norm_mix_g[i][None, :]
        lse = None
        if kind == 0:
            o, lse, w_o = _dilated_layer(h, g_mix, a_w_qkv[j], a_w_o[j], window_bias, batch, seq)
        elif kind == 1:
            scale = jnp.concatenate([jnp.full((d,), Q_SCALE_LOG2, F32),
                                     jnp.ones((2 * d,), F32)])[None, :]
            qkv = _norm_proj(h, g_mix, b_w_qkv[j].astype(BF16), scale, chunk=1024)
            n_pairs = d // LANES
            o = _pair_attn(qkv.reshape(batch, seq, 3 * d), n_pairs=n_pairs,
                           k_block=lambda p: n_pairs + p, v_block=lambda p: 2 * n_pairs + p,
                           mode="diff", gb=_diff_bias_blocks(rel_bias), lam=b_lambda[j],
                           subln=b_subln_g[j][None, :], lambda_init=_lambda_init(i))
            o = o.reshape(t, d)
            w_o = b_w_o[j].astype(BF16)
        else:
            n_q = d
            n_kv = C_KV_HEADS * HEAD_DIM
            qkv = _norm_proj_rope(h, g_mix, c_w_qkv[j].astype(BF16),
                                  jnp.tile(c_q_norm_g[j], 2)[None, :],
                                  jnp.tile(c_k_norm_g[j], 2)[None, :], cos, sin,
                                  n_q=n_q, n_k=n_kv, seq=seq, chunk=512)
            n_pairs = n_q // LANES
            kv_blocks = n_kv // LANES
            o = _pair_attn(qkv.reshape(batch, seq, n_q + 2 * n_kv), n_pairs=n_pairs,
                           k_block=lambda p: n_pairs + p // 4,
                           v_block=lambda p: n_pairs + kv_blocks + p // 4,
                           mode="gqa", tq=2048, ts=512)
            o = o.reshape(t, d)
            w_o = c_w_o[j].astype(BF16)
        g_final = norm_final_g[None, :] if i == N_LAYERS - 1 else None
        h = _post(h, o, w_o, norm_mlp_g[i][None, :], mlp_w_in[i].astype(BF16),
                  mlp_w_out[i].astype(BF16), lse=lse, g_final=g_final, seq=seq)
    return h.reshape(batch, seq, d)
```
